```python
import math
import jax, jax.numpy as jnp
from jax import lax
import numpy as np

D_MODEL = 1024
BATCH = 2
SEQ = 8192
DEPTH = 1

D_FF = 2816
RMS_EPS = 1e-6
SUBLN_EPS = 1e-5
CONV_WIDTH = 4
DN_HEADS = 8
DN_HEAD_DIM = 64
DN_WIDTH = DN_HEADS * DN_HEAD_DIM
DN_CHUNK = 64
DIFF_HEADS = 4
DIFF_HEAD_DIM = 64
DIFF_V_DIM = 2 * DIFF_HEAD_DIM
DIFF_QK_WIDTH = DIFF_HEADS * 2 * DIFF_HEAD_DIM
DIFF_V_WIDTH = DIFF_HEADS * DIFF_V_DIM
Q_BLOCK = 128
N_BRANCHES = 2
IN_SPLIT_SIZES = (3 * DN_WIDTH, DN_WIDTH, DN_HEADS, DN_HEADS, DIFF_QK_WIDTH, DIFF_QK_WIDTH, DIFF_V_WIDTH, N_BRANCHES * D_MODEL)
W_IN_COLS = sum(IN_SPLIT_SIZES)

kernel_name = "hybrid_gdn_diffattn_macaron"


def lambda_init(layer_idx):
    return 0.8 - 0.6 * math.exp(-0.3 * layer_idx)


def rms_norm(x, g, eps=RMS_EPS):
    xf = x.astype(jnp.float32)
    y = xf * lax.rsqrt(jnp.mean(xf * xf, axis=-1, keepdims=True) + eps)
    return (y * g.astype(jnp.float32)).astype(x.dtype)


def l2_normalize(x, eps=1e-6):
    xf = x.astype(jnp.float32)
    return xf * lax.rsqrt(jnp.sum(xf * xf, axis=-1, keepdims=True) + eps)


def swiglu_ffn(x, w_gate, w_up, w_down):
    return (jax.nn.silu(x @ w_gate) * (x @ w_up)) @ w_down


def causal_depthwise_conv(x, w):
    width = w.shape[0]
    return lax.conv_general_dilated(
        x, w[:, None, :], window_strides=(1,), padding=[(width - 1, 0)],
        dimension_numbers=('NWC', 'WIO', 'NWC'), feature_group_count=x.shape[-1])


def gated_delta_rule_chunked(q, k, v, g, beta):
    B, H, T, dk = q.shape
    dv = v.shape[-1]
    C = DN_CHUNK
    N = T // C
    f32 = jnp.float32
    q = (q.astype(f32) * dk ** -0.5).reshape(B, H, N, C, dk)
    k = k.astype(f32).reshape(B, H, N, C, dk)
    v = v.astype(f32).reshape(B, H, N, C, dv)
    g = g.astype(f32).reshape(B, H, N, C)
    beta = beta.astype(f32).reshape(B, H, N, C)
    gc = jnp.cumsum(g, axis=-1)
    incl = jnp.tril(jnp.ones((C, C), dtype=bool))
    strict = jnp.tril(jnp.ones((C, C), dtype=bool), k=-1)
    rel = gc[..., :, None] - gc[..., None, :]
    decay = jnp.where(incl, jnp.exp(jnp.where(incl, rel, 0.0)), 0.0)
    kb = k * beta[..., None]
    a_strict = jnp.where(strict, jnp.einsum('bhncd,bhnsd->bhncs', kb, k) * decay, 0.0)
    eye = jnp.eye(C, dtype=f32)
    t_inv = lax.linalg.triangular_solve(eye + a_strict, jnp.broadcast_to(eye, a_strict.shape),
                                        left_side=True, lower=True, unit_diagonal=True)
    u = jnp.einsum('bhncs,bhnse->bhnce', t_inv, v * beta[..., None])
    w = jnp.einsum('bhncs,bhnsd->bhncd', t_inv, kb * jnp.exp(gc)[..., None])
    attn = jnp.einsum('bhncd,bhnsd->bhncs', q, k) * decay
    q_dec = q * jnp.exp(gc)[..., None]
    k_dec = k * jnp.exp(gc[..., -1:] - gc)[..., None]
    chunk_decay = jnp.exp(gc[..., -1])

    def step(state, xs):
        u_n, w_n, q_n, k_n, attn_n, cd_n = xs
        v_new = u_n - jnp.einsum('bhcd,bhde->bhce', w_n, state)
        o_n = jnp.einsum('bhcd,bhde->bhce', q_n, state) + jnp.einsum('bhcs,bhse->bhce', attn_n, v_new)
        state = state * cd_n[..., None, None] + jnp.einsum('bhcd,bhce->bhde', k_n, v_new)
        return state, o_n

    def to_front(t):
        return jnp.moveaxis(t, 2, 0)

    s0 = jnp.zeros((B, H, dk, dv), f32)
    _, o = lax.scan(step, s0, (to_front(u), to_front(w), to_front(q_dec), to_front(k_dec),
                               to_front(attn), to_front(chunk_decay)))
    return jnp.moveaxis(o, 0, 2).reshape(B, H, T, dv)


def differential_attention(q1, q2, k1, k2, v, lam, slopes):
    B, H, T, d = q1.shape
    nb = T // Q_BLOCK
    f32 = jnp.float32
    scale = d ** -0.5
    k1f, k2f, vf = k1.astype(f32), k2.astype(f32), v.astype(f32)
    kpos = jnp.arange(T, dtype=jnp.int32)

    def probs(qb, kf, bias, causal):
        s = jnp.einsum('bhqd,bhkd->bhqk', qb, kf) * scale + bias
        s = jnp.where(causal, s, -jnp.inf)
        return jax.nn.softmax(s, axis=-1)

    def block(args):
        q1b, q2b, start = args
        qpos = start + jnp.arange(Q_BLOCK, dtype=jnp.int32)
        dist = qpos[:, None] - kpos[None, :]
        causal = dist >= 0
        bias = -slopes[:, None, None] * dist.astype(f32)[None]
        p = probs(q1b.astype(f32), k1f, bias, causal) - lam * probs(q2b.astype(f32), k2f, bias, causal)
        return jnp.einsum('bhqk,bhke->bhqe', p, vf)

    def to_blocks(t):
        return t.reshape(B, H, nb, Q_BLOCK, d).transpose(2, 0, 1, 3, 4)

    starts = jnp.arange(nb, dtype=jnp.int32) * Q_BLOCK
    out = lax.map(block, (to_blocks(q1), to_blocks(q2), starts))
    return out.transpose(1, 2, 0, 3, 4).reshape(B, H, T, v.shape[-1])


def setup_inputs(seed: int = 0) -> dict:
    key = jax.random.key(seed)
    ks = jax.random.split(key, 24)
    f32 = jnp.float32
    L = DEPTH

    def normal(k, shape, fan_in):
        return jax.random.normal(k, shape, f32) * fan_in ** -0.5

    def gain(k, shape):
        return 1.0 + 0.02 * jax.random.normal(k, shape, f32)

    x = jax.random.normal(ks[0], (BATCH, SEQ, D_MODEL), f32)
    ffn1_norm = gain(ks[1], (L, D_MODEL))
    ffn1_w_gate = normal(ks[2], (L, D_MODEL, D_FF), D_MODEL)
    ffn1_w_up = normal(ks[3], (L, D_MODEL, D_FF), D_MODEL)
    ffn1_w_down = normal(ks[4], (L, D_FF, D_MODEL), D_FF)
    mix_norm = gain(ks[5], (L, D_MODEL))
    w_in = normal(ks[6], (L, D_MODEL, W_IN_COLS), D_MODEL)
    conv_qkv = normal(ks[7], (L, CONV_WIDTH, 3 * DN_WIDTH), CONV_WIDTH)
    dn_a_log = jnp.log(jax.random.uniform(ks[8], (L, DN_HEADS), f32, 1.0, 16.0))
    dt = jnp.exp(jax.random.uniform(ks[9], (L, DN_HEADS), f32, math.log(1e-3), math.log(1e-1)))
    dn_dt_bias = dt + jnp.log(-jnp.expm1(-dt))
    dn_out_norm = gain(ks[10], (L, DN_HEAD_DIM))
    diff_lambda_q1 = 0.1 * jax.random.normal(ks[11], (L, DIFF_HEAD_DIM), f32)
    diff_lambda_k1 = 0.1 * jax.random.normal(ks[12], (L, DIFF_HEAD_DIM), f32)
    diff_lambda_q2 = 0.1 * jax.random.normal(ks[13], (L, DIFF_HEAD_DIM), f32)
    diff_lambda_k2 = 0.1 * jax.random.normal(ks[14], (L, DIFF_HEAD_DIM), f32)
    diff_subln = gain(ks[15], (L, DIFF_V_DIM))
    w_branch_a = normal(ks[16], (L, DN_WIDTH, D_MODEL), DN_WIDTH)
    w_branch_b = normal(ks[17], (L, DIFF_V_WIDTH, D_MODEL), DIFF_V_WIDTH)
    w_out = normal(ks[18], (L, D_MODEL, D_MODEL), D_MODEL)
    ffn2_norm = gain(ks[19], (L, D_MODEL))
    ffn2_w_gate = normal(ks[20], (L, D_MODEL, D_FF), D_MODEL)
    ffn2_w_up = normal(ks[21], (L, D_MODEL, D_FF), D_MODEL)
    ffn2_w_down = normal(ks[22], (L, D_FF, D_MODEL), D_FF)
    final_norm = gain(ks[23], (D_MODEL,))
    return {"x": x, "ffn1_norm": ffn1_norm, "ffn1_w_gate": ffn1_w_gate, "ffn1_w_up": ffn1_w_up,
            "ffn1_w_down": ffn1_w_down, "mix_norm": mix_norm, "w_in": w_in, "conv_qkv": conv_qkv,
            "dn_a_log": dn_a_log, "dn_dt_bias": dn_dt_bias, "dn_out_norm": dn_out_norm,
            "diff_lambda_q1": diff_lambda_q1, "diff_lambda_k1": diff_lambda_k1,
            "diff_lambda_q2": diff_lambda_q2, "diff_lambda_k2": diff_lambda_k2, "diff_subln": diff_subln,
            "w_branch_a": w_branch_a, "w_branch_b": w_branch_b, "w_out": w_out,
            "ffn2_norm": ffn2_norm, "ffn2_w_gate": ffn2_w_gate, "ffn2_w_up": ffn2_w_up,
            "ffn2_w_down": ffn2_w_down, "final_norm": final_norm}


def reference(x, ffn1_norm, ffn1_w_gate, ffn1_w_up, ffn1_w_down, mix_norm, w_in, conv_qkv,
              dn_a_log, dn_dt_bias, dn_out_norm, diff_lambda_q1, diff_lambda_k1, diff_lambda_q2,
              diff_lambda_k2, diff_subln, w_branch_a, w_branch_b, w_out, ffn2_norm, ffn2_w_gate,
              ffn2_w_up, ffn2_w_down, final_norm):
    B, T, _ = x.shape
    f32 = jnp.float32
    split_points = [int(s) for s in np.cumsum(IN_SPLIT_SIZES)[:-1]]
    slopes = jnp.asarray([2.0 ** (-8.0 * (i + 1) / DIFF_HEADS) for i in range(DIFF_HEADS)], f32)

    for l in range(DEPTH):
        x = x + 0.5 * swiglu_ffn(rms_norm(x, ffn1_norm[l]), ffn1_w_gate[l], ffn1_w_up[l], ffn1_w_down[l])

        h = rms_norm(x, mix_norm[l])
        proj = h @ w_in[l]
        qkv_dn, z_dn, a_dn, b_dn, q_df, k_df, v_df, gate_logits = jnp.split(proj, split_points, axis=-1)

        qkv_dn = jax.nn.silu(causal_depthwise_conv(qkv_dn, conv_qkv[l]))
        q_dn, k_dn, v_dn = jnp.split(qkv_dn, 3, axis=-1)

        def dn_heads(t):
            return t.reshape(B, T, DN_HEADS, DN_HEAD_DIM).transpose(0, 2, 1, 3)

        q_dn = l2_normalize(dn_heads(q_dn))
        k_dn = l2_normalize(dn_heads(k_dn))
        v_dn = dn_heads(v_dn)
        g_dn = -jnp.exp(dn_a_log[l].astype(f32)) * jax.nn.softplus(a_dn.astype(f32) + dn_dt_bias[l].astype(f32))
        beta_dn = jax.nn.sigmoid(b_dn.astype(f32))
        o_dn = gated_delta_rule_chunked(q_dn, k_dn, v_dn, g_dn.transpose(0, 2, 1), beta_dn.transpose(0, 2, 1))
        o_dn = o_dn.transpose(0, 2, 1, 3).astype(x.dtype)
        o_dn = rms_norm(o_dn, dn_out_norm[l]) * jax.nn.silu(z_dn.reshape(B, T, DN_HEADS, DN_HEAD_DIM))
        y_a = o_dn.reshape(B, T, DN_WIDTH) @ w_branch_a[l]

        q_df = q_df.reshape(B, T, DIFF_HEADS, 2, DIFF_HEAD_DIM).transpose(3, 0, 2, 1, 4)
        k_df = k_df.reshape(B, T, DIFF_HEADS, 2, DIFF_HEAD_DIM).transpose(3, 0, 2, 1, 4)
        v_df = v_df.reshape(B, T, DIFF_HEADS, DIFF_V_DIM).transpose(0, 2, 1, 3)
        lam_init = lambda_init(l)
        lam = (jnp.exp(jnp.sum(diff_lambda_q1[l].astype(f32) * diff_lambda_k1[l].astype(f32)))
               - jnp.exp(jnp.sum(diff_lambda_q2[l].astype(f32) * diff_lambda_k2[l].astype(f32)))
               + lam_init)
        o_df = differential_attention(q_df[0], q_df[1], k_df[0], k_df[1], v_df, lam, slopes)
        o_df = o_df.transpose(0, 2, 1, 3).astype(x.dtype)
        o_df = rms_norm(o_df, diff_subln[l], eps=SUBLN_EPS) * (1.0 - lam_init)
        y_b = o_df.reshape(B, T, DIFF_V_WIDTH) @ w_branch_b[l]

        gate_a, gate_b = jnp.split(jax.nn.sigmoid(gate_logits), 2, axis=-1)
        x = x + (gate_a * y_a + gate_b * y_b) @ w_out[l]

        x = x + 0.5 * swiglu_ffn(rms_norm(x, ffn2_norm[l]), ffn2_w_gate[l], ffn2_w_up[l], ffn2_w_down[l])

    return rms_norm(x, final_norm)
```

```python
import functools
import math

import jax
import jax.numpy as jnp
from jax import lax
from jax.experimental import pallas as pl
from jax.experimental.pallas import tpu as pltpu

F32 = jnp.float32
BF16 = jnp.bfloat16
HIGHEST = lax.Precision.HIGHEST

D_MODEL = 1024
D_FF = 2816
RMS_EPS = 1e-6
SUBLN_EPS = 1e-5
L2_EPS = 1e-6
CONV_WIDTH = 4
DN_HEADS = 8
DN_HEAD_DIM = 64
DN_WIDTH = DN_HEADS * DN_HEAD_DIM
DN_CHUNK = 64
DN_PAIRS = DN_HEADS // 2
DIFF_HEADS = 4
DIFF_HEAD_DIM = 64
DIFF_V_DIM = 2 * DIFF_HEAD_DIM
DIFF_WIDTH = DIFF_HEADS * DIFF_V_DIM
LANES = 128
SUBLANES = 8

VMEM_LIMIT_BYTES = 56 * 1024 * 1024

FFN_ROWS = 512
FF_CHUNKS = ((0, 512), (512, 1024), (1024, 1536), (1536, 2048), (2048, 2560), (2560, 2816))
PROJ_ROWS = 512
GDN_ROWS = 512
ATTN_Q = 512
ATTN_K = 512
MERGE_ROWS = 512


def _rms(x, g, eps):
    return x * lax.rsqrt(jnp.mean(x * x, axis=-1, keepdims=True) + eps) * g


def _sigmoid(x):
    return 1.0 / (1.0 + jnp.exp(-x))


def _dot(a, b, precision=None):
    return jnp.dot(a, b, preferred_element_type=F32, precision=precision)


def _dot_nt(a, b, precision=None):
    return lax.dot_general(a, b, (((1,), (1,)), ((), ())), preferred_element_type=F32, precision=precision)


def _resident(shape):
    return pl.BlockSpec(shape, lambda *_: (0,) * len(shape), pipeline_mode=pl.Buffered(1))


def _ffn_kernel(x_ref, g_ref, wg_ref, wu_ref, wd_ref, fg_ref, o_ref, *, final_norm):
    x = x_ref[...]
    h = _rms(x, g_ref[...], RMS_EPS).astype(BF16)
    acc = None
    for c0, c1 in FF_CHUNKS:
        gate = _dot(h, wg_ref[:, c0:c1])
        up = _dot(h, wu_ref[:, c0:c1])
        act = (gate * _sigmoid(gate) * up).astype(BF16)
        part = _dot(act, wd_ref[c0:c1, :])
        acc = part if acc is None else acc + part
    y = x + 0.5 * acc
    if final_norm:
        y = _rms(y, fg_ref[...], RMS_EPS)
    o_ref[...] = y


def _ffn(x, g, wg, wu, wd, fg, final_norm):
    n = x.shape[0]
    row = pl.BlockSpec((FFN_ROWS, D_MODEL), lambda i: (i, 0))
    return pl.pallas_call(
        functools.partial(_ffn_kernel, final_norm=final_norm),
        grid=(n // FFN_ROWS,),
        in_specs=[row, _resident((1, D_MODEL)), _resident((D_MODEL, D_FF)), _resident((D_MODEL, D_FF)),
                  _resident((D_FF, D_MODEL)), _resident((1, D_MODEL))],
        out_specs=row,
        out_shape=jax.ShapeDtypeStruct((n, D_MODEL), F32),
        compiler_params=pltpu.CompilerParams(dimension_semantics=("arbitrary",),
                                             vmem_limit_bytes=VMEM_LIMIT_BYTES),
        name="ffn",
    )(x, g, wg, wu, wd, fg)


def _proj_kernel(x_ref, g_ref, wqkv_ref, wz_ref, wab_ref, wdf_ref, cw_ref,
                 qkv_ref, z_ref, ab_ref, df_ref, ext_ref, *, tiles_per_seq):
    halo = SUBLANES

    @pl.when(pl.program_id(0) % tiles_per_seq == 0)
    def _():
        ext_ref[0:halo, :] = jnp.zeros((halo, 3 * DN_WIDTH), F32)

    h = _rms(x_ref[...], g_ref[...], RMS_EPS).astype(BF16)
    pre = _dot(h, wqkv_ref[...])
    ext_ref[halo:halo + PROJ_ROWS, :] = pre
    cw = cw_ref[...]
    y = cw[CONV_WIDTH - 1:CONV_WIDTH, :] * pre
    for j in range(CONV_WIDTH - 1):
        y = y + cw[j:j + 1, :] * ext_ref[pl.ds(halo - (CONV_WIDTH - 1) + j, PROJ_ROWS), :]
    ext_ref[0:halo, :] = ext_ref[PROJ_ROWS:PROJ_ROWS + halo, :]
    qkv_ref[...] = y * _sigmoid(y)
    z_ref[...] = _dot(h, wz_ref[...]).astype(BF16)
    ab_ref[...] = _dot(h, wab_ref[...])
    df_ref[...] = _dot(h, wdf_ref[...]).astype(BF16)


def _proj(x1, g, wqkv, wz, wab, wdf, cw, seq):
    n = x1.shape[0]
    row = lambda w: pl.BlockSpec((PROJ_ROWS, w), lambda i: (i, 0))
    return pl.pallas_call(
        functools.partial(_proj_kernel, tiles_per_seq=seq // PROJ_ROWS),
        grid=(n // PROJ_ROWS,),
        in_specs=[row(D_MODEL), _resident((1, D_MODEL)), _resident(wqkv.shape), _resident(wz.shape),
                  _resident(wab.shape), _resident(wdf.shape), _resident(cw.shape)],
        out_specs=[row(3 * DN_WIDTH), row(DN_WIDTH), row(LANES), row(3 * DIFF_WIDTH)],
        out_shape=[jax.ShapeDtypeStruct((n, 3 * DN_WIDTH), F32),
                   jax.ShapeDtypeStruct((n, DN_WIDTH), BF16),
                   jax.ShapeDtypeStruct((n, LANES), F32),
                   jax.ShapeDtypeStruct((n, 3 * DIFF_WIDTH), BF16)],
        scratch_shapes=[pltpu.VMEM((PROJ_ROWS + SUBLANES, 3 * DN_WIDTH), F32)],
        compiler_params=pltpu.CompilerParams(dimension_semantics=("arbitrary",),
                                             vmem_limit_bytes=VMEM_LIMIT_BYTES),
        name="proj",
    )(x1, g, wqkv, wz, wab, wdf, cw)


def _gdn_kernel(q_ref, k_ref, v_ref, ab_ref, z_ref, alog_ref, dtb_ref, onorm_ref, o_ref, s_ref):
    c = DN_CHUNK
    hd = DN_HEAD_DIM
    pair = pl.program_id(1)

    @pl.when(pl.program_id(2) == 0)
    def _():
        s_ref[...] = jnp.zeros((LANES, LANES), F32)

    row = lax.broadcasted_iota(jnp.int32, (c, LANES), 0)
    lane = lax.broadcasted_iota(jnp.int32, (c, LANES), 1)
    even = lane < hd
    col = lane & (hd - 1)
    incl = col <= row
    strict = col < row
    same16 = (row >> 4) == (col >> 4)
    same32 = (row >> 5) == (col >> 5)
    eye = jnp.where(col == row, 1.0, 0.0).astype(F32)
    head_first_lane = jnp.where(col == 0, 1.0, 0.0).astype(F32)
    r2 = lax.broadcasted_iota(jnp.int32, (LANES, LANES), 0)
    l2 = lax.broadcasted_iota(jnp.int32, (LANES, LANES), 1)
    same_head = (r2 >> 6) == (l2 >> 6)
    head_ones = jnp.where(same_head, 1.0, 0.0).astype(F32)
    ident = jnp.where(r2 == l2, 1.0, 0.0).astype(F32)
    expand_a = jnp.where(r2 == 2 * pair + (l2 >> 6), 1.0, 0.0).astype(F32)
    expand_b = jnp.where(r2 == DN_HEADS + 2 * pair + (l2 >> 6), 1.0, 0.0).astype(F32)
    rc = lax.broadcasted_iota(jnp.int32, (c, c), 0)
    cc = lax.broadcasted_iota(jnp.int32, (c, c), 1)
    lower_ones = jnp.where(cc <= rc, 1.0, 0.0).astype(F32)

    def stack(x):
        return jnp.concatenate([jnp.where(even, x, 0.0), jnp.where(even, 0.0, x)], axis=0)

    def mm(a, b):
        return _dot(a.astype(BF16), b.astype(BF16))

    def mm_nt(a, b):
        return _dot_nt(a.astype(BF16), b.astype(BF16))

    ab = ab_ref[...]
    a_lanes = _dot(ab, expand_a, HIGHEST)
    b_lanes = _dot(ab, expand_b, HIGHEST)
    sp_in = a_lanes + dtb_ref[0, 0:1, :]
    softplus = jnp.maximum(sp_in, 0.0) + jnp.log(1.0 + jnp.exp(-jnp.abs(sp_in)))
    g_all = -jnp.exp(alog_ref[0, 0:1, :]) * softplus
    beta_all = _sigmoid(b_lanes)
    q_all = q_ref[...]
    k_all = k_ref[...]
    q_all = q_all * lax.rsqrt(_dot(q_all * q_all, head_ones, HIGHEST) + L2_EPS) * (hd ** -0.5)
    k_all = k_all * lax.rsqrt(_dot(k_all * k_all, head_ones, HIGHEST) + L2_EPS)
    v_all = v_ref[...]

    state = s_ref[...]
    outs = []
    for n in range(GDN_ROWS // c):
        sl = slice(n * c, (n + 1) * c)
        q, k, v, g, beta = q_all[sl], k_all[sl], v_all[sl], g_all[sl], beta_all[sl]
        gc = _dot(lower_ones, g, HIGHEST)
        gc_last = gc[c - 1:c, :]
        exp_gc = jnp.exp(gc)
        kb = k * beta
        k_stack = stack(k)
        gc_row = _dot_nt(head_first_lane, stack(gc), HIGHEST)
        rel = gc - gc_row
        decay = jnp.where(incl, jnp.exp(jnp.where(incl, rel, 0.0)), 0.0)
        a_mat = jnp.where(strict, mm_nt(kb, k_stack) * decay, 0.0)
        attn = mm_nt(q, k_stack) * decay

        neg_d = jnp.where(strict & same16, -a_mat, 0.0)
        t_inv = eye + neg_d
        power = mm(neg_d, stack(neg_d))
        for _ in range(2):
            both = mm(jnp.concatenate([power, t_inv], axis=0), stack(power))
            power = both[:c]
            t_inv = t_inv + both[c:]
        t_inv = t_inv + mm(t_inv, stack(power))
        for off in (jnp.where(strict & same32 & ~same16, a_mat, 0.0),
                    jnp.where(strict & ~same32, a_mat, 0.0)):
            t_inv = t_inv - mm(mm(t_inv, stack(off)), stack(t_inv))

        uw = mm(t_inv, jnp.concatenate([stack(v * beta), stack(kb * exp_gc)], axis=1))
        u = uw[:, :LANES]
        w = uw[:, LANES:]
        q_dec = q * exp_gc
        k_dec = k * jnp.exp(gc_last - gc)
        chunk_decay = jnp.exp(gc_last)

        ws = mm(jnp.concatenate([w, q_dec], axis=0), state)
        v_new = u - ws[:c]
        outs.append(ws[c:] + mm(attn, stack(v_new)))
        k_dec_t = _dot_nt(ident, k_dec, HIGHEST)
        state = state * chunk_decay + jnp.where(same_head, mm(k_dec_t, v_new), 0.0)

    s_ref[...] = state
    o = jnp.concatenate(outs, axis=0)
    ms = _dot(o * o, head_ones, HIGHEST) * (1.0 / hd)
    z = z_ref[...].astype(F32)
    o_ref[...] = (o * lax.rsqrt(ms + RMS_EPS) * onorm_ref[0, 0:1, :] * (z * _sigmoid(z))).astype(BF16)


def _gdn(qkv, ab, z, alog_lanes, dtb_lanes, onorm_lanes, batch, seq):
    n = qkv.shape[0]
    nt = seq // GDN_ROWS

    def tile(col_of_pair):
        return pl.BlockSpec((GDN_ROWS, LANES), lambda b, p, t: (b * nt + t, col_of_pair(p)))

    lanes_vec = pl.BlockSpec((1, SUBLANES, LANES), lambda b, p, t: (p, 0, 0))
    return pl.pallas_call(
        _gdn_kernel,
        grid=(batch, DN_PAIRS, nt),
        in_specs=[tile(lambda p: p), tile(lambda p: DN_PAIRS + p), tile(lambda p: 2 * DN_PAIRS + p),
                  tile(lambda p: 0), tile(lambda p: p), lanes_vec, lanes_vec, lanes_vec],
        out_specs=tile(lambda p: p),
        out_shape=jax.ShapeDtypeStruct((n, DN_WIDTH), BF16),
        scratch_shapes=[pltpu.VMEM((LANES, LANES), F32)],
        compiler_params=pltpu.CompilerParams(dimension_semantics=("arbitrary", "arbitrary", "arbitrary"),
                                             vmem_limit_bytes=VMEM_LIMIT_BYTES),
        name="gdn",
    )(qkv, qkv, qkv, ab, z, alog_lanes, dtb_lanes, onorm_lanes)


def _attn_kernel(slopes_ref, q_ref, k_ref, v_ref, lam_ref, subln_ref, o_ref,
                 m1_ref, l1_ref, acc1_ref, m2_ref, l2_ref, acc2_ref, *, lam_init):
    h = pl.program_id(1)
    i = pl.program_id(2)
    j = pl.program_id(3)
    d = DIFF_HEAD_DIM

    @pl.when(j == 0)
    def _():
        for m_ref, l_ref, acc_ref in ((m1_ref, l1_ref, acc1_ref), (m2_ref, l2_ref, acc2_ref)):
            m_ref[...] = jnp.full(m_ref.shape, -jnp.inf, F32)
            l_ref[...] = jnp.zeros(l_ref.shape, F32)
            acc_ref[...] = jnp.zeros(acc_ref.shape, F32)

    def step(masked):
        q = q_ref[...]
        first = lax.broadcasted_iota(jnp.int32, q.shape, 1) < d
        scale = jnp.asarray(d ** -0.5, BF16)
        q1 = jnp.where(first, q, 0) * scale
        q2 = jnp.where(first, 0, q) * scale
        k = k_ref[...]
        v = v_ref[...]
        qpos = i * ATTN_Q + lax.broadcasted_iota(jnp.int32, (ATTN_Q, ATTN_K), 0)
        kpos = j * ATTN_K + lax.broadcasted_iota(jnp.int32, (ATTN_Q, ATTN_K), 1)
        dist = qpos - kpos
        bias = -slopes_ref[h] * dist.astype(F32)
        for qh, m_ref, l_ref, acc_ref in ((q1, m1_ref, l1_ref, acc1_ref), (q2, m2_ref, l2_ref, acc2_ref)):
            s = _dot_nt(qh, k) + bias
            if masked:
                s = jnp.where(dist >= 0, s, -jnp.inf)
            m_old = m_ref[...]
            m_new = jnp.maximum(m_old, jnp.max(s, axis=-1, keepdims=True))
            alpha = jnp.exp(m_old - m_new)
            p = jnp.exp(s - m_new)
            l_ref[...] = alpha * l_ref[...] + jnp.sum(p, axis=-1, keepdims=True)
            acc_ref[...] = alpha * acc_ref[...] + _dot(p.astype(BF16), v)
            m_ref[...] = m_new

    @pl.when(j < i)
    def _():
        step(masked=False)

    @pl.when(j == i)
    def _():
        step(masked=True)
        lam_rows = lam_ref[...]
        lam = (jnp.exp(jnp.sum(lam_rows[0:1] * lam_rows[1:2], axis=-1, keepdims=True))
               - jnp.exp(jnp.sum(lam_rows[2:3] * lam_rows[3:4], axis=-1, keepdims=True)) + lam_init)
        o = acc1_ref[...] / l1_ref[...] - lam * (acc2_ref[...] / l2_ref[...])
        o_ref[...] = (_rms(o, subln_ref[...], SUBLN_EPS) * (1.0 - lam_init)).astype(BF16)


def _attn(slopes, df, lam_rows, subln, batch, seq, lam_init):
    n = df.shape[0]
    nq = seq // ATTN_Q
    assert ATTN_Q == ATTN_K
    q_spec = pl.BlockSpec((ATTN_Q, LANES), lambda b, h, i, j: (b * nq + i, h))
    k_spec = pl.BlockSpec((ATTN_K, LANES), lambda b, h, i, j: (b * nq + jnp.minimum(j, i), DIFF_HEADS + h))
    v_spec = pl.BlockSpec((ATTN_K, LANES), lambda b, h, i, j: (b * nq + jnp.minimum(j, i), 2 * DIFF_HEADS + h))
    stat = pltpu.VMEM((ATTN_Q, 1), F32)
    acc = pltpu.VMEM((ATTN_Q, DIFF_V_DIM), F32)
    return pl.pallas_call(
        functools.partial(_attn_kernel, lam_init=lam_init),
        grid=(batch, DIFF_HEADS, nq, nq),
        in_specs=[pl.BlockSpec(memory_space=pltpu.SMEM), q_spec, k_spec, v_spec,
                  pl.BlockSpec((SUBLANES, LANES), lambda b, h, i, j: (0, 0)),
                  pl.BlockSpec((1, DIFF_V_DIM), lambda b, h, i, j: (0, 0))],
        out_specs=pl.BlockSpec((ATTN_Q, LANES), lambda b, h, i, j: (b * nq + i, h)),
        out_shape=jax.ShapeDtypeStruct((n, DIFF_WIDTH), BF16),
        scratch_shapes=[stat, stat, acc, stat, stat, acc],
        compiler_params=pltpu.CompilerParams(
            dimension_semantics=("arbitrary", "arbitrary", "arbitrary", "arbitrary"),
            vmem_limit_bytes=VMEM_LIMIT_BYTES),
        name="diff_attn",
    )(slopes, df, df, df, lam_rows, subln)


def _merge_kernel(x_ref, g_ref, oa_ref, ob_ref, wa_ref, wb_ref, wgate_ref, wout_ref, o_ref):
    x = x_ref[...]
    h = _rms(x, g_ref[...], RMS_EPS).astype(BF16)
    gates = _sigmoid(_dot(h, wgate_ref[...]))
    y_a = _dot(oa_ref[...], wa_ref[...])
    y_b = _dot(ob_ref[...], wb_ref[...])
    mixed = (gates[:, :D_MODEL] * y_a + gates[:, D_MODEL:] * y_b).astype(BF16)
    o_ref[...] = x + _dot(mixed, wout_ref[...])


def _merge(x1, g, o_a, o_b, wa, wb, wgate, wout):
    n = x1.shape[0]
    row = lambda w: pl.BlockSpec((MERGE_ROWS, w), lambda i: (i, 0))
    return pl.pallas_call(
        _merge_kernel,
        grid=(n // MERGE_ROWS,),
        in_specs=[row(D_MODEL), _resident((1, D_MODEL)), row(DN_WIDTH), row(DIFF_WIDTH),
                  _resident(wa.shape), _resident(wb.shape), _resident(wgate.shape), _resident(wout.shape)],
        out_specs=row(D_MODEL),
        out_shape=jax.ShapeDtypeStruct((n, D_MODEL), F32),
        compiler_params=pltpu.CompilerParams(dimension_semantics=("arbitrary",),
                                             vmem_limit_bytes=VMEM_LIMIT_BYTES),
        name="merge",
    )(x1, g, o_a, o_b, wa, wb, wgate, wout)


def _lambda_init(layer_idx):
    return 0.8 - 0.6 * math.exp(-0.3 * layer_idx)


def _pair_lanes(per_head):
    lanes = jnp.repeat(per_head.astype(F32).reshape(DN_PAIRS, 2), DN_HEAD_DIM, axis=1)
    return jnp.broadcast_to(lanes[:, None, :], (DN_PAIRS, SUBLANES, LANES))


def kernel(x, ffn1_norm, ffn1_w_gate, ffn1_w_up, ffn1_w_down, mix_norm, w_in, conv_qkv, dn_a_log, dn_dt_bias, dn_out_norm, diff_lambda_q1, diff_lambda_k1, diff_lambda_q2, diff_lambda_k2, diff_subln, w_branch_a, w_branch_b, w_out, ffn2_norm, ffn2_w_gate, ffn2_w_up, ffn2_w_down, final_norm):
    batch, seq, _ = x.shape
    depth = ffn1_norm.shape[0]
    n = batch * seq
    xs = x.reshape(n, D_MODEL)
    slopes = jnp.asarray([2.0 ** (-8.0 * (i + 1) / DIFF_HEADS) for i in range(DIFF_HEADS)], F32)
    final_g = final_norm.reshape(1, D_MODEL)

    c_qkv = 3 * DN_WIDTH
    c_z = c_qkv + DN_WIDTH
    c_ab = c_z + 2 * DN_HEADS
    c_df = c_ab + 3 * DIFF_WIDTH

    for l in range(depth):
        xs = _ffn(xs, ffn1_norm[l].reshape(1, D_MODEL), ffn1_w_gate[l].astype(BF16), ffn1_w_up[l].astype(BF16),
                  ffn1_w_down[l].astype(BF16), final_g, final_norm=False)

        w = w_in[l]
        wab = jnp.pad(w[:, c_z:c_ab], ((0, 0), (0, LANES - 2 * DN_HEADS))).astype(BF16)
        qkv, z, ab, df = _proj(xs, mix_norm[l].reshape(1, D_MODEL), w[:, :c_qkv].astype(BF16),
                               w[:, c_qkv:c_z].astype(BF16), wab, w[:, c_ab:c_df].astype(BF16),
                               conv_qkv[l], seq)

        o_a = _gdn(qkv, ab, z, _pair_lanes(dn_a_log[l]), _pair_lanes(dn_dt_bias[l]),
                   jnp.broadcast_to(jnp.tile(dn_out_norm[l].astype(F32), 2)[None, None, :],
                                    (DN_PAIRS, SUBLANES, LANES)), batch, seq)

        lam_rows = jnp.zeros((SUBLANES, LANES), F32).at[0:4, :DIFF_HEAD_DIM].set(
            jnp.stack([diff_lambda_q1[l], diff_lambda_k1[l], diff_lambda_q2[l], diff_lambda_k2[l]]).astype(F32))
        o_b = _attn(slopes, df, lam_rows, diff_subln[l].reshape(1, DIFF_V_DIM), batch, seq, _lambda_init(l))

        xs = _merge(xs, mix_norm[l].reshape(1, D_MODEL), o_a, o_b, w_branch_a[l].astype(BF16),
                    w_branch_b[l].astype(BF16), w[:, c_df:].astype(BF16), w_out[l].astype(BF16))

        xs = _ffn(xs, ffn2_norm[l].reshape(1, D_MODEL), ffn2_w_gate[l].astype(BF16), ffn2_w_up[l].astype(BF16),
                  ffn2_w_down[l].astype(BF16), final_g, final_norm=(l == depth - 1))

    return xs.reshape(batch, seq, D_MODEL)
```

```python
import functools
import math

import jax
import jax.numpy as jnp
import numpy as np
from jax import lax
from jax.experimental import pallas as pl
from jax.experimental.pallas import tpu as pltpu

F32 = jnp.float32
BF16 = jnp.bfloat16

D_MODEL = 1024
D_FF = 2816
RMS_EPS = 1e-6
SUBLN_EPS = 1e-5
L2_EPS = 1e-6
CONV_WIDTH = 4
DN_HEADS = 8
DN_HEAD_DIM = 64
DN_WIDTH = DN_HEADS * DN_HEAD_DIM
DN_CHUNK = 64
DN_PAIRS = DN_HEADS // 2
DIFF_HEADS = 4
DIFF_HEAD_DIM = 64
DIFF_V_DIM = 2 * DIFF_HEAD_DIM
DIFF_WIDTH = DIFF_HEADS * DIFF_V_DIM
LANES = 128
SUBLANES = 8
LOG2E = math.log2(math.e)

VMEM_LIMIT_BYTES = 56 * 1024 * 1024

FFN_ROWS = 512
FF_CHUNKS = ((0, 512), (512, 1024), (1024, 1536), (1536, 2048), (2048, 2560), (2560, 2816))
PROJ_ROWS = 512
GDN_ROWS = 256
ATTN_Q = 512
ATTN_K = 512
ATTN_ROW_GROUP = 256
MERGE_ROWS = 512
POS_LO = 16


def _rms(x, g, eps):
    return x * lax.rsqrt(jnp.mean(x * x, axis=-1, keepdims=True) + eps) * g


def _sigmoid(x):
    return 1.0 / (1.0 + jnp.exp(-x))


def _dot(a, b):
    return jnp.dot(a, b, preferred_element_type=F32)


def _dot_nt(a, b):
    return lax.dot_general(a, b, (((1,), (1,)), ((), ())), preferred_element_type=F32)


def _split_bf16(x, terms):
    parts = []
    rest = x
    for t in range(terms):
        part = rest.astype(BF16)
        parts.append(part)
        if t + 1 < terms:
            rest = rest - part.astype(F32)
    return parts


def _resident(shape):
    return pl.BlockSpec(shape, lambda *_: (0,) * len(shape), pipeline_mode=pl.Buffered(1))


def _ffn_kernel(x_ref, g_ref, wg_ref, wu_ref, wd_ref, fg_ref, o_ref, *, final_norm):
    x = x_ref[...]
    h = _rms(x, g_ref[...], RMS_EPS).astype(BF16)
    acc = None
    for c0, c1 in FF_CHUNKS:
        gate = _dot(h, wg_ref[:, c0:c1])
        up = _dot(h, wu_ref[:, c0:c1])
        act = (gate * _sigmoid(gate) * up).astype(BF16)
        part = _dot(act, wd_ref[c0:c1, :])
        acc = part if acc is None else acc + part
    y = x + 0.5 * acc
    if final_norm:
        y = _rms(y, fg_ref[...], RMS_EPS)
    o_ref[...] = y


def _ffn(x, g, wg, wu, wd, fg, final_norm):
    n = x.shape[0]
    row = pl.BlockSpec((FFN_ROWS, D_MODEL), lambda i: (i, 0))
    return pl.pallas_call(
        functools.partial(_ffn_kernel, final_norm=final_norm),
        grid=(n // FFN_ROWS,),
        in_specs=[row, _resident((1, D_MODEL)), _resident((D_MODEL, D_FF)), _resident((D_MODEL, D_FF)),
                  _resident((D_FF, D_MODEL)), _resident((1, D_MODEL))],
        out_specs=row,
        out_shape=jax.ShapeDtypeStruct((n, D_MODEL), F32),
        compiler_params=pltpu.CompilerParams(dimension_semantics=("arbitrary",),
                                             vmem_limit_bytes=VMEM_LIMIT_BYTES),
        name="ffn",
    )(x, g, wg, wu, wd, fg)


def _proj_kernel(x_ref, g_ref, wqkv_ref, wz_ref, wab_ref, wdfq_ref, wdfkv_ref, cw_ref,
                 qkv_ref, z_ref, ab_ref, df_ref, ext_ref, *, tiles_per_seq):
    halo = SUBLANES

    @pl.when(pl.program_id(0) % tiles_per_seq == 0)
    def _():
        ext_ref[0:halo, :] = jnp.zeros((halo, 3 * DN_WIDTH), F32)

    h = _rms(x_ref[...], g_ref[...], RMS_EPS).astype(BF16)
    pre = _dot(h, wqkv_ref[...])
    ext_ref[halo:halo + PROJ_ROWS, :] = pre
    cw = cw_ref[...]
    y = cw[CONV_WIDTH - 1:CONV_WIDTH, :] * pre
    for j in range(CONV_WIDTH - 1):
        y = y + cw[j:j + 1, :] * ext_ref[pl.ds(halo - (CONV_WIDTH - 1) + j, PROJ_ROWS), :]
    ext_ref[0:halo, :] = ext_ref[PROJ_ROWS:PROJ_ROWS + halo, :]
    qkv_ref[...] = y * _sigmoid(y)
    z_ref[...] = _dot(h, wz_ref[...]).astype(BF16)
    ab_ref[...] = _dot(h, wab_ref[...])
    df_ref[:, :DIFF_WIDTH] = (_dot(h, wdfq_ref[...]) * (DIFF_HEAD_DIM ** -0.5 * LOG2E)).astype(BF16)
    df_ref[:, DIFF_WIDTH:] = _dot(h, wdfkv_ref[...]).astype(BF16)


def _proj(x1, g, wqkv, wz, wab, wdfq, wdfkv, cw, seq):
    n = x1.shape[0]
    row = lambda w: pl.BlockSpec((PROJ_ROWS, w), lambda i: (i, 0))
    return pl.pallas_call(
        functools.partial(_proj_kernel, tiles_per_seq=seq // PROJ_ROWS),
        grid=(n // PROJ_ROWS,),
        in_specs=[row(D_MODEL), _resident((1, D_MODEL)), _resident(wqkv.shape), _resident(wz.shape),
                  _resident(wab.shape), _resident(wdfq.shape), _resident(wdfkv.shape), _resident(cw.shape)],
        out_specs=[row(3 * DN_WIDTH), row(DN_WIDTH), row(LANES), row(3 * DIFF_WIDTH)],
        out_shape=[jax.ShapeDtypeStruct((n, 3 * DN_WIDTH), F32),
                   jax.ShapeDtypeStruct((n, DN_WIDTH), BF16),
                   jax.ShapeDtypeStruct((n, LANES), F32),
                   jax.ShapeDtypeStruct((n, 3 * DIFF_WIDTH), BF16)],
        scratch_shapes=[pltpu.VMEM((PROJ_ROWS + SUBLANES, 3 * DN_WIDTH), F32)],
        compiler_params=pltpu.CompilerParams(dimension_semantics=("arbitrary",),
                                             vmem_limit_bytes=VMEM_LIMIT_BYTES),
        name="proj",
    )(x1, g, wqkv, wz, wab, wdfq, wdfkv, cw)


def _gdn_kernel(q_ref, k_ref, v_ref, ab_ref, z_ref, alog_ref, dtb_ref, onorm_ref, o_ref, s_ref, *, batch):
    c = DN_CHUNK
    hd = DN_HEAD_DIM
    units = [(b, p) for b in range(batch) for p in range(DN_PAIRS)]

    @pl.when(pl.program_id(0) == 0)
    def _():
        s_ref[...] = jnp.zeros(s_ref.shape, F32)

    row = lax.broadcasted_iota(jnp.int32, (c, LANES), 0)
    lane = lax.broadcasted_iota(jnp.int32, (c, LANES), 1)
    col = lane & (hd - 1)
    even_bf = jnp.where(lane < hd, 1.0, 0.0).astype(BF16)
    odd_bf = jnp.where(lane < hd, 0.0, 1.0).astype(BF16)
    incl = col <= row
    strict = col < row
    same16 = (row >> 4) == (col >> 4)
    same32 = (row >> 5) == (col >> 5)
    eye = jnp.where(col == row, 1.0, 0.0).astype(F32)
    r2 = lax.broadcasted_iota(jnp.int32, (LANES, LANES), 0)
    l2 = lax.broadcasted_iota(jnp.int32, (LANES, LANES), 1)
    same_head = (r2 >> 6) == (l2 >> 6)
    head_ones = jnp.where(same_head, 1.0, 0.0).astype(BF16)
    head_ones2 = jnp.concatenate([head_ones, head_ones], axis=0)
    ident = jnp.where(r2 == l2, 1.0, 0.0).astype(BF16)
    r3 = lax.broadcasted_iota(jnp.int32, (3 * LANES, LANES), 0) & (LANES - 1)
    l3 = lax.broadcasted_iota(jnp.int32, (3 * LANES, LANES), 1)
    rc = lax.broadcasted_iota(jnp.int32, (c, 3 * c), 0)
    cc = lax.broadcasted_iota(jnp.int32, (c, 3 * c), 1) & (c - 1)
    lower3 = jnp.where(cc <= rc, 1.0, 0.0).astype(BF16)
    ones3 = jnp.ones((c, 3 * c), BF16)

    def stack(x):
        xb = x.astype(BF16)
        return jnp.concatenate([xb * even_bf, xb * odd_bf], axis=0)

    def mm(a, b):
        return _dot(a.astype(BF16), b.astype(BF16))

    def mm_nt(a, b):
        return _dot_nt(a.astype(BF16), b.astype(BF16))

    def head_sums(x):
        return _dot(jnp.concatenate(_split_bf16(x, 2), axis=1), head_ones2)

    def exact_rows(lhs3, x):
        return _dot(lhs3, jnp.concatenate(_split_bf16(x, 3), axis=0))

    ab3 = [jnp.concatenate(_split_bf16(ab_ref[b], 3), axis=1) for b in range(batch)]
    q_t, k_t, v_t, g_t, beta_t = [], [], [], [], []
    for b, p in units:
        lanes = slice(p * LANES, (p + 1) * LANES)
        expand_a = jnp.where(r3 == 2 * p + (l3 >> 6), 1.0, 0.0).astype(BF16)
        expand_b = jnp.where(r3 == DN_HEADS + 2 * p + (l3 >> 6), 1.0, 0.0).astype(BF16)
        sp_in = _dot(ab3[b], expand_a) + dtb_ref[0:1, lanes]
        softplus = jnp.maximum(sp_in, 0.0) + jnp.log(1.0 + jnp.exp(-jnp.abs(sp_in)))
        g_t.append(-jnp.exp(alog_ref[0:1, lanes]) * softplus)
        beta_t.append(_sigmoid(_dot(ab3[b], expand_b)))
        q = q_ref[b, :, lanes]
        k = k_ref[b, :, lanes]
        q_t.append(q * lax.rsqrt(head_sums(q * q) + L2_EPS) * (hd ** -0.5))
        k_t.append(k * lax.rsqrt(head_sums(k * k) + L2_EPS))
        v_t.append(v_ref[b, :, lanes])

    states = [s_ref[i] for i in range(len(units))]
    outs = [[] for _ in units]
    for n in range(GDN_ROWS // c):
        sl = slice(n * c, (n + 1) * c)
        q = [x[sl] for x in q_t]
        k = [x[sl] for x in k_t]
        v = [x[sl] for x in v_t]
        beta = [x[sl] for x in beta_t]
        gc = [exact_rows(lower3, x[sl]) for x in g_t]
        gc_row = [exact_rows(ones3, x * eye) for x in gc]
        decay = [jnp.where(incl, jnp.exp(jnp.where(incl, a - b_, 0.0)), 0.0) for a, b_ in zip(gc, gc_row)]
        exp_gc = [jnp.exp(x) for x in gc]
        kb = [a * b_ for a, b_ in zip(k, beta)]
        kq = [mm_nt(jnp.concatenate([a, b_], axis=0), stack(k_)) for a, b_, k_ in zip(kb, q, k)]
        a_mat = [jnp.where(strict, x[:c] * d, 0.0) for x, d in zip(kq, decay)]
        attn = [x[c:] * d for x, d in zip(kq, decay)]

        neg_d = [jnp.where(strict & same16, -x, 0.0) for x in a_mat]
        t_inv = [eye + x for x in neg_d]
        power = [mm(x, stack(x)) for x in neg_d]
        for _ in range(2):
            both = [mm(jnp.concatenate([pw, t], axis=0), stack(pw)) for pw, t in zip(power, t_inv)]
            power = [x[:c] for x in both]
            t_inv = [t + x[c:] for t, x in zip(t_inv, both)]
        t_inv = [t + mm(t, stack(pw)) for t, pw in zip(t_inv, power)]
        for block in (strict & same32 & ~same16, strict & ~same32):
            left = [mm(t, stack(jnp.where(block, x, 0.0))) for t, x in zip(t_inv, a_mat)]
            t_inv = [t - mm(x, stack(t)) for t, x in zip(t_inv, left)]

        uw = [mm(t, jnp.concatenate([stack(v_ * b_), stack(kb_ * e)], axis=1))
              for t, v_, b_, kb_, e in zip(t_inv, v, beta, kb, exp_gc)]
        q_dec = [a * e for a, e in zip(q, exp_gc)]
        gc_last = [x[c - 1:c, :] for x in gc]
        k_dec_t = [_dot_nt(ident, (k_ * jnp.exp(gl - g_)).astype(BF16)) for k_, gl, g_ in zip(k, gc_last, gc)]

        ws = [mm(jnp.concatenate([x[:, LANES:], qd], axis=0), st) for x, qd, st in zip(uw, q_dec, states)]
        v_new = [x[:, :LANES] - y[:c] for x, y in zip(uw, ws)]
        o_n = [y[c:] + mm(a, stack(vn)) for y, a, vn in zip(ws, attn, v_new)]
        for lst, x in zip(outs, o_n):
            lst.append(x)
        states = [st * jnp.exp(gl) + jnp.where(same_head, mm(kt, vn), 0.0)
                  for st, gl, kt, vn in zip(states, gc_last, k_dec_t, v_new)]

    for i, (b, p) in enumerate(units):
        lanes = slice(p * LANES, (p + 1) * LANES)
        s_ref[i] = states[i]
        o = jnp.concatenate(outs[i], axis=0)
        ms = head_sums(o * o) * (1.0 / hd)
        z = z_ref[b, :, lanes].astype(F32)
        o_ref[b, :, lanes] = (o * lax.rsqrt(ms + RMS_EPS) * onorm_ref[0:1, lanes] * (z * _sigmoid(z))).astype(BF16)


def _gdn(qkv, ab, z, alog_lanes, dtb_lanes, onorm_lanes, batch, seq):
    qkv3 = qkv.reshape(batch, seq, 3 * DN_WIDTH)

    def tile(width, col):
        return pl.BlockSpec((batch, GDN_ROWS, width), lambda t: (0, t, col))

    return pl.pallas_call(
        functools.partial(_gdn_kernel, batch=batch),
        grid=(seq // GDN_ROWS,),
        in_specs=[tile(DN_WIDTH, 0), tile(DN_WIDTH, 1), tile(DN_WIDTH, 2), tile(LANES, 0), tile(DN_WIDTH, 0),
                  _resident(alog_lanes.shape), _resident(dtb_lanes.shape), _resident(onorm_lanes.shape)],
        out_specs=tile(DN_WIDTH, 0),
        out_shape=jax.ShapeDtypeStruct((batch, seq, DN_WIDTH), BF16),
        scratch_shapes=[pltpu.VMEM((batch * DN_PAIRS, LANES, LANES), F32)],
        compiler_params=pltpu.CompilerParams(dimension_semantics=("arbitrary",),
                                             vmem_limit_bytes=VMEM_LIMIT_BYTES),
        name="gdn",
    )(qkv3, qkv3, qkv3, ab.reshape(batch, seq, LANES), z.reshape(batch, seq, DN_WIDTH),
      alog_lanes, dtb_lanes, onorm_lanes).reshape(batch * seq, DN_WIDTH)


def _attn_kernel(off_ref, q_ref, k_ref, v_ref, qcols_ref, kcols_ref, vcols_ref, lam_ref, subln_ref, o_ref,
                 qa_ref, m_ref, acc_ref, *, lam_init):
    h = pl.program_id(1)
    i = pl.program_id(2)
    d = DIFF_HEAD_DIM
    tq = ATTN_Q
    tk = ATTN_K
    rg = ATTN_ROW_GROUP

    q = q_ref[...]
    first = lax.broadcasted_iota(jnp.int32, q.shape, 1) < d
    qcols = jnp.broadcast_to(qcols_ref[0, 0:1, :], (tq, LANES))
    qa_ref[0:tq, :] = jnp.concatenate([jnp.where(first, q, 0), qcols], axis=1)
    qa_ref[tq:2 * tq, :] = jnp.concatenate([jnp.where(first, 0, q), qcols], axis=1)
    m_ref[...] = jnp.full(m_ref.shape, -jnp.inf, F32)
    acc_ref[...] = jnp.zeros(acc_ref.shape, F32)
    kcols = kcols_ref[...]
    vcols = vcols_ref[...]

    def kv_step(j, masked):
        start = pl.multiple_of(j * tk, tk)
        ka = jnp.concatenate([k_ref[pl.ds(start, tk), :], kcols], axis=1)
        va = jnp.concatenate([v_ref[pl.ds(start, tk), :], vcols], axis=1)
        off = off_ref[h] * j.astype(F32)
        groups = range(0, 2 * tq, rg)
        scores = [_dot_nt(qa_ref[r0:r0 + rg, :], ka) for r0 in groups]
        for r0, s in zip(groups, scores):
            rows = slice(r0, r0 + rg)
            if masked:
                qrow = lax.broadcasted_iota(jnp.int32, (rg, tk), 0) + (r0 % tq)
                kcol = lax.broadcasted_iota(jnp.int32, (rg, tk), 1)
                s = jnp.where(kcol <= qrow, s, -jnp.inf)
            m_old = m_ref[rows, :]
            m_blk = jnp.broadcast_to(jnp.max(s, axis=-1, keepdims=True), (rg, LANES)) + off
            m_new = jnp.maximum(m_old, m_blk)
            alpha = jnp.exp2(m_old - m_new)
            shift = m_new - off
            p = jnp.concatenate([jnp.exp2(s[:, c0:c0 + LANES] - shift) for c0 in range(0, tk, LANES)],
                                axis=1).astype(BF16)
            acc_ref[rows, :] = jnp.concatenate([alpha, alpha], axis=1) * acc_ref[rows, :] + _dot(p, va)
            m_ref[rows, :] = m_new

    def body(j, carry):
        kv_step(j, masked=False)
        return carry

    lax.fori_loop(0, i, body, 0)
    kv_step(i, masked=True)

    lam_rows = lam_ref[...]
    lam = (jnp.exp(jnp.sum(lam_rows[0:1] * lam_rows[1:2], axis=-1, keepdims=True))
           - jnp.exp(jnp.sum(lam_rows[2:3] * lam_rows[3:4], axis=-1, keepdims=True)) + lam_init)
    acc1 = acc_ref[0:tq, :]
    acc2 = acc_ref[tq:2 * tq, :]
    o = acc1[:, :LANES] / acc1[:, LANES:] - lam * (acc2[:, :LANES] / acc2[:, LANES:])
    o_ref[...] = (_rms(o, subln_ref[...], SUBLN_EPS) * (1.0 - lam_init)).astype(BF16)


def _attn(slopes, df, lam_rows, subln, batch, seq, lam_init):
    assert ATTN_Q == ATTN_K and ATTN_K <= 256 * POS_LO and ATTN_Q % ATTN_ROW_GROUP == 0
    nq = seq // ATTN_Q
    df3 = df.reshape(batch, seq, 3 * DIFF_WIDTH)
    coef = slopes * LOG2E
    c_parts = jnp.stack(_split_bf16(coef, 3), axis=1)
    qcols = jnp.zeros((DIFF_HEADS, SUBLANES, LANES), BF16).at[:, :, 0:6].set(
        jnp.broadcast_to(jnp.tile(c_parts, (1, 2))[:, None, :], (DIFF_HEADS, SUBLANES, 6)))
    pos = np.arange(ATTN_K)
    kcols_np = np.zeros((ATTN_K, LANES), np.float32)
    kcols_np[:, 0:3] = (pos // POS_LO * POS_LO)[:, None]
    kcols_np[:, 3:6] = (pos % POS_LO)[:, None]
    kcols = jnp.asarray(kcols_np, BF16)
    vcols = jnp.ones((ATTN_K, LANES), BF16)
    block_off = coef * ATTN_K

    seq_block = lambda col0: pl.BlockSpec((None, seq, LANES), lambda b, h, i: (b, 0, col0 + h))
    const = lambda shape: pl.BlockSpec(shape, lambda b, h, i: (0,) * len(shape))
    return pl.pallas_call(
        functools.partial(_attn_kernel, lam_init=lam_init),
        grid=(batch, DIFF_HEADS, nq),
        in_specs=[pl.BlockSpec(memory_space=pltpu.SMEM),
                  pl.BlockSpec((None, ATTN_Q, LANES), lambda b, h, i: (b, i, h)),
                  seq_block(DIFF_HEADS), seq_block(2 * DIFF_HEADS),
                  pl.BlockSpec((1, SUBLANES, LANES), lambda b, h, i: (h, 0, 0)),
                  const((ATTN_K, LANES)), const((ATTN_K, LANES)),
                  const((SUBLANES, LANES)), const((1, DIFF_V_DIM))],
        out_specs=pl.BlockSpec((None, ATTN_Q, LANES), lambda b, h, i: (b, i, h)),
        out_shape=jax.ShapeDtypeStruct((batch, seq, DIFF_WIDTH), BF16),
        scratch_shapes=[pltpu.VMEM((2 * ATTN_Q, 2 * LANES), BF16),
                        pltpu.VMEM((2 * ATTN_Q, LANES), F32),
                        pltpu.VMEM((2 * ATTN_Q, 2 * LANES), F32)],
        compiler_params=pltpu.CompilerParams(dimension_semantics=("arbitrary", "arbitrary", "arbitrary"),
                                             vmem_limit_bytes=VMEM_LIMIT_BYTES),
        name="diff_attn",
    )(block_off, df3, df3, df3, qcols, kcols, vcols, lam_rows, subln).reshape(batch * seq, DIFF_WIDTH)


def _merge_kernel(x_ref, g_ref, oa_ref, ob_ref, wa_ref, wb_ref, wgate_ref, wout_ref, o_ref):
    x = x_ref[...]
    h = _rms(x, g_ref[...], RMS_EPS).astype(BF16)
    gates = _sigmoid(_dot(h, wgate_ref[...]))
    y_a = _dot(oa_ref[...], wa_ref[...])
    y_b = _dot(ob_ref[...], wb_ref[...])
    mixed = (gates[:, :D_MODEL] * y_a + gates[:, D_MODEL:] * y_b).astype(BF16)
    o_ref[...] = x + _dot(mixed, wout_ref[...])


def _merge(x1, g, o_a, o_b, wa, wb, wgate, wout):
    n = x1.shape[0]
    row = lambda w: pl.BlockSpec((MERGE_ROWS, w), lambda i: (i, 0))
    return pl.pallas_call(
        _merge_kernel,
        grid=(n // MERGE_ROWS,),
        in_specs=[row(D_MODEL), _resident((1, D_MODEL)), row(DN_WIDTH), row(DIFF_WIDTH),
                  _resident(wa.shape), _resident(wb.shape), _resident(wgate.shape), _resident(wout.shape)],
        out_specs=row(D_MODEL),
        out_shape=jax.ShapeDtypeStruct((n, D_MODEL), F32),
        compiler_params=pltpu.CompilerParams(dimension_semantics=("arbitrary",),
                                             vmem_limit_bytes=VMEM_LIMIT_BYTES),
        name="merge",
    )(x1, g, o_a, o_b, wa, wb, wgate, wout)


def _lambda_init(layer_idx):
    return 0.8 - 0.6 * math.exp(-0.3 * layer_idx)


def _head_lanes(per_head):
    lanes = jnp.repeat(per_head.astype(F32), DN_HEAD_DIM)
    return jnp.broadcast_to(lanes[None, :], (SUBLANES, DN_WIDTH))


def kernel(x, ffn1_norm, ffn1_w_gate, ffn1_w_up, ffn1_w_down, mix_norm, w_in, conv_qkv, dn_a_log, dn_dt_bias, dn_out_norm, diff_lambda_q1, diff_lambda_k1, diff_lambda_q2, diff_lambda_k2, diff_subln, w_branch_a, w_branch_b, w_out, ffn2_norm, ffn2_w_gate, ffn2_w_up, ffn2_w_down, final_norm):
    batch, seq, _ = x.shape
    depth = ffn1_norm.shape[0]
    n = batch * seq
    xs = x.reshape(n, D_MODEL)
    slopes = jnp.asarray([2.0 ** (-8.0 * (i + 1) / DIFF_HEADS) for i in range(DIFF_HEADS)], F32)
    final_g = final_norm.reshape(1, D_MODEL)

    c_qkv = 3 * DN_WIDTH
    c_z = c_qkv + DN_WIDTH
    c_ab = c_z + 2 * DN_HEADS
    c_dfq = c_ab + DIFF_WIDTH
    c_df = c_ab + 3 * DIFF_WIDTH

    for l in range(depth):
        xs = _ffn(xs, ffn1_norm[l].reshape(1, D_MODEL), ffn1_w_gate[l].astype(BF16), ffn1_w_up[l].astype(BF16),
                  ffn1_w_down[l].astype(BF16), final_g, final_norm=False)

        w = w_in[l]
        wab = jnp.pad(w[:, c_z:c_ab], ((0, 0), (0, LANES - 2 * DN_HEADS))).astype(BF16)
        qkv, z, ab, df = _proj(xs, mix_norm[l].reshape(1, D_MODEL), w[:, :c_qkv].astype(BF16),
                               w[:, c_qkv:c_z].astype(BF16), wab, w[:, c_ab:c_dfq].astype(BF16),
                               w[:, c_dfq:c_df].astype(BF16), conv_qkv[l], seq)

        o_a = _gdn(qkv, ab, z, _head_lanes(dn_a_log[l]), _head_lanes(dn_dt_bias[l]),
                   jnp.broadcast_to(jnp.tile(dn_out_norm[l].astype(F32), DN_HEADS)[None, :], (SUBLANES, DN_WIDTH)),
                   batch, seq)

        lam_rows = jnp.zeros((SUBLANES, LANES), F32).at[0:4, :DIFF_HEAD_DIM].set(
            jnp.stack([diff_lambda_q1[l], diff_lambda_k1[l], diff_lambda_q2[l], diff_lambda_k2[l]]).astype(F32))
        o_b = _attn(slopes, df, lam_rows, diff_subln[l].reshape(1, DIFF_V_DIM), batch, seq, _lambda_init(l))

        xs = _merge(xs, mix_norm[l].reshape(1, D_MODEL), o_a, o_b, w_branch_a[l].astype(BF16),
                    w_branch_b[l].astype(BF16), w[:, c_df:].astype(BF16), w_out[l].astype(BF16))

        xs = _ffn(xs, ffn2_norm[l].reshape(1, D_MODEL), ffn2_w_gate[l].astype(BF16), ffn2_w_up[l].astype(BF16),
                  ffn2_w_down[l].astype(BF16), final_g, final_norm=(l == depth - 1))

    return xs.reshape(batch, seq, D_MODEL)
```

```python
import functools
import math

import jax
import jax.numpy as jnp
import numpy as np
from jax import lax
from jax.experimental import pallas as pl
from jax.experimental.pallas import tpu as pltpu

F32 = jnp.float32
BF16 = jnp.bfloat16

D_MODEL = 1024
D_FF = 2816
RMS_EPS = 1e-6
SUBLN_EPS = 1e-5
L2_EPS = 1e-6
CONV_WIDTH = 4
DN_HEADS = 8
DN_HEAD_DIM = 64
DN_WIDTH = DN_HEADS * DN_HEAD_DIM
DN_CHUNK = 64
DN_PAIRS = DN_HEADS // 2
DIFF_HEADS = 4
DIFF_HEAD_DIM = 64
DIFF_V_DIM = 2 * DIFF_HEAD_DIM
DIFF_WIDTH = DIFF_HEADS * DIFF_V_DIM
LANES = 128
SUBLANES = 8
LOG2E = math.log2(math.e)

VMEM_LIMIT_BYTES = 56 * 1024 * 1024

FFN_ROWS = 512
FF_CHUNKS = ((0, 512), (512, 1024), (1024, 1536), (1536, 2048), (2048, 2560), (2560, 2816))
PROJ_ROWS = 512
GDN_ROWS = 256
ATTN_Q = 512
ATTN_K = 512
ATTN_ROW_GROUP = 256
MERGE_ROWS = 512
POS_LO = 16
SKIP_LOG2 = 160.0
NORM_SLACK = 1.001


def _rms(x, g, eps):
    return x * lax.rsqrt(jnp.mean(x * x, axis=-1, keepdims=True) + eps) * g


def _sigmoid(x):
    return 1.0 / (1.0 + jnp.exp(-x))


def _dot(a, b):
    return jnp.dot(a, b, preferred_element_type=F32)


def _dot_nt(a, b):
    return lax.dot_general(a, b, (((1,), (1,)), ((), ())), preferred_element_type=F32)


def _split_bf16(x, terms):
    parts = []
    rest = x
    for t in range(terms):
        part = rest.astype(BF16)
        parts.append(part)
        if t + 1 < terms:
            rest = rest - part.astype(F32)
    return parts


def _resident(shape):
    return pl.BlockSpec(shape, lambda *_: (0,) * len(shape), pipeline_mode=pl.Buffered(1))


def _ffn_kernel(x_ref, g_ref, wg_ref, wu_ref, wd_ref, fg_ref, o_ref, *, final_norm):
    x = x_ref[...]
    h = _rms(x, g_ref[...], RMS_EPS).astype(BF16)
    acc = None
    for c0, c1 in FF_CHUNKS:
        gate = _dot(h, wg_ref[:, c0:c1])
        up = _dot(h, wu_ref[:, c0:c1])
        act = (gate * _sigmoid(gate) * up).astype(BF16)
        part = _dot(act, wd_ref[c0:c1, :])
        acc = part if acc is None else acc + part
    y = x + 0.5 * acc
    if final_norm:
        y = _rms(y, fg_ref[...], RMS_EPS)
    o_ref[...] = y


def _ffn(x, g, wg, wu, wd, fg, final_norm):
    n = x.shape[0]
    row = pl.BlockSpec((FFN_ROWS, D_MODEL), lambda i: (i, 0))
    return pl.pallas_call(
        functools.partial(_ffn_kernel, final_norm=final_norm),
        grid=(n // FFN_ROWS,),
        in_specs=[row, _resident((1, D_MODEL)), _resident((D_MODEL, D_FF)), _resident((D_MODEL, D_FF)),
                  _resident((D_FF, D_MODEL)), _resident((1, D_MODEL))],
        out_specs=row,
        out_shape=jax.ShapeDtypeStruct((n, D_MODEL), F32),
        compiler_params=pltpu.CompilerParams(dimension_semantics=("arbitrary",),
                                             vmem_limit_bytes=VMEM_LIMIT_BYTES),
        name="ffn",
    )(x, g, wg, wu, wd, fg)


def _proj_kernel(x_ref, g_ref, wqkv_ref, wz_ref, wab_ref, wdfq_ref, wdfkv_ref, cw_ref,
                 qkv_ref, z_ref, ab_ref, df_ref, ext_ref, *, tiles_per_seq):
    halo = SUBLANES

    @pl.when(pl.program_id(0) % tiles_per_seq == 0)
    def _():
        ext_ref[0:halo, :] = jnp.zeros((halo, 3 * DN_WIDTH), F32)

    h = _rms(x_ref[...], g_ref[...], RMS_EPS).astype(BF16)
    pre = _dot(h, wqkv_ref[...])
    ext_ref[halo:halo + PROJ_ROWS, :] = pre
    cw = cw_ref[...]
    y = cw[CONV_WIDTH - 1:CONV_WIDTH, :] * pre
    for j in range(CONV_WIDTH - 1):
        y = y + cw[j:j + 1, :] * ext_ref[pl.ds(halo - (CONV_WIDTH - 1) + j, PROJ_ROWS), :]
    ext_ref[0:halo, :] = ext_ref[PROJ_ROWS:PROJ_ROWS + halo, :]
    qkv_ref[...] = y * _sigmoid(y)
    z_ref[...] = _dot(h, wz_ref[...]).astype(BF16)
    ab_ref[...] = _dot(h, wab_ref[...])
    df_ref[:, :DIFF_WIDTH] = (_dot(h, wdfq_ref[...]) * (DIFF_HEAD_DIM ** -0.5 * LOG2E)).astype(BF16)
    df_ref[:, DIFF_WIDTH:] = _dot(h, wdfkv_ref[...]).astype(BF16)


def _proj(x1, g, wqkv, wz, wab, wdfq, wdfkv, cw, seq):
    n = x1.shape[0]
    row = lambda w: pl.BlockSpec((PROJ_ROWS, w), lambda i: (i, 0))
    return pl.pallas_call(
        functools.partial(_proj_kernel, tiles_per_seq=seq // PROJ_ROWS),
        grid=(n // PROJ_ROWS,),
        in_specs=[row(D_MODEL), _resident((1, D_MODEL)), _resident(wqkv.shape), _resident(wz.shape),
                  _resident(wab.shape), _resident(wdfq.shape), _resident(wdfkv.shape), _resident(cw.shape)],
        out_specs=[row(3 * DN_WIDTH), row(DN_WIDTH), row(LANES), row(3 * DIFF_WIDTH)],
        out_shape=[jax.ShapeDtypeStruct((n, 3 * DN_WIDTH), F32),
                   jax.ShapeDtypeStruct((n, DN_WIDTH), BF16),
                   jax.ShapeDtypeStruct((n, LANES), F32),
                   jax.ShapeDtypeStruct((n, 3 * DIFF_WIDTH), BF16)],
        scratch_shapes=[pltpu.VMEM((PROJ_ROWS + SUBLANES, 3 * DN_WIDTH), F32)],
        compiler_params=pltpu.CompilerParams(dimension_semantics=("arbitrary",),
                                             vmem_limit_bytes=VMEM_LIMIT_BYTES),
        name="proj",
    )(x1, g, wqkv, wz, wab, wdfq, wdfkv, cw)


def _gdn_kernel(q_ref, k_ref, v_ref, ab_ref, z_ref, alog_ref, dtb_ref, onorm_ref, o_ref, s_ref, *, batch):
    c = DN_CHUNK
    hd = DN_HEAD_DIM
    units = [(b, p) for b in range(batch) for p in range(DN_PAIRS)]

    @pl.when(pl.program_id(0) == 0)
    def _():
        s_ref[...] = jnp.zeros(s_ref.shape, F32)

    row = lax.broadcasted_iota(jnp.int32, (c, LANES), 0)
    lane = lax.broadcasted_iota(jnp.int32, (c, LANES), 1)
    col = lane & (hd - 1)
    even_bf = jnp.where(lane < hd, 1.0, 0.0).astype(BF16)
    odd_bf = jnp.where(lane < hd, 0.0, 1.0).astype(BF16)
    incl = col <= row
    strict = col < row
    same16 = (row >> 4) == (col >> 4)
    same32 = (row >> 5) == (col >> 5)
    eye = jnp.where(col == row, 1.0, 0.0).astype(F32)
    r2 = lax.broadcasted_iota(jnp.int32, (LANES, LANES), 0)
    l2 = lax.broadcasted_iota(jnp.int32, (LANES, LANES), 1)
    same_head = (r2 >> 6) == (l2 >> 6)
    head_ones = jnp.where(same_head, 1.0, 0.0).astype(BF16)
    head_ones2 = jnp.concatenate([head_ones, head_ones], axis=0)
    ident = jnp.where(r2 == l2, 1.0, 0.0).astype(BF16)
    r3 = lax.broadcasted_iota(jnp.int32, (3 * LANES, LANES), 0) & (LANES - 1)
    l3 = lax.broadcasted_iota(jnp.int32, (3 * LANES, LANES), 1)
    rc = lax.broadcasted_iota(jnp.int32, (c, 3 * c), 0)
    cc = lax.broadcasted_iota(jnp.int32, (c, 3 * c), 1) & (c - 1)
    lower3 = jnp.where(cc <= rc, 1.0, 0.0).astype(BF16)
    ones3 = jnp.ones((c, 3 * c), BF16)

    def stack(x):
        xb = x.astype(BF16)
        return jnp.concatenate([xb * even_bf, xb * odd_bf], axis=0)

    def mm(a, b):
        return _dot(a.astype(BF16), b.astype(BF16))

    def mm_nt(a, b):
        return _dot_nt(a.astype(BF16), b.astype(BF16))

    def head_sums(x):
        return _dot(jnp.concatenate(_split_bf16(x, 2), axis=1), head_ones2)

    def exact_rows(lhs3, x):
        return _dot(lhs3, jnp.concatenate(_split_bf16(x, 3), axis=0))

    ab3 = [jnp.concatenate(_split_bf16(ab_ref[b], 3), axis=1) for b in range(batch)]
    q_t, k_t, v_t, g_t, beta_t = [], [], [], [], []
    for b, p in units:
        lanes = slice(p * LANES, (p + 1) * LANES)
        expand_a = jnp.where(r3 == 2 * p + (l3 >> 6), 1.0, 0.0).astype(BF16)
        expand_b = jnp.where(r3 == DN_HEADS + 2 * p + (l3 >> 6), 1.0, 0.0).astype(BF16)
        sp_in = _dot(ab3[b], expand_a) + dtb_ref[0:1, lanes]
        softplus = jnp.maximum(sp_in, 0.0) + jnp.log(1.0 + jnp.exp(-jnp.abs(sp_in)))
        g_t.append(-jnp.exp(alog_ref[0:1, lanes]) * softplus)
        beta_t.append(_sigmoid(_dot(ab3[b], expand_b)))
        q = q_ref[b, :, lanes]
        k = k_ref[b, :, lanes]
        q_t.append(q * lax.rsqrt(head_sums(q * q) + L2_EPS) * (hd ** -0.5))
        k_t.append(k * lax.rsqrt(head_sums(k * k) + L2_EPS))
        v_t.append(v_ref[b, :, lanes])

    states = [s_ref[i] for i in range(len(units))]
    outs = [[] for _ in units]
    for n in range(GDN_ROWS // c):
        sl = slice(n * c, (n + 1) * c)
        q = [x[sl] for x in q_t]
        k = [x[sl] for x in k_t]
        v = [x[sl] for x in v_t]
        beta = [x[sl] for x in beta_t]
        gc = [exact_rows(lower3, x[sl]) for x in g_t]
        gc_row = [exact_rows(ones3, x * eye) for x in gc]
        decay = [jnp.where(incl, jnp.exp(jnp.where(incl, a - b_, 0.0)), 0.0) for a, b_ in zip(gc, gc_row)]
        exp_gc = [jnp.exp(x) for x in gc]
        kb = [a * b_ for a, b_ in zip(k, beta)]
        kq = [mm_nt(jnp.concatenate([a, b_], axis=0), stack(k_)) for a, b_, k_ in zip(kb, q, k)]
        a_mat = [jnp.where(strict, x[:c] * d, 0.0) for x, d in zip(kq, decay)]
        attn = [x[c:] * d for x, d in zip(kq, decay)]

        neg_d = [jnp.where(strict & same16, -x, 0.0) for x in a_mat]
        t_inv = [eye + x for x in neg_d]
        power = [mm(x, stack(x)) for x in neg_d]
        for _ in range(2):
            both = [mm(jnp.concatenate([pw, t], axis=0), stack(pw)) for pw, t in zip(power, t_inv)]
            power = [x[:c] for x in both]
            t_inv = [t + x[c:] for t, x in zip(t_inv, both)]
        t_inv = [t + mm(t, stack(pw)) for t, pw in zip(t_inv, power)]
        for block in (strict & same32 & ~same16, strict & ~same32):
            left = [mm(t, stack(jnp.where(block, x, 0.0))) for t, x in zip(t_inv, a_mat)]
            t_inv = [t - mm(x, stack(t)) for t, x in zip(t_inv, left)]

        uw = [mm(t, jnp.concatenate([stack(v_ * b_), stack(kb_ * e)], axis=1))
              for t, v_, b_, kb_, e in zip(t_inv, v, beta, kb, exp_gc)]
        q_dec = [a * e for a, e in zip(q, exp_gc)]
        gc_last = [x[c - 1:c, :] for x in gc]
        k_dec_t = [_dot_nt(ident, (k_ * jnp.exp(gl - g_)).astype(BF16)) for k_, gl, g_ in zip(k, gc_last, gc)]

        ws = [mm(jnp.concatenate([x[:, LANES:], qd], axis=0), st) for x, qd, st in zip(uw, q_dec, states)]
        v_new = [x[:, :LANES] - y[:c] for x, y in zip(uw, ws)]
        o_n = [y[c:] + mm(a, stack(vn)) for y, a, vn in zip(ws, attn, v_new)]
        for lst, x in zip(outs, o_n):
            lst.append(x)
        states = [st * jnp.exp(gl) + jnp.where(same_head, mm(kt, vn), 0.0)
                  for st, gl, kt, vn in zip(states, gc_last, k_dec_t, v_new)]

    for i, (b, p) in enumerate(units):
        lanes = slice(p * LANES, (p + 1) * LANES)
        s_ref[i] = states[i]
        o = jnp.concatenate(outs[i], axis=0)
        ms = head_sums(o * o) * (1.0 / hd)
        z = z_ref[b, :, lanes].astype(F32)
        o_ref[b, :, lanes] = (o * lax.rsqrt(ms + RMS_EPS) * onorm_ref[0:1, lanes] * (z * _sigmoid(z))).astype(BF16)


def _gdn(qkv, ab, z, alog_lanes, dtb_lanes, onorm_lanes, batch, seq):
    qkv3 = qkv.reshape(batch, seq, 3 * DN_WIDTH)

    def tile(width, col):
        return pl.BlockSpec((batch, GDN_ROWS, width), lambda t: (0, t, col))

    return pl.pallas_call(
        functools.partial(_gdn_kernel, batch=batch),
        grid=(seq // GDN_ROWS,),
        in_specs=[tile(DN_WIDTH, 0), tile(DN_WIDTH, 1), tile(DN_WIDTH, 2), tile(LANES, 0), tile(DN_WIDTH, 0),
                  _resident(alog_lanes.shape), _resident(dtb_lanes.shape), _resident(onorm_lanes.shape)],
        out_specs=tile(DN_WIDTH, 0),
        out_shape=jax.ShapeDtypeStruct((batch, seq, DN_WIDTH), BF16),
        scratch_shapes=[pltpu.VMEM((batch * DN_PAIRS, LANES, LANES), F32)],
        compiler_params=pltpu.CompilerParams(dimension_semantics=("arbitrary",),
                                             vmem_limit_bytes=VMEM_LIMIT_BYTES),
        name="gdn",
    )(qkv3, qkv3, qkv3, ab.reshape(batch, seq, LANES), z.reshape(batch, seq, DN_WIDTH),
      alog_lanes, dtb_lanes, onorm_lanes).reshape(batch * seq, DN_WIDTH)


def _attn_kernel(off_ref, coef_ref, q_ref, k_ref, v_ref, qcols_ref, kcols_ref, vcols_ref, lam_ref, subln_ref,
                 o_ref, qa_ref, m_ref, acc_ref, k2max_ref, *, lam_init):
    h = pl.program_id(1)
    i = pl.program_id(2)
    d = DIFF_HEAD_DIM
    tq = ATTN_Q
    tk = ATTN_K
    rg = ATTN_ROW_GROUP

    q = q_ref[...]
    first = lax.broadcasted_iota(jnp.int32, q.shape, 1) < d
    qcols = jnp.broadcast_to(qcols_ref[0, 0:1, :], (tq, LANES))
    qa_ref[0:tq, :] = jnp.concatenate([jnp.where(first, q, 0), qcols], axis=1)
    qa_ref[tq:2 * tq, :] = jnp.concatenate([jnp.where(first, 0, q), qcols], axis=1)
    m_ref[...] = jnp.full(m_ref.shape, -jnp.inf, F32)
    acc_ref[...] = jnp.zeros(acc_ref.shape, F32)
    kcols = kcols_ref[...]
    vcols = vcols_ref[...]

    def max_sq_norm(x):
        sq = x.astype(F32) * x.astype(F32)
        half = lax.broadcasted_iota(jnp.int32, x.shape, 1) < d
        both = jnp.maximum(jnp.sum(jnp.where(half, sq, 0.0), axis=-1, keepdims=True),
                           jnp.sum(jnp.where(half, 0.0, sq), axis=-1, keepdims=True))
        return jnp.max(both, axis=0, keepdims=True)

    @pl.when(i == 0)
    def _():
        k2max_ref[...] = jnp.broadcast_to(max_sq_norm(k_ref[...]), k2max_ref.shape)

    qk_bound = jnp.sqrt(max_sq_norm(q) * k2max_ref[0:1, 0:1]) * NORM_SLACK
    gap = (SKIP_LOG2 + 2.0 * qk_bound) / coef_ref[h]
    first_block = jnp.floor(((i * tq).astype(F32) - gap) * (1.0 / tk))
    j0 = jnp.clip(first_block, 0.0, i.astype(F32)).astype(jnp.int32)[0, 0]

    groups = range(0, 2 * tq, rg)

    def kv_steps(blocks):
        scores, values, offs = [], [], []
        for j, _ in blocks:
            start = pl.multiple_of(j * tk, tk)
            ka = jnp.concatenate([k_ref[pl.ds(start, tk), :], kcols], axis=1)
            scores.append([_dot_nt(qa_ref[r0:r0 + rg, :], ka) for r0 in groups])
            values.append(jnp.concatenate([v_ref[pl.ds(start, tk), :], vcols], axis=1))
            offs.append(off_ref[h] * j.astype(F32))
        for (_, masked), block_scores, va, off in zip(blocks, scores, values, offs):
            for r0, s in zip(groups, block_scores):
                rows = slice(r0, r0 + rg)
                if masked:
                    qrow = lax.broadcasted_iota(jnp.int32, (rg, tk), 0) + (r0 % tq)
                    kcol = lax.broadcasted_iota(jnp.int32, (rg, tk), 1)
                    s = jnp.where(kcol <= qrow, s, -jnp.inf)
                m_old = m_ref[rows, :]
                m_blk = jnp.broadcast_to(jnp.max(s, axis=-1, keepdims=True), (rg, LANES)) + off
                m_new = jnp.maximum(m_old, m_blk)
                alpha = jnp.exp2(m_old - m_new)
                shift = m_new - off
                p = jnp.concatenate([jnp.exp2(s[:, c0:c0 + LANES] - shift) for c0 in range(0, tk, LANES)],
                                    axis=1).astype(BF16)
                acc_ref[rows, :] = jnp.concatenate([alpha, alpha], axis=1) * acc_ref[rows, :] + _dot(p, va)
                m_ref[rows, :] = m_new

    def body(t, carry):
        kv_steps([(j0 + 2 * t, False), (j0 + 2 * t + 1, False)])
        return carry

    n_past = i - j0
    lax.fori_loop(0, n_past // 2, body, 0)

    @pl.when(n_past % 2 == 1)
    def _():
        kv_steps([(i - 1, False), (i, True)])

    @pl.when(n_past % 2 == 0)
    def _():
        kv_steps([(i, True)])

    lam_rows = lam_ref[...]
    lam = (jnp.exp(jnp.sum(lam_rows[0:1] * lam_rows[1:2], axis=-1, keepdims=True))
           - jnp.exp(jnp.sum(lam_rows[2:3] * lam_rows[3:4], axis=-1, keepdims=True)) + lam_init)
    acc1 = acc_ref[0:tq, :]
    acc2 = acc_ref[tq:2 * tq, :]
    o = acc1[:, :LANES] / acc1[:, LANES:] - lam * (acc2[:, :LANES] / acc2[:, LANES:])
    o_ref[...] = (_rms(o, subln_ref[...], SUBLN_EPS) * (1.0 - lam_init)).astype(BF16)


def _attn(slopes, df, lam_rows, subln, batch, seq, lam_init):
    assert ATTN_Q == ATTN_K and ATTN_K <= 256 * POS_LO and ATTN_Q % ATTN_ROW_GROUP == 0
    nq = seq // ATTN_Q
    df3 = df.reshape(batch, seq, 3 * DIFF_WIDTH)
    coef = slopes * LOG2E
    c_parts = jnp.stack(_split_bf16(coef, 3), axis=1)
    qcols = jnp.zeros((DIFF_HEADS, SUBLANES, LANES), BF16).at[:, :, 0:6].set(
        jnp.broadcast_to(jnp.tile(c_parts, (1, 2))[:, None, :], (DIFF_HEADS, SUBLANES, 6)))
    pos = np.arange(ATTN_K)
    kcols_np = np.zeros((ATTN_K, LANES), np.float32)
    kcols_np[:, 0:3] = (pos // POS_LO * POS_LO)[:, None]
    kcols_np[:, 3:6] = (pos % POS_LO)[:, None]
    kcols = jnp.asarray(kcols_np, BF16)
    vcols = jnp.ones((ATTN_K, LANES), BF16)
    block_off = coef * ATTN_K

    seq_block = lambda col0: pl.BlockSpec((None, seq, LANES), lambda b, h, i: (b, 0, col0 + h))
    const = lambda shape: pl.BlockSpec(shape, lambda b, h, i: (0,) * len(shape))
    return pl.pallas_call(
        functools.partial(_attn_kernel, lam_init=lam_init),
        grid=(batch, DIFF_HEADS, nq),
        in_specs=[pl.BlockSpec(memory_space=pltpu.SMEM), pl.BlockSpec(memory_space=pltpu.SMEM),
                  pl.BlockSpec((None, ATTN_Q, LANES), lambda b, h, i: (b, i, h)),
                  seq_block(DIFF_HEADS), seq_block(2 * DIFF_HEADS),
                  pl.BlockSpec((1, SUBLANES, LANES), lambda b, h, i: (h, 0, 0)),
                  const((ATTN_K, LANES)), const((ATTN_K, LANES)),
                  const((SUBLANES, LANES)), const((1, DIFF_V_DIM))],
        out_specs=pl.BlockSpec((None, ATTN_Q, LANES), lambda b, h, i: (b, i, h)),
        out_shape=jax.ShapeDtypeStruct((batch, seq, DIFF_WIDTH), BF16),
        scratch_shapes=[pltpu.VMEM((2 * ATTN_Q, 2 * LANES), BF16),
                        pltpu.VMEM((2 * ATTN_Q, LANES), F32),
                        pltpu.VMEM((2 * ATTN_Q, 2 * LANES), F32),
                        pltpu.VMEM((SUBLANES, LANES), F32)],
        compiler_params=pltpu.CompilerParams(dimension_semantics=("arbitrary", "arbitrary", "arbitrary"),
                                             vmem_limit_bytes=VMEM_LIMIT_BYTES),
        name="diff_attn",
    )(block_off, coef, df3, df3, df3, qcols, kcols, vcols, lam_rows, subln).reshape(batch * seq, DIFF_WIDTH)


def _merge_kernel(x_ref, g_ref, oa_ref, ob_ref, wa_ref, wb_ref, wgate_ref, wout_ref, o_ref):
    x = x_ref[...]
    h = _rms(x, g_ref[...], RMS_EPS).astype(BF16)
    gates = _sigmoid(_dot(h, wgate_ref[...]))
    y_a = _dot(oa_ref[...], wa_ref[...])
    y_b = _dot(ob_ref[...], wb_ref[...])
    mixed = (gates[:, :D_MODEL] * y_a + gates[:, D_MODEL:] * y_b).astype(BF16)
    o_ref[...] = x + _dot(mixed, wout_ref[...])


def _merge(x1, g, o_a, o_b, wa, wb, wgate, wout):
    n = x1.shape[0]
    row = lambda w: pl.BlockSpec((MERGE_ROWS, w), lambda i: (i, 0))
    return pl.pallas_call(
        _merge_kernel,
        grid=(n // MERGE_ROWS,),
        in_specs=[row(D_MODEL), _resident((1, D_MODEL)), row(DN_WIDTH), row(DIFF_WIDTH),
                  _resident(wa.shape), _resident(wb.shape), _resident(wgate.shape), _resident(wout.shape)],
        out_specs=row(D_MODEL),
        out_shape=jax.ShapeDtypeStruct((n, D_MODEL), F32),
        compiler_params=pltpu.CompilerParams(dimension_semantics=("arbitrary",),
                                             vmem_limit_bytes=VMEM_LIMIT_BYTES),
        name="merge",
    )(x1, g, o_a, o_b, wa, wb, wgate, wout)


def _lambda_init(layer_idx):
    return 0.8 - 0.6 * math.exp(-0.3 * layer_idx)


def _head_lanes(per_head):
    lanes = jnp.repeat(per_head.astype(F32), DN_HEAD_DIM)
    return jnp.broadcast_to(lanes[None, :], (SUBLANES, DN_WIDTH))


def kernel(x, ffn1_norm, ffn1_w_gate, ffn1_w_up, ffn1_w_down, mix_norm, w_in, conv_qkv, dn_a_log, dn_dt_bias, dn_out_norm, diff_lambda_q1, diff_lambda_k1, diff_lambda_q2, diff_lambda_k2, diff_subln, w_branch_a, w_branch_b, w_out, ffn2_norm, ffn2_w_gate, ffn2_w_up, ffn2_w_down, final_norm):
    batch, seq, _ = x.shape
    depth = ffn1_norm.shape[0]
    n = batch * seq
    xs = x.reshape(n, D_MODEL)
    slopes = jnp.asarray([2.0 ** (-8.0 * (i + 1) / DIFF_HEADS) for i in range(DIFF_HEADS)], F32)
    final_g = final_norm.reshape(1, D_MODEL)

    c_qkv = 3 * DN_WIDTH
    c_z = c_qkv + DN_WIDTH
    c_ab = c_z + 2 * DN_HEADS
    c_dfq = c_ab + DIFF_WIDTH
    c_df = c_ab + 3 * DIFF_WIDTH

    for l in range(depth):
        xs = _ffn(xs, ffn1_norm[l].reshape(1, D_MODEL), ffn1_w_gate[l].astype(BF16), ffn1_w_up[l].astype(BF16),
                  ffn1_w_down[l].astype(BF16), final_g, final_norm=False)

        w = w_in[l]
        wab = jnp.pad(w[:, c_z:c_ab], ((0, 0), (0, LANES - 2 * DN_HEADS))).astype(BF16)
        qkv, z, ab, df = _proj(xs, mix_norm[l].reshape(1, D_MODEL), w[:, :c_qkv].astype(BF16),
                               w[:, c_qkv:c_z].astype(BF16), wab, w[:, c_ab:c_dfq].astype(BF16),
                               w[:, c_dfq:c_df].astype(BF16), conv_qkv[l], seq)

        o_a = _gdn(qkv, ab, z, _head_lanes(dn_a_log[l]), _head_lanes(dn_dt_bias[l]),
                   jnp.broadcast_to(jnp.tile(dn_out_norm[l].astype(F32), DN_HEADS)[None, :], (SUBLANES, DN_WIDTH)),
                   batch, seq)

        lam_rows = jnp.zeros((SUBLANES, LANES), F32).at[0:4, :DIFF_HEAD_DIM].set(
            jnp.stack([diff_lambda_q1[l], diff_lambda_k1[l], diff_lambda_q2[l], diff_lambda_k2[l]]).astype(F32))
        o_b = _attn(slopes, df, lam_rows, diff_subln[l].reshape(1, DIFF_V_DIM), batch, seq, _lambda_init(l))

        xs = _merge(xs, mix_norm[l].reshape(1, D_MODEL), o_a, o_b, w_branch_a[l].astype(BF16),
                    w_branch_b[l].astype(BF16), w[:, c_df:].astype(BF16), w_out[l].astype(BF16))

        xs = _ffn(xs, ffn2_norm[l].reshape(1, D_MODEL), ffn2_w_gate[l].astype(BF16), ffn2_w_up[l].astype(BF16),
                  ffn2_w_down[l].astype(BF16), final_g, final_norm=(l == depth - 1))

    return xs.reshape(batch, seq, D_MODEL)
```

```python
import functools
import math

import jax
import jax.numpy as jnp
import numpy as np
from jax import lax
from jax.experimental import pallas as pl
from jax.experimental.pallas import tpu as pltpu

F32 = jnp.float32
BF16 = jnp.bfloat16

D_MODEL = 1024
D_FF = 2816
RMS_EPS = 1e-6
SUBLN_EPS = 1e-5
L2_EPS = 1e-6
CONV_WIDTH = 4
DN_HEADS = 8
DN_HEAD_DIM = 64
DN_WIDTH = DN_HEADS * DN_HEAD_DIM
DN_CHUNK = 64
DN_PAIRS = DN_HEADS // 2
DIFF_HEADS = 4
DIFF_HEAD_DIM = 64
DIFF_V_DIM = 2 * DIFF_HEAD_DIM
DIFF_WIDTH = DIFF_HEADS * DIFF_V_DIM
LANES = 128
SUBLANES = 8
LOG2E = math.log2(math.e)

VMEM_LIMIT_BYTES = 56 * 1024 * 1024

FFN_ROWS = 512
FF_CHUNKS = ((0, 512), (512, 1024), (1024, 1536), (1536, 2048), (2048, 2560), (2560, 2816))
PROJ_ROWS = 512
GDN_ROWS = 256
ATTN_Q = 512
ATTN_K = 512
ATTN_ROW_GROUP = 256
MERGE_ROWS = 512
POS_LO = 16
SKIP_LOG2 = 160.0
NORM_SLACK = 1.001


def _rms(x, g, eps):
    return x * lax.rsqrt(jnp.mean(x * x, axis=-1, keepdims=True) + eps) * g


def _sigmoid(x):
    return 1.0 / (1.0 + jnp.exp(-x))


def _dot(a, b):
    return jnp.dot(a, b, preferred_element_type=F32)


def _dot_nt(a, b):
    return lax.dot_general(a, b, (((1,), (1,)), ((), ())), preferred_element_type=F32)


def _split_bf16(x, terms):
    parts = []
    rest = x
    for t in range(terms):
        part = rest.astype(BF16)
        parts.append(part)
        if t + 1 < terms:
            rest = rest - part.astype(F32)
    return parts


def _resident(shape):
    return pl.BlockSpec(shape, lambda *_: (0,) * len(shape), pipeline_mode=pl.Buffered(1))


def _ffn_kernel(x_ref, g_ref, wg_ref, wu_ref, wd_ref, fg_ref, o_ref, *, final_norm):
    x = x_ref[...]
    h = _rms(x, g_ref[...], RMS_EPS).astype(BF16)
    acc = None
    for c0, c1 in FF_CHUNKS:
        gate = _dot(h, wg_ref[:, c0:c1])
        up = _dot(h, wu_ref[:, c0:c1])
        act = (gate * _sigmoid(gate) * up).astype(BF16)
        part = _dot(act, wd_ref[c0:c1, :])
        acc = part if acc is None else acc + part
    y = x + 0.5 * acc
    if final_norm:
        y = _rms(y, fg_ref[...], RMS_EPS)
    o_ref[...] = y


def _ffn(x, g, wg, wu, wd, fg, final_norm):
    n = x.shape[0]
    row = pl.BlockSpec((FFN_ROWS, D_MODEL), lambda i: (i, 0))
    return pl.pallas_call(
        functools.partial(_ffn_kernel, final_norm=final_norm),
        grid=(n // FFN_ROWS,),
        in_specs=[row, _resident((1, D_MODEL)), _resident((D_MODEL, D_FF)), _resident((D_MODEL, D_FF)),
                  _resident((D_FF, D_MODEL)), _resident((1, D_MODEL))],
        out_specs=row,
        out_shape=jax.ShapeDtypeStruct((n, D_MODEL), F32),
        compiler_params=pltpu.CompilerParams(dimension_semantics=("arbitrary",),
                                             vmem_limit_bytes=VMEM_LIMIT_BYTES),
        name="ffn",
    )(x, g, wg, wu, wd, fg)


def _proj_kernel(x_ref, g_ref, wqkv_ref, wz_ref, wab_ref, wdfq_ref, wdfkv_ref, cw_ref,
                 qkv_ref, z_ref, ab_ref, df_ref, ext_ref, *, tiles_per_seq):
    halo = SUBLANES

    @pl.when(pl.program_id(0) % tiles_per_seq == 0)
    def _():
        ext_ref[0:halo, :] = jnp.zeros((halo, 3 * DN_WIDTH), F32)

    h = _rms(x_ref[...], g_ref[...], RMS_EPS).astype(BF16)
    pre = _dot(h, wqkv_ref[...])
    ext_ref[halo:halo + PROJ_ROWS, :] = pre
    cw = cw_ref[...]
    y = cw[CONV_WIDTH - 1:CONV_WIDTH, :] * pre
    for j in range(CONV_WIDTH - 1):
        y = y + cw[j:j + 1, :] * ext_ref[pl.ds(halo - (CONV_WIDTH - 1) + j, PROJ_ROWS), :]
    ext_ref[0:halo, :] = ext_ref[PROJ_ROWS:PROJ_ROWS + halo, :]
    qkv_ref[...] = y * _sigmoid(y)
    z_ref[...] = _dot(h, wz_ref[...]).astype(BF16)
    ab_ref[...] = _dot(h, wab_ref[...])
    df_ref[:, :DIFF_WIDTH] = (_dot(h, wdfq_ref[...]) * (DIFF_HEAD_DIM ** -0.5 * LOG2E)).astype(BF16)
    df_ref[:, DIFF_WIDTH:] = _dot(h, wdfkv_ref[...]).astype(BF16)


def _proj(x1, g, wqkv, wz, wab, wdfq, wdfkv, cw, seq):
    n = x1.shape[0]
    row = lambda w: pl.BlockSpec((PROJ_ROWS, w), lambda i: (i, 0))
    return pl.pallas_call(
        functools.partial(_proj_kernel, tiles_per_seq=seq // PROJ_ROWS),
        grid=(n // PROJ_ROWS,),
        in_specs=[row(D_MODEL), _resident((1, D_MODEL)), _resident(wqkv.shape), _resident(wz.shape),
                  _resident(wab.shape), _resident(wdfq.shape), _resident(wdfkv.shape), _resident(cw.shape)],
        out_specs=[row(3 * DN_WIDTH), row(DN_WIDTH), row(LANES), row(3 * DIFF_WIDTH)],
        out_shape=[jax.ShapeDtypeStruct((n, 3 * DN_WIDTH), F32),
                   jax.ShapeDtypeStruct((n, DN_WIDTH), BF16),
                   jax.ShapeDtypeStruct((n, LANES), F32),
                   jax.ShapeDtypeStruct((n, 3 * DIFF_WIDTH), BF16)],
        scratch_shapes=[pltpu.VMEM((PROJ_ROWS + SUBLANES, 3 * DN_WIDTH), F32)],
        compiler_params=pltpu.CompilerParams(dimension_semantics=("arbitrary",),
                                             vmem_limit_bytes=VMEM_LIMIT_BYTES),
        name="proj",
    )(x1, g, wqkv, wz, wab, wdfq, wdfkv, cw)


def _gdn_kernel(q_ref, k_ref, v_ref, ab_ref, z_ref, alog_ref, dtb_ref, onorm_ref, o_ref, s_ref, *, batch):
    c = DN_CHUNK
    hd = DN_HEAD_DIM
    units = [(b, p) for b in range(batch) for p in range(DN_PAIRS)]

    @pl.when(pl.program_id(0) == 0)
    def _():
        s_ref[...] = jnp.zeros(s_ref.shape, F32)

    row = lax.broadcasted_iota(jnp.int32, (c, LANES), 0)
    lane = lax.broadcasted_iota(jnp.int32, (c, LANES), 1)
    col = lane & (hd - 1)
    even_bf = jnp.where(lane < hd, 1.0, 0.0).astype(BF16)
    odd_bf = jnp.where(lane < hd, 0.0, 1.0).astype(BF16)
    incl = col <= row
    strict = col < row
    same16 = (row >> 4) == (col >> 4)
    same32 = (row >> 5) == (col >> 5)
    eye = jnp.where(col == row, 1.0, 0.0).astype(F32)
    r2 = lax.broadcasted_iota(jnp.int32, (LANES, LANES), 0)
    l2 = lax.broadcasted_iota(jnp.int32, (LANES, LANES), 1)
    same_head = (r2 >> 6) == (l2 >> 6)
    head_ones = jnp.where(same_head, 1.0, 0.0).astype(BF16)
    head_ones2 = jnp.concatenate([head_ones, head_ones], axis=0)
    ident = jnp.where(r2 == l2, 1.0, 0.0).astype(BF16)
    r3 = lax.broadcasted_iota(jnp.int32, (3 * LANES, LANES), 0) & (LANES - 1)
    l3 = lax.broadcasted_iota(jnp.int32, (3 * LANES, LANES), 1)
    rc = lax.broadcasted_iota(jnp.int32, (c, 3 * c), 0)
    cc = lax.broadcasted_iota(jnp.int32, (c, 3 * c), 1) & (c - 1)
    lower3 = jnp.where(cc <= rc, 1.0, 0.0).astype(BF16)
    ones3 = jnp.ones((c, 3 * c), BF16)

    def stack(x):
        xb = x.astype(BF16)
        return jnp.concatenate([xb * even_bf, xb * odd_bf], axis=0)

    def mm(a, b):
        return _dot(a.astype(BF16), b.astype(BF16))

    def mm_nt(a, b):
        return _dot_nt(a.astype(BF16), b.astype(BF16))

    def head_sums(x):
        return _dot(jnp.concatenate(_split_bf16(x, 2), axis=1), head_ones2)

    def exact_rows(lhs3, x):
        return _dot(lhs3, jnp.concatenate(_split_bf16(x, 3), axis=0))

    ab3 = [jnp.concatenate(_split_bf16(ab_ref[b], 3), axis=1) for b in range(batch)]
    q_t, k_t, v_t, g_t, beta_t = [], [], [], [], []
    for b, p in units:
        lanes = slice(p * LANES, (p + 1) * LANES)
        expand_a = jnp.where(r3 == 2 * p + (l3 >> 6), 1.0, 0.0).astype(BF16)
        expand_b = jnp.where(r3 == DN_HEADS + 2 * p + (l3 >> 6), 1.0, 0.0).astype(BF16)
        sp_in = _dot(ab3[b], expand_a) + dtb_ref[0:1, lanes]
        softplus = jnp.maximum(sp_in, 0.0) + jnp.log(1.0 + jnp.exp(-jnp.abs(sp_in)))
        g_t.append(-jnp.exp(alog_ref[0:1, lanes]) * softplus)
        beta_t.append(_sigmoid(_dot(ab3[b], expand_b)))
        q = q_ref[b, :, lanes]
        k = k_ref[b, :, lanes]
        q_t.append(q * lax.rsqrt(head_sums(q * q) + L2_EPS) * (hd ** -0.5))
        k_t.append(k * lax.rsqrt(head_sums(k * k) + L2_EPS))
        v_t.append(v_ref[b, :, lanes])

    states = [s_ref[i] for i in range(len(units))]
    outs = [[] for _ in units]
    for n in range(GDN_ROWS // c):
        sl = slice(n * c, (n + 1) * c)
        q = [x[sl] for x in q_t]
        k = [x[sl] for x in k_t]
        v = [x[sl] for x in v_t]
        beta = [x[sl] for x in beta_t]
        gc = [exact_rows(lower3, x[sl]) for x in g_t]
        gc_row = [exact_rows(ones3, x * eye) for x in gc]
        decay = [jnp.where(incl, jnp.exp(jnp.where(incl, a - b_, 0.0)), 0.0) for a, b_ in zip(gc, gc_row)]
        exp_gc = [jnp.exp(x) for x in gc]
        kb = [a * b_ for a, b_ in zip(k, beta)]
        kq = [mm_nt(jnp.concatenate([a, b_], axis=0), stack(k_)) for a, b_, k_ in zip(kb, q, k)]
        a_mat = [jnp.where(strict, x[:c] * d, 0.0) for x, d in zip(kq, decay)]
        attn = [x[c:] * d for x, d in zip(kq, decay)]

        neg_d = [jnp.where(strict & same16, -x, 0.0) for x in a_mat]
        t_inv = [eye + x for x in neg_d]
        power = [mm(x, stack(x)) for x in neg_d]
        for _ in range(2):
            both = [mm(jnp.concatenate([pw, t], axis=0), stack(pw)) for pw, t in zip(power, t_inv)]
            power = [x[:c] for x in both]
            t_inv = [t + x[c:] for t, x in zip(t_inv, both)]
        t_inv = [t + mm(t, stack(pw)) for t, pw in zip(t_inv, power)]
        for block in (strict & same32 & ~same16, strict & ~same32):
            left = [mm(t, stack(jnp.where(block, x, 0.0))) for t, x in zip(t_inv, a_mat)]
            t_inv = [t - mm(x, stack(t)) for t, x in zip(t_inv, left)]

        uw = [mm(t, jnp.concatenate([stack(v_ * b_), stack(kb_ * e)], axis=1))
              for t, v_, b_, kb_, e in zip(t_inv, v, beta, kb, exp_gc)]
        q_dec = [a * e for a, e in zip(q, exp_gc)]
        gc_last = [x[c - 1:c, :] for x in gc]
        k_dec_t = [_dot_nt(ident, (k_ * jnp.exp(gl - g_)).astype(BF16)) for k_, gl, g_ in zip(k, gc_last, gc)]

        ws = [mm(jnp.concatenate([x[:, LANES:], qd], axis=0), st) for x, qd, st in zip(uw, q_dec, states)]
        v_new = [x[:, :LANES] - y[:c] for x, y in zip(uw, ws)]
        o_n = [y[c:] + mm(a, stack(vn)) for y, a, vn in zip(ws, attn, v_new)]
        for lst, x in zip(outs, o_n):
            lst.append(x)
        states = [st * jnp.exp(gl) + jnp.where(same_head, mm(kt, vn), 0.0)
                  for st, gl, kt, vn in zip(states, gc_last, k_dec_t, v_new)]

    for i, (b, p) in enumerate(units):
        lanes = slice(p * LANES, (p + 1) * LANES)
        s_ref[i] = states[i]
        o = jnp.concatenate(outs[i], axis=0)
        ms = head_sums(o * o) * (1.0 / hd)
        z = z_ref[b, :, lanes].astype(F32)
        o_ref[b, :, lanes] = (o * lax.rsqrt(ms + RMS_EPS) * onorm_ref[0:1, lanes] * (z * _sigmoid(z))).astype(BF16)


def _gdn(qkv, ab, z, alog_lanes, dtb_lanes, onorm_lanes, batch, seq):
    qkv3 = qkv.reshape(batch, seq, 3 * DN_WIDTH)

    def tile(width, col):
        return pl.BlockSpec((batch, GDN_ROWS, width), lambda t: (0, t, col))

    return pl.pallas_call(
        functools.partial(_gdn_kernel, batch=batch),
        grid=(seq // GDN_ROWS,),
        in_specs=[tile(DN_WIDTH, 0), tile(DN_WIDTH, 1), tile(DN_WIDTH, 2), tile(LANES, 0), tile(DN_WIDTH, 0),
                  _resident(alog_lanes.shape), _resident(dtb_lanes.shape), _resident(onorm_lanes.shape)],
        out_specs=tile(DN_WIDTH, 0),
        out_shape=jax.ShapeDtypeStruct((batch, seq, DN_WIDTH), BF16),
        scratch_shapes=[pltpu.VMEM((batch * DN_PAIRS, LANES, LANES), F32)],
        compiler_params=pltpu.CompilerParams(dimension_semantics=("arbitrary",),
                                             vmem_limit_bytes=VMEM_LIMIT_BYTES),
        name="gdn",
    )(qkv3, qkv3, qkv3, ab.reshape(batch, seq, LANES), z.reshape(batch, seq, DN_WIDTH),
      alog_lanes, dtb_lanes, onorm_lanes).reshape(batch * seq, DN_WIDTH)


def _attn_kernel(off_ref, coef_ref, q_ref, k_ref, v_ref, qcols_ref, kcols_ref, vcols_ref, lam_ref, subln_ref,
                 o_ref, qa_ref, m_ref, acc_ref, k2max_ref, *, lam_init):
    h = pl.program_id(1)
    i = pl.program_id(2)
    d = DIFF_HEAD_DIM
    tq = ATTN_Q
    tk = ATTN_K
    rg = ATTN_ROW_GROUP

    q = q_ref[...]
    first = lax.broadcasted_iota(jnp.int32, q.shape, 1) < d
    qcols = jnp.broadcast_to(qcols_ref[0, 0:1, :], (tq, LANES))
    qa_ref[0:tq, :] = jnp.concatenate([jnp.where(first, q, 0), qcols], axis=1)
    qa_ref[tq:2 * tq, :] = jnp.concatenate([jnp.where(first, 0, q), qcols], axis=1)
    m_ref[...] = jnp.full(m_ref.shape, -jnp.inf, F32)
    acc_ref[...] = jnp.zeros(acc_ref.shape, F32)
    kcols = kcols_ref[...]
    vcols = vcols_ref[...]

    def max_sq_norm(x):
        sq = x.astype(F32) * x.astype(F32)
        half = lax.broadcasted_iota(jnp.int32, x.shape, 1) < d
        both = jnp.maximum(jnp.sum(jnp.where(half, sq, 0.0), axis=-1, keepdims=True),
                           jnp.sum(jnp.where(half, 0.0, sq), axis=-1, keepdims=True))
        return jnp.max(both, axis=0, keepdims=True)

    @pl.when(i == 0)
    def _():
        k2max_ref[...] = jnp.broadcast_to(max_sq_norm(k_ref[...]), k2max_ref.shape)

    qk_bound = jnp.sqrt(max_sq_norm(q) * k2max_ref[0:1, 0:1]) * NORM_SLACK
    gap = (SKIP_LOG2 + 2.0 * qk_bound) / coef_ref[h]
    first_block = jnp.floor(((i * tq).astype(F32) - gap) * (1.0 / tk))
    j0 = jnp.clip(first_block, 0.0, i.astype(F32)).astype(jnp.int32)[0, 0]

    groups = range(0, 2 * tq, rg)

    def kv_steps(blocks):
        scores, values, offs = [], [], []
        for j, masked in blocks:
            start = pl.multiple_of(j * tk, tk)
            ka = jnp.concatenate([k_ref[pl.ds(start, tk), :], kcols], axis=1)
            widths = [(r0 % tq) + rg if masked else tk for r0 in groups]
            scores.append([_dot_nt(qa_ref[r0:r0 + rg, :], ka[:w]) for r0, w in zip(groups, widths)])
            values.append(jnp.concatenate([v_ref[pl.ds(start, tk), :], vcols], axis=1))
            offs.append(off_ref[h] * j.astype(F32))
        for (_, masked), block_scores, va, off in zip(blocks, scores, values, offs):
            for r0, s in zip(groups, block_scores):
                rows = slice(r0, r0 + rg)
                width = s.shape[1]
                if masked:
                    qrow = lax.broadcasted_iota(jnp.int32, (rg, width), 0) + (r0 % tq)
                    kcol = lax.broadcasted_iota(jnp.int32, (rg, width), 1)
                    s = jnp.where(kcol <= qrow, s, -jnp.inf)
                m_old = m_ref[rows, :]
                m_blk = jnp.broadcast_to(jnp.max(s, axis=-1, keepdims=True), (rg, LANES)) + off
                m_new = jnp.maximum(m_old, m_blk)
                alpha = jnp.exp2(m_old - m_new)
                shift = m_new - off
                p = jnp.concatenate([jnp.exp2(s[:, c0:c0 + LANES] - shift) for c0 in range(0, width, LANES)],
                                    axis=1).astype(BF16)
                acc_ref[rows, :] = (jnp.concatenate([alpha, alpha], axis=1) * acc_ref[rows, :]
                                    + _dot(p, va[:width]))
                m_ref[rows, :] = m_new

    def body(t, carry):
        kv_steps([(j0 + 2 * t, False), (j0 + 2 * t + 1, False)])
        return carry

    n_past = i - j0
    lax.fori_loop(0, n_past // 2, body, 0)

    @pl.when(n_past % 2 == 1)
    def _():
        kv_steps([(i - 1, False), (i, True)])

    @pl.when(n_past % 2 == 0)
    def _():
        kv_steps([(i, True)])

    lam_rows = lam_ref[...]
    lam = (jnp.exp(jnp.sum(lam_rows[0:1] * lam_rows[1:2], axis=-1, keepdims=True))
           - jnp.exp(jnp.sum(lam_rows[2:3] * lam_rows[3:4], axis=-1, keepdims=True)) + lam_init)
    acc1 = acc_ref[0:tq, :]
    acc2 = acc_ref[tq:2 * tq, :]
    o = acc1[:, :LANES] / acc1[:, LANES:] - lam * (acc2[:, :LANES] / acc2[:, LANES:])
    o_ref[...] = (_rms(o, subln_ref[...], SUBLN_EPS) * (1.0 - lam_init)).astype(BF16)


def _attn(slopes, df, lam_rows, subln, batch, seq, lam_init):
    assert ATTN_Q == ATTN_K and ATTN_K <= 256 * POS_LO and ATTN_Q % ATTN_ROW_GROUP == 0
    nq = seq // ATTN_Q
    df3 = df.reshape(batch, seq, 3 * DIFF_WIDTH)
    coef = slopes * LOG2E
    c_parts = jnp.stack(_split_bf16(coef, 3), axis=1)
    qcols = jnp.zeros((DIFF_HEADS, SUBLANES, LANES), BF16).at[:, :, 0:6].set(
        jnp.broadcast_to(jnp.tile(c_parts, (1, 2))[:, None, :], (DIFF_HEADS, SUBLANES, 6)))
    pos = np.arange(ATTN_K)
    kcols_np = np.zeros((ATTN_K, LANES), np.float32)
    kcols_np[:, 0:3] = (pos // POS_LO * POS_LO)[:, None]
    kcols_np[:, 3:6] = (pos % POS_LO)[:, None]
    kcols = jnp.asarray(kcols_np, BF16)
    vcols = jnp.ones((ATTN_K, LANES), BF16)
    block_off = coef * ATTN_K

    seq_block = lambda col0: pl.BlockSpec((None, seq, LANES), lambda b, h, i: (b, 0, col0 + h))
    const = lambda shape: pl.BlockSpec(shape, lambda b, h, i: (0,) * len(shape))
    return pl.pallas_call(
        functools.partial(_attn_kernel, lam_init=lam_init),
        grid=(batch, DIFF_HEADS, nq),
        in_specs=[pl.BlockSpec(memory_space=pltpu.SMEM), pl.BlockSpec(memory_space=pltpu.SMEM),
                  pl.BlockSpec((None, ATTN_Q, LANES), lambda b, h, i: (b, i, h)),
                  seq_block(DIFF_HEADS), seq_block(2 * DIFF_HEADS),
                  pl.BlockSpec((1, SUBLANES, LANES), lambda b, h, i: (h, 0, 0)),
                  const((ATTN_K, LANES)), const((ATTN_K, LANES)),
                  const((SUBLANES, LANES)), const((1, DIFF_V_DIM))],
        out_specs=pl.BlockSpec((None, ATTN_Q, LANES), lambda b, h, i: (b, i, h)),
        out_shape=jax.ShapeDtypeStruct((batch, seq, DIFF_WIDTH), BF16),
        scratch_shapes=[pltpu.VMEM((2 * ATTN_Q, 2 * LANES), BF16),
                        pltpu.VMEM((2 * ATTN_Q, LANES), F32),
                        pltpu.VMEM((2 * ATTN_Q, 2 * LANES), F32),
                        pltpu.VMEM((SUBLANES, LANES), F32)],
        compiler_params=pltpu.CompilerParams(dimension_semantics=("arbitrary", "arbitrary", "arbitrary"),
                                             vmem_limit_bytes=VMEM_LIMIT_BYTES),
        name="diff_attn",
    )(block_off, coef, df3, df3, df3, qcols, kcols, vcols, lam_rows, subln).reshape(batch * seq, DIFF_WIDTH)


def _merge_kernel(x_ref, g_ref, oa_ref, ob_ref, wa_ref, wb_ref, wgate_ref, wout_ref, o_ref):
    x = x_ref[...]
    h = _rms(x, g_ref[...], RMS_EPS).astype(BF16)
    gates = _sigmoid(_dot(h, wgate_ref[...]))
    y_a = _dot(oa_ref[...], wa_ref[...])
    y_b = _dot(ob_ref[...], wb_ref[...])
    mixed = (gates[:, :D_MODEL] * y_a + gates[:, D_MODEL:] * y_b).astype(BF16)
    o_ref[...] = x + _dot(mixed, wout_ref[...])


def _merge(x1, g, o_a, o_b, wa, wb, wgate, wout):
    n = x1.shape[0]
    row = lambda w: pl.BlockSpec((MERGE_ROWS, w), lambda i: (i, 0))
    return pl.pallas_call(
        _merge_kernel,
        grid=(n // MERGE_ROWS,),
        in_specs=[row(D_MODEL), _resident((1, D_MODEL)), row(DN_WIDTH), row(DIFF_WIDTH),
                  _resident(wa.shape), _resident(wb.shape), _resident(wgate.shape), _resident(wout.shape)],
        out_specs=row(D_MODEL),
        out_shape=jax.ShapeDtypeStruct((n, D_MODEL), F32),
        compiler_params=pltpu.CompilerParams(dimension_semantics=("arbitrary",),
                                             vmem_limit_bytes=VMEM_LIMIT_BYTES),
        name="merge",
    )(x1, g, o_a, o_b, wa, wb, wgate, wout)


def _lambda_init(layer_idx):
    return 0.8 - 0.6 * math.exp(-0.3 * layer_idx)


def _head_lanes(per_head):
    lanes = jnp.repeat(per_head.astype(F32), DN_HEAD_DIM)
    return jnp.broadcast_to(lanes[None, :], (SUBLANES, DN_WIDTH))


def kernel(x, ffn1_norm, ffn1_w_gate, ffn1_w_up, ffn1_w_down, mix_norm, w_in, conv_qkv, dn_a_log, dn_dt_bias, dn_out_norm, diff_lambda_q1, diff_lambda_k1, diff_lambda_q2, diff_lambda_k2, diff_subln, w_branch_a, w_branch_b, w_out, ffn2_norm, ffn2_w_gate, ffn2_w_up, ffn2_w_down, final_norm):
    batch, seq, _ = x.shape
    depth = ffn1_norm.shape[0]
    n = batch * seq
    xs = x.reshape(n, D_MODEL)
    slopes = jnp.asarray([2.0 ** (-8.0 * (i + 1) / DIFF_HEADS) for i in range(DIFF_HEADS)], F32)
    final_g = final_norm.reshape(1, D_MODEL)

    c_qkv = 3 * DN_WIDTH
    c_z = c_qkv + DN_WIDTH
    c_ab = c_z + 2 * DN_HEADS
    c_dfq = c_ab + DIFF_WIDTH
    c_df = c_ab + 3 * DIFF_WIDTH

    for l in range(depth):
        xs = _ffn(xs, ffn1_norm[l].reshape(1, D_MODEL), ffn1_w_gate[l].astype(BF16), ffn1_w_up[l].astype(BF16),
                  ffn1_w_down[l].astype(BF16), final_g, final_norm=False)

        w = w_in[l]
        wab = jnp.pad(w[:, c_z:c_ab], ((0, 0), (0, LANES - 2 * DN_HEADS))).astype(BF16)
        qkv, z, ab, df = _proj(xs, mix_norm[l].reshape(1, D_MODEL), w[:, :c_qkv].astype(BF16),
                               w[:, c_qkv:c_z].astype(BF16), wab, w[:, c_ab:c_dfq].astype(BF16),
                               w[:, c_dfq:c_df].astype(BF16), conv_qkv[l], seq)

        o_a = _gdn(qkv, ab, z, _head_lanes(dn_a_log[l]), _head_lanes(dn_dt_bias[l]),
                   jnp.broadcast_to(jnp.tile(dn_out_norm[l].astype(F32), DN_HEADS)[None, :], (SUBLANES, DN_WIDTH)),
                   batch, seq)

        lam_rows = jnp.zeros((SUBLANES, LANES), F32).at[0:4, :DIFF_HEAD_DIM].set(
            jnp.stack([diff_lambda_q1[l], diff_lambda_k1[l], diff_lambda_q2[l], diff_lambda_k2[l]]).astype(F32))
        o_b = _attn(slopes, df, lam_rows, diff_subln[l].reshape(1, DIFF_V_DIM), batch, seq, _lambda_init(l))

        xs = _merge(xs, mix_norm[l].reshape(1, D_MODEL), o_a, o_b, w_branch_a[l].astype(BF16),
                    w_branch_b[l].astype(BF16), w[:, c_df:].astype(BF16), w_out[l].astype(BF16))

        xs = _ffn(xs, ffn2_norm[l].reshape(1, D_MODEL), ffn2_w_gate[l].astype(BF16), ffn2_w_up[l].astype(BF16),
                  ffn2_w_down[l].astype(BF16), final_g, final_norm=(l == depth - 1))

    return xs.reshape(batch, seq, D_MODEL)
```

```python
import functools
import math

import jax
import jax.numpy as jnp
import numpy as np
from jax import lax
from jax.experimental import pallas as pl
from jax.experimental.pallas import tpu as pltpu

F32 = jnp.float32
BF16 = jnp.bfloat16

D_MODEL = 1024
D_FF = 2816
RMS_EPS = 1e-6
SUBLN_EPS = 1e-5
L2_EPS = 1e-6
CONV_WIDTH = 4
DN_HEADS = 8
DN_HEAD_DIM = 64
DN_WIDTH = DN_HEADS * DN_HEAD_DIM
DN_CHUNK = 64
DN_PAIRS = DN_HEADS // 2
DIFF_HEADS = 4
DIFF_HEAD_DIM = 64
DIFF_V_DIM = 2 * DIFF_HEAD_DIM
DIFF_WIDTH = DIFF_HEADS * DIFF_V_DIM
LANES = 128
SUBLANES = 8
LOG2E = math.log2(math.e)

VMEM_LIMIT_BYTES = 56 * 1024 * 1024

FFN_ROWS = 512
FF_CHUNKS = ((0, 512), (512, 1024), (1024, 1536), (1536, 2048), (2048, 2560), (2560, 2816))
PROJ_ROWS = 512
GDN_ROWS = 256
GDN_PREP_CHUNKS = 2
ATTN_Q = 512
ATTN_K = 512
ATTN_ROW_GROUP = 256
MERGE_ROWS = 512
POS_LO = 16
SKIP_LOG2 = 160.0
NORM_SLACK = 1.001


def _rms(x, g, eps):
    return x * lax.rsqrt(jnp.mean(x * x, axis=-1, keepdims=True) + eps) * g


def _sigmoid(x):
    return 1.0 / (1.0 + jnp.exp(-x))


def _dot(a, b):
    return jnp.dot(a, b, preferred_element_type=F32)


def _dot_nt(a, b):
    return lax.dot_general(a, b, (((1,), (1,)), ((), ())), preferred_element_type=F32)


def _split_bf16(x, terms):
    parts = []
    rest = x
    for t in range(terms):
        part = rest.astype(BF16)
        parts.append(part)
        if t + 1 < terms:
            rest = rest - part.astype(F32)
    return parts


def _resident(shape):
    return pl.BlockSpec(shape, lambda *_: (0,) * len(shape), pipeline_mode=pl.Buffered(1))


def _ffn_half_step(x, g_ref, wg_ref, wu_ref, wd_ref, fg_ref, final_norm):
    h = _rms(x, g_ref[...], RMS_EPS).astype(BF16)
    acc = None
    for c0, c1 in FF_CHUNKS:
        gate = _dot(h, wg_ref[:, c0:c1])
        up = _dot(h, wu_ref[:, c0:c1])
        act = (gate * _sigmoid(gate) * up).astype(BF16)
        part = _dot(act, wd_ref[c0:c1, :])
        acc = part if acc is None else acc + part
    y = x + 0.5 * acc
    if final_norm:
        y = _rms(y, fg_ref[...], RMS_EPS)
    return y


def _ffn_kernel(x_ref, g_ref, wg_ref, wu_ref, wd_ref, fg_ref, o_ref, *, final_norm):
    o_ref[...] = _ffn_half_step(x_ref[...], g_ref, wg_ref, wu_ref, wd_ref, fg_ref, final_norm)


def _ffn(x, g, wg, wu, wd, fg, final_norm):
    n = x.shape[0]
    row = pl.BlockSpec((FFN_ROWS, D_MODEL), lambda i: (i, 0))
    return pl.pallas_call(
        functools.partial(_ffn_kernel, final_norm=final_norm),
        grid=(n // FFN_ROWS,),
        in_specs=[row, _resident((1, D_MODEL)), _resident((D_MODEL, D_FF)), _resident((D_MODEL, D_FF)),
                  _resident((D_FF, D_MODEL)), _resident((1, D_MODEL))],
        out_specs=row,
        out_shape=jax.ShapeDtypeStruct((n, D_MODEL), F32),
        compiler_params=pltpu.CompilerParams(dimension_semantics=("arbitrary",),
                                             vmem_limit_bytes=VMEM_LIMIT_BYTES),
        name="ffn",
    )(x, g, wg, wu, wd, fg)


def _proj_kernel(x_ref, g_ref, wqkv_ref, wz_ref, wab_ref, wdfq_ref, wdfkv_ref, cw_ref,
                 qkv_ref, z_ref, ab_ref, df_ref, ext_ref, *, tiles_per_seq):
    halo = SUBLANES

    @pl.when(pl.program_id(0) % tiles_per_seq == 0)
    def _():
        ext_ref[0:halo, :] = jnp.zeros((halo, 3 * DN_WIDTH), F32)

    h = _rms(x_ref[...], g_ref[...], RMS_EPS).astype(BF16)
    pre = _dot(h, wqkv_ref[...])
    ext_ref[halo:halo + PROJ_ROWS, :] = pre
    cw = cw_ref[...]
    y = cw[CONV_WIDTH - 1:CONV_WIDTH, :] * pre
    for j in range(CONV_WIDTH - 1):
        y = y + cw[j:j + 1, :] * ext_ref[pl.ds(halo - (CONV_WIDTH - 1) + j, PROJ_ROWS), :]
    ext_ref[0:halo, :] = ext_ref[PROJ_ROWS:PROJ_ROWS + halo, :]
    qkv_ref[...] = (y * _sigmoid(y)).astype(BF16)
    z_ref[...] = _dot(h, wz_ref[...]).astype(BF16)
    ab_ref[...] = _dot(h, wab_ref[...])
    df_ref[:, :DIFF_WIDTH] = (_dot(h, wdfq_ref[...]) * (DIFF_HEAD_DIM ** -0.5 * LOG2E)).astype(BF16)
    df_ref[:, DIFF_WIDTH:] = _dot(h, wdfkv_ref[...]).astype(BF16)


def _proj(x1, g, wqkv, wz, wab, wdfq, wdfkv, cw, seq):
    n = x1.shape[0]
    row = lambda w: pl.BlockSpec((PROJ_ROWS, w), lambda i: (i, 0))
    return pl.pallas_call(
        functools.partial(_proj_kernel, tiles_per_seq=seq // PROJ_ROWS),
        grid=(n // PROJ_ROWS,),
        in_specs=[row(D_MODEL), _resident((1, D_MODEL)), _resident(wqkv.shape), _resident(wz.shape),
                  _resident(wab.shape), _resident(wdfq.shape), _resident(wdfkv.shape), _resident(cw.shape)],
        out_specs=[row(3 * DN_WIDTH), row(DN_WIDTH), row(LANES), row(3 * DIFF_WIDTH)],
        out_shape=[jax.ShapeDtypeStruct((n, 3 * DN_WIDTH), BF16),
                   jax.ShapeDtypeStruct((n, DN_WIDTH), BF16),
                   jax.ShapeDtypeStruct((n, LANES), F32),
                   jax.ShapeDtypeStruct((n, 3 * DIFF_WIDTH), BF16)],
        scratch_shapes=[pltpu.VMEM((PROJ_ROWS + SUBLANES, 3 * DN_WIDTH), F32)],
        compiler_params=pltpu.CompilerParams(dimension_semantics=("arbitrary",),
                                             vmem_limit_bytes=VMEM_LIMIT_BYTES),
        name="proj",
    )(x1, g, wqkv, wz, wab, wdfq, wdfkv, cw)


def _gdn_kernel(q_ref, k_ref, v_ref, ab_ref, z_ref, alog_ref, dtb_ref, onorm_ref, o_ref, s_ref, *, batch):
    c = DN_CHUNK
    hd = DN_HEAD_DIM
    units = [(b, p) for b in range(batch) for p in range(DN_PAIRS)]

    @pl.when(pl.program_id(0) == 0)
    def _():
        s_ref[...] = jnp.zeros(s_ref.shape, F32)

    row = lax.broadcasted_iota(jnp.int32, (c, LANES), 0)
    lane = lax.broadcasted_iota(jnp.int32, (c, LANES), 1)
    col = lane & (hd - 1)
    even_bf = jnp.where(lane < hd, 1.0, 0.0).astype(BF16)
    odd_bf = jnp.where(lane < hd, 0.0, 1.0).astype(BF16)
    incl = col <= row
    strict = col < row
    same16 = (row >> 4) == (col >> 4)
    same32 = (row >> 5) == (col >> 5)
    eye = jnp.where(col == row, 1.0, 0.0).astype(F32)
    r2 = lax.broadcasted_iota(jnp.int32, (LANES, LANES), 0)
    l2 = lax.broadcasted_iota(jnp.int32, (LANES, LANES), 1)
    same_head = (r2 >> 6) == (l2 >> 6)
    head_ones = jnp.where(same_head, 1.0, 0.0).astype(BF16)
    head_ones2 = jnp.concatenate([head_ones, head_ones], axis=0)
    r3 = lax.broadcasted_iota(jnp.int32, (3 * LANES, LANES), 0) & (LANES - 1)
    l3 = lax.broadcasted_iota(jnp.int32, (3 * LANES, LANES), 1)
    rc = lax.broadcasted_iota(jnp.int32, (c, 3 * c), 0)
    cc = lax.broadcasted_iota(jnp.int32, (c, 3 * c), 1) & (c - 1)
    lower3 = jnp.where(cc <= rc, 1.0, 0.0).astype(BF16)
    ones3 = jnp.ones((c, 3 * c), BF16)

    def stack(x):
        xb = x.astype(BF16)
        return jnp.concatenate([xb * even_bf, xb * odd_bf], axis=0)

    def mm(a, b):
        return _dot(a.astype(BF16), b.astype(BF16))

    def mm_nt(a, b):
        return _dot_nt(a.astype(BF16), b.astype(BF16))

    def head_sums(x):
        return _dot(jnp.concatenate(_split_bf16(x, 2), axis=1), head_ones2)

    def exact_rows(lhs3, x):
        return _dot(lhs3, jnp.concatenate(_split_bf16(x, 3), axis=0))

    ab3 = [jnp.concatenate(_split_bf16(ab_ref[b], 3), axis=1) for b in range(batch)]
    q_t, k_t, v_t, g_t, beta_t = [], [], [], [], []
    for b, p in units:
        lanes = slice(p * LANES, (p + 1) * LANES)
        expand_a = jnp.where(r3 == 2 * p + (l3 >> 6), 1.0, 0.0).astype(BF16)
        expand_b = jnp.where(r3 == DN_HEADS + 2 * p + (l3 >> 6), 1.0, 0.0).astype(BF16)
        sp_in = _dot(ab3[b], expand_a) + dtb_ref[0:1, lanes]
        softplus = jnp.maximum(sp_in, 0.0) + jnp.log(1.0 + jnp.exp(-jnp.abs(sp_in)))
        g_t.append(-jnp.exp(alog_ref[0:1, lanes]) * softplus)
        beta_t.append(_sigmoid(_dot(ab3[b], expand_b)))
        q = q_ref[b, :, lanes].astype(F32)
        k = k_ref[b, :, lanes].astype(F32)
        q_t.append(q * lax.rsqrt(head_sums(q * q) + L2_EPS) * (hd ** -0.5))
        k_t.append(k * lax.rsqrt(head_sums(k * k) + L2_EPS))
        v_t.append(v_ref[b, :, lanes].astype(F32))

    states = [s_ref[i] for i in range(len(units))]
    outs = [[] for _ in units]
    for n0 in range(0, GDN_ROWS // c, GDN_PREP_CHUNKS):
        items = [(u, n) for n in range(n0, n0 + GDN_PREP_CHUNKS) for u in range(len(units))]

        def take(per_unit):
            return [per_unit[u][n * c:(n + 1) * c] for u, n in items]

        q = take(q_t)
        k = take(k_t)
        v = take(v_t)
        beta = take(beta_t)
        gc = [exact_rows(lower3, x) for x in take(g_t)]
        gc_row = [exact_rows(ones3, x * eye) for x in gc]
        decay = [jnp.where(incl, jnp.exp(jnp.where(incl, a - b_, 0.0)), 0.0) for a, b_ in zip(gc, gc_row)]
        exp_gc = [jnp.exp(x) for x in gc]
        kb = [a * b_ for a, b_ in zip(k, beta)]
        kq = [mm_nt(jnp.concatenate([a, b_], axis=0), stack(k_)) for a, b_, k_ in zip(kb, q, k)]
        a_mat = [jnp.where(strict, x[:c] * d, 0.0) for x, d in zip(kq, decay)]
        attn = [x[c:] * d for x, d in zip(kq, decay)]

        neg_d = [jnp.where(strict & same16, -x, 0.0) for x in a_mat]
        t_inv = [eye + x for x in neg_d]
        power = [mm(x, stack(x)) for x in neg_d]
        for _ in range(2):
            both = [mm(jnp.concatenate([pw, t], axis=0), stack(pw)) for pw, t in zip(power, t_inv)]
            power = [x[:c] for x in both]
            t_inv = [t + x[c:] for t, x in zip(t_inv, both)]
        t_inv = [t + mm(t, stack(pw)) for t, pw in zip(t_inv, power)]
        for block in (strict & same32 & ~same16, strict & ~same32):
            left = [mm(t, stack(jnp.where(block, x, 0.0))) for t, x in zip(t_inv, a_mat)]
            t_inv = [t - mm(x, stack(t)) for t, x in zip(t_inv, left)]

        uw = [mm(t, jnp.concatenate([stack(v_ * b_), stack(kb_ * e)], axis=1))
              for t, v_, b_, kb_, e in zip(t_inv, v, beta, kb, exp_gc)]
        q_dec = [a * e for a, e in zip(q, exp_gc)]
        gc_last = [x[c - 1:c, :] for x in gc]
        k_dec_t = [(k_ * jnp.exp(gl - g_)).T for k_, gl, g_ in zip(k, gc_last, gc)]

        for first in range(0, len(items), len(units)):
            sel = slice(first, first + len(units))
            ws = [mm(jnp.concatenate([x[:, LANES:], qd], axis=0), st)
                  for x, qd, st in zip(uw[sel], q_dec[sel], states)]
            v_new = [x[:, :LANES] - y[:c] for x, y in zip(uw[sel], ws)]
            o_n = [y[c:] + mm(a, stack(vn)) for y, a, vn in zip(ws, attn[sel], v_new)]
            for lst, x in zip(outs, o_n):
                lst.append(x)
            states = [st * jnp.exp(gl) + jnp.where(same_head, mm(kt, vn), 0.0)
                      for st, gl, kt, vn in zip(states, gc_last[sel], k_dec_t[sel], v_new)]

    for i, (b, p) in enumerate(units):
        lanes = slice(p * LANES, (p + 1) * LANES)
        s_ref[i] = states[i]
        o = jnp.concatenate(outs[i], axis=0)
        ms = head_sums(o * o) * (1.0 / hd)
        z = z_ref[b, :, lanes].astype(F32)
        o_ref[b, :, lanes] = (o * lax.rsqrt(ms + RMS_EPS) * onorm_ref[0:1, lanes] * (z * _sigmoid(z))).astype(BF16)


def _gdn(qkv, ab, z, alog_lanes, dtb_lanes, onorm_lanes, batch, seq):
    qkv3 = qkv.reshape(batch, seq, 3 * DN_WIDTH)

    def tile(width, col):
        return pl.BlockSpec((batch, GDN_ROWS, width), lambda t: (0, t, col))

    return pl.pallas_call(
        functools.partial(_gdn_kernel, batch=batch),
        grid=(seq // GDN_ROWS,),
        in_specs=[tile(DN_WIDTH, 0), tile(DN_WIDTH, 1), tile(DN_WIDTH, 2), tile(LANES, 0), tile(DN_WIDTH, 0),
                  _resident(alog_lanes.shape), _resident(dtb_lanes.shape), _resident(onorm_lanes.shape)],
        out_specs=tile(DN_WIDTH, 0),
        out_shape=jax.ShapeDtypeStruct((batch, seq, DN_WIDTH), BF16),
        scratch_shapes=[pltpu.VMEM((batch * DN_PAIRS, LANES, LANES), F32)],
        compiler_params=pltpu.CompilerParams(dimension_semantics=("arbitrary",),
                                             vmem_limit_bytes=VMEM_LIMIT_BYTES),
        name="gdn",
    )(qkv3, qkv3, qkv3, ab.reshape(batch, seq, LANES), z.reshape(batch, seq, DN_WIDTH),
      alog_lanes, dtb_lanes, onorm_lanes).reshape(batch * seq, DN_WIDTH)


def _attn_kernel(off_ref, coef_ref, q_ref, k_ref, v_ref, qcols_ref, kcols_ref, vcols_ref, lam_ref, subln_ref,
                 o_ref, qa_ref, m_ref, acc_ref, k2max_ref, *, lam_init):
    h = pl.program_id(1)
    i = pl.program_id(2)
    d = DIFF_HEAD_DIM
    tq = ATTN_Q
    tk = ATTN_K
    rg = ATTN_ROW_GROUP

    q = q_ref[...]
    first = lax.broadcasted_iota(jnp.int32, q.shape, 1) < d
    qcols = jnp.broadcast_to(qcols_ref[0, 0:1, :], (tq, LANES))
    qa_ref[0:tq, :] = jnp.concatenate([jnp.where(first, q, 0), qcols], axis=1)
    qa_ref[tq:2 * tq, :] = jnp.concatenate([jnp.where(first, 0, q), qcols], axis=1)
    m_ref[...] = jnp.full(m_ref.shape, -jnp.inf, F32)
    acc_ref[...] = jnp.zeros(acc_ref.shape, F32)
    kcols = kcols_ref[...]
    vcols = vcols_ref[...]

    def max_sq_norm(x):
        sq = x.astype(F32) * x.astype(F32)
        half = lax.broadcasted_iota(jnp.int32, x.shape, 1) < d
        both = jnp.maximum(jnp.sum(jnp.where(half, sq, 0.0), axis=-1, keepdims=True),
                           jnp.sum(jnp.where(half, 0.0, sq), axis=-1, keepdims=True))
        return jnp.max(both, axis=0, keepdims=True)

    @pl.when(i == 0)
    def _():
        k2max_ref[...] = jnp.broadcast_to(max_sq_norm(k_ref[...]), k2max_ref.shape)

    qk_bound = jnp.sqrt(max_sq_norm(q) * k2max_ref[0:1, 0:1]) * NORM_SLACK
    gap = (SKIP_LOG2 + 2.0 * qk_bound) / coef_ref[h]
    first_block = jnp.floor(((i * tq).astype(F32) - gap) * (1.0 / tk))
    j0 = jnp.clip(first_block, 0.0, i.astype(F32)).astype(jnp.int32)[0, 0]

    groups = range(0, 2 * tq, rg)

    def kv_steps(blocks):
        scores, values, offs = [], [], []
        for j, masked in blocks:
            start = pl.multiple_of(j * tk, tk)
            ka = jnp.concatenate([k_ref[pl.ds(start, tk), :], kcols], axis=1)
            widths = [(r0 % tq) + rg if masked else tk for r0 in groups]
            scores.append([_dot_nt(qa_ref[r0:r0 + rg, :], ka[:w]) for r0, w in zip(groups, widths)])
            values.append(jnp.concatenate([v_ref[pl.ds(start, tk), :], vcols], axis=1))
            offs.append(off_ref[h] * j.astype(F32))
        for (_, masked), block_scores, va, off in zip(blocks, scores, values, offs):
            for r0, s in zip(groups, block_scores):
                rows = slice(r0, r0 + rg)
                width = s.shape[1]
                if masked:
                    qrow = lax.broadcasted_iota(jnp.int32, (rg, width), 0) + (r0 % tq)
                    kcol = lax.broadcasted_iota(jnp.int32, (rg, width), 1)
                    s = jnp.where(kcol <= qrow, s, -jnp.inf)
                m_old = m_ref[rows, :]
                m_blk = jnp.broadcast_to(jnp.max(s, axis=-1, keepdims=True), (rg, LANES)) + off
                m_new = jnp.maximum(m_old, m_blk)
                alpha = jnp.exp2(m_old - m_new)
                shift = m_new - off
                p = jnp.concatenate([jnp.exp2(s[:, c0:c0 + LANES] - shift) for c0 in range(0, width, LANES)],
                                    axis=1).astype(BF16)
                acc_ref[rows, :] = (jnp.concatenate([alpha, alpha], axis=1) * acc_ref[rows, :]
                                    + _dot(p, va[:width]))
                m_ref[rows, :] = m_new

    def body(t, carry):
        kv_steps([(j0 + 2 * t, False), (j0 + 2 * t + 1, False)])
        return carry

    n_past = i - j0
    lax.fori_loop(0, n_past // 2, body, 0)

    @pl.when(n_past % 2 == 1)
    def _():
        kv_steps([(i - 1, False), (i, True)])

    @pl.when(n_past % 2 == 0)
    def _():
        kv_steps([(i, True)])

    lam_rows = lam_ref[...]
    lam = (jnp.exp(jnp.sum(lam_rows[0:1] * lam_rows[1:2], axis=-1, keepdims=True))
           - jnp.exp(jnp.sum(lam_rows[2:3] * lam_rows[3:4], axis=-1, keepdims=True)) + lam_init)
    acc1 = acc_ref[0:tq, :]
    acc2 = acc_ref[tq:2 * tq, :]
    o = acc1[:, :LANES] / acc1[:, LANES:] - lam * (acc2[:, :LANES] / acc2[:, LANES:])
    o_ref[...] = (_rms(o, subln_ref[...], SUBLN_EPS) * (1.0 - lam_init)).astype(BF16)


def _attn(slopes, df, lam_rows, subln, batch, seq, lam_init):
    assert ATTN_Q == ATTN_K and ATTN_K <= 256 * POS_LO and ATTN_Q % ATTN_ROW_GROUP == 0
    nq = seq // ATTN_Q
    df3 = df.reshape(batch, seq, 3 * DIFF_WIDTH)
    coef = slopes * LOG2E
    c_parts = jnp.stack(_split_bf16(coef, 3), axis=1)
    qcols = jnp.zeros((DIFF_HEADS, SUBLANES, LANES), BF16).at[:, :, 0:6].set(
        jnp.broadcast_to(jnp.tile(c_parts, (1, 2))[:, None, :], (DIFF_HEADS, SUBLANES, 6)))
    pos = np.arange(ATTN_K)
    kcols_np = np.zeros((ATTN_K, LANES), np.float32)
    kcols_np[:, 0:3] = (pos // POS_LO * POS_LO)[:, None]
    kcols_np[:, 3:6] = (pos % POS_LO)[:, None]
    kcols = jnp.asarray(kcols_np, BF16)
    vcols = jnp.ones((ATTN_K, LANES), BF16)
    block_off = coef * ATTN_K

    seq_block = lambda col0: pl.BlockSpec((None, seq, LANES), lambda b, h, i: (b, 0, col0 + h))
    const = lambda shape: pl.BlockSpec(shape, lambda b, h, i: (0,) * len(shape))
    return pl.pallas_call(
        functools.partial(_attn_kernel, lam_init=lam_init),
        grid=(batch, DIFF_HEADS, nq),
        in_specs=[pl.BlockSpec(memory_space=pltpu.SMEM), pl.BlockSpec(memory_space=pltpu.SMEM),
                  pl.BlockSpec((None, ATTN_Q, LANES), lambda b, h, i: (b, i, h)),
                  seq_block(DIFF_HEADS), seq_block(2 * DIFF_HEADS),
                  pl.BlockSpec((1, SUBLANES, LANES), lambda b, h, i: (h, 0, 0)),
                  const((ATTN_K, LANES)), const((ATTN_K, LANES)),
                  const((SUBLANES, LANES)), const((1, DIFF_V_DIM))],
        out_specs=pl.BlockSpec((None, ATTN_Q, LANES), lambda b, h, i: (b, i, h)),
        out_shape=jax.ShapeDtypeStruct((batch, seq, DIFF_WIDTH), BF16),
        scratch_shapes=[pltpu.VMEM((2 * ATTN_Q, 2 * LANES), BF16),
                        pltpu.VMEM((2 * ATTN_Q, LANES), F32),
                        pltpu.VMEM((2 * ATTN_Q, 2 * LANES), F32),
                        pltpu.VMEM((SUBLANES, LANES), F32)],
        compiler_params=pltpu.CompilerParams(dimension_semantics=("arbitrary", "arbitrary", "arbitrary"),
                                             vmem_limit_bytes=VMEM_LIMIT_BYTES),
        name="diff_attn",
    )(block_off, coef, df3, df3, df3, qcols, kcols, vcols, lam_rows, subln).reshape(batch * seq, DIFF_WIDTH)


def _merge_ffn_kernel(x_ref, g_ref, oa_ref, ob_ref, wa_ref, wb_ref, wgate_ref, wout_ref,
                      fg_norm_ref, wg_ref, wu_ref, wd_ref, fg_ref, o_ref, *, final_norm):
    x = x_ref[...]
    h = _rms(x, g_ref[...], RMS_EPS).astype(BF16)
    gates = _sigmoid(_dot(h, wgate_ref[...]))
    y_a = _dot(oa_ref[...], wa_ref[...])
    y_b = _dot(ob_ref[...], wb_ref[...])
    mixed = (gates[:, :D_MODEL] * y_a + gates[:, D_MODEL:] * y_b).astype(BF16)
    x = x + _dot(mixed, wout_ref[...])
    o_ref[...] = _ffn_half_step(x, fg_norm_ref, wg_ref, wu_ref, wd_ref, fg_ref, final_norm)


def _merge_ffn(x1, g, o_a, o_b, wa, wb, wgate, wout, ffn_g, wg, wu, wd, fg, final_norm):
    n = x1.shape[0]
    row = lambda w: pl.BlockSpec((MERGE_ROWS, w), lambda i: (i, 0))
    return pl.pallas_call(
        functools.partial(_merge_ffn_kernel, final_norm=final_norm),
        grid=(n // MERGE_ROWS,),
        in_specs=[row(D_MODEL), _resident((1, D_MODEL)), row(DN_WIDTH), row(DIFF_WIDTH),
                  _resident(wa.shape), _resident(wb.shape), _resident(wgate.shape), _resident(wout.shape),
                  _resident((1, D_MODEL)), _resident(wg.shape), _resident(wu.shape), _resident(wd.shape),
                  _resident((1, D_MODEL))],
        out_specs=row(D_MODEL),
        out_shape=jax.ShapeDtypeStruct((n, D_MODEL), F32),
        compiler_params=pltpu.CompilerParams(dimension_semantics=("arbitrary",),
                                             vmem_limit_bytes=VMEM_LIMIT_BYTES),
        name="merge_ffn",
    )(x1, g, o_a, o_b, wa, wb, wgate, wout, ffn_g, wg, wu, wd, fg)


def _lambda_init(layer_idx):
    return 0.8 - 0.6 * math.exp(-0.3 * layer_idx)


def _head_lanes(per_head):
    lanes = jnp.repeat(per_head.astype(F32), DN_HEAD_DIM)
    return jnp.broadcast_to(lanes[None, :], (SUBLANES, DN_WIDTH))


def kernel(x, ffn1_norm, ffn1_w_gate, ffn1_w_up, ffn1_w_down, mix_norm, w_in, conv_qkv, dn_a_log, dn_dt_bias, dn_out_norm, diff_lambda_q1, diff_lambda_k1, diff_lambda_q2, diff_lambda_k2, diff_subln, w_branch_a, w_branch_b, w_out, ffn2_norm, ffn2_w_gate, ffn2_w_up, ffn2_w_down, final_norm):
    batch, seq, _ = x.shape
    depth = ffn1_norm.shape[0]
    n = batch * seq
    xs = x.reshape(n, D_MODEL)
    slopes = jnp.asarray([2.0 ** (-8.0 * (i + 1) / DIFF_HEADS) for i in range(DIFF_HEADS)], F32)
    final_g = final_norm.reshape(1, D_MODEL)

    c_qkv = 3 * DN_WIDTH
    c_z = c_qkv + DN_WIDTH
    c_ab = c_z + 2 * DN_HEADS
    c_dfq = c_ab + DIFF_WIDTH
    c_df = c_ab + 3 * DIFF_WIDTH

    for l in range(depth):
        xs = _ffn(xs, ffn1_norm[l].reshape(1, D_MODEL), ffn1_w_gate[l].astype(BF16), ffn1_w_up[l].astype(BF16),
                  ffn1_w_down[l].astype(BF16), final_g, final_norm=False)

        w = w_in[l]
        wab = jnp.pad(w[:, c_z:c_ab], ((0, 0), (0, LANES - 2 * DN_HEADS))).astype(BF16)
        qkv, z, ab, df = _proj(xs, mix_norm[l].reshape(1, D_MODEL), w[:, :c_qkv].astype(BF16),
                               w[:, c_qkv:c_z].astype(BF16), wab, w[:, c_ab:c_dfq].astype(BF16),
                               w[:, c_dfq:c_df].astype(BF16), conv_qkv[l], seq)

        o_a = _gdn(qkv, ab, z, _head_lanes(dn_a_log[l]), _head_lanes(dn_dt_bias[l]),
                   jnp.broadcast_to(jnp.tile(dn_out_norm[l].astype(F32), DN_HEADS)[None, :], (SUBLANES, DN_WIDTH)),
                   batch, seq)

        lam_rows = jnp.zeros((SUBLANES, LANES), F32).at[0:4, :DIFF_HEAD_DIM].set(
            jnp.stack([diff_lambda_q1[l], diff_lambda_k1[l], diff_lambda_q2[l], diff_lambda_k2[l]]).astype(F32))
        o_b = _attn(slopes, df, lam_rows, diff_subln[l].reshape(1, DIFF_V_DIM), batch, seq, _lambda_init(l))

        xs = _merge_ffn(xs, mix_norm[l].reshape(1, D_MODEL), o_a, o_b, w_branch_a[l].astype(BF16),
                        w_branch_b[l].astype(BF16), w[:, c_df:].astype(BF16), w_out[l].astype(BF16),
                        ffn2_norm[l].reshape(1, D_MODEL), ffn2_w_gate[l].astype(BF16), ffn2_w_up[l].astype(BF16),
                        ffn2_w_down[l].astype(BF16), final_g, final_norm=(l == depth - 1))

    return xs.reshape(batch, seq, D_MODEL)
```

```python
import functools
import math

import jax
import jax.numpy as jnp
import numpy as np
from jax import lax
from jax.experimental import pallas as pl
from jax.experimental.pallas import tpu as pltpu

F32 = jnp.float32
BF16 = jnp.bfloat16

D_MODEL = 1024
D_FF = 2816
RMS_EPS = 1e-6
SUBLN_EPS = 1e-5
L2_EPS = 1e-6
CONV_WIDTH = 4
DN_HEADS = 8
DN_HEAD_DIM = 64
DN_WIDTH = DN_HEADS * DN_HEAD_DIM
DN_CHUNK = 64
DN_PAIRS = DN_HEADS // 2
DIFF_HEADS = 4
DIFF_HEAD_DIM = 64
DIFF_V_DIM = 2 * DIFF_HEAD_DIM
DIFF_WIDTH = DIFF_HEADS * DIFF_V_DIM
LANES = 128
SUBLANES = 8
LOG2E = math.log2(math.e)

VMEM_LIMIT_BYTES = 56 * 1024 * 1024

FFN_ROWS = 512
FF_CHUNKS = ((0, 512), (512, 1024), (1024, 1536), (1536, 2048), (2048, 2560), (2560, 2816))
PROJ_ROWS = 512
GDN_ROWS = 256
GDN_PREP_CHUNKS = 2
ATTN_Q = 512
ATTN_K = 512
ATTN_ROW_GROUP = 512
ATTN_DIAG_GROUP = 128
MERGE_ROWS = 512
POS_LO = 16
SKIP_LOG2 = 160.0
NORM_SLACK = 1.01


def _rms(x, g, eps):
    return x * lax.rsqrt(jnp.mean(x * x, axis=-1, keepdims=True) + eps) * g


def _sigmoid(x):
    return 1.0 / (1.0 + jnp.exp(-x))


def _dot(a, b):
    return jnp.dot(a, b, preferred_element_type=F32)


def _dot_nt(a, b):
    return lax.dot_general(a, b, (((1,), (1,)), ((), ())), preferred_element_type=F32)


def _split_bf16(x, terms):
    parts = []
    rest = x
    for t in range(terms):
        part = rest.astype(BF16)
        parts.append(part)
        if t + 1 < terms:
            rest = rest - part.astype(F32)
    return parts


def _resident(shape):
    return pl.BlockSpec(shape, lambda *_: (0,) * len(shape), pipeline_mode=pl.Buffered(1))


def _ffn_half_step(x, g_ref, wg_ref, wu_ref, wd_ref, fg_ref, final_norm):
    h = _rms(x, g_ref[...], RMS_EPS).astype(BF16)
    acc = None
    for c0, c1 in FF_CHUNKS:
        gate = _dot(h, wg_ref[:, c0:c1])
        up = _dot(h, wu_ref[:, c0:c1])
        act = (gate * _sigmoid(gate) * up).astype(BF16)
        part = _dot(act, wd_ref[c0:c1, :])
        acc = part if acc is None else acc + part
    y = x + 0.5 * acc
    if final_norm:
        y = _rms(y, fg_ref[...], RMS_EPS)
    return y


def _ffn_kernel(x_ref, g_ref, wg_ref, wu_ref, wd_ref, fg_ref, o_ref, *, final_norm):
    o_ref[...] = _ffn_half_step(x_ref[...], g_ref, wg_ref, wu_ref, wd_ref, fg_ref, final_norm)


def _ffn(x, g, wg, wu, wd, fg, final_norm):
    n = x.shape[0]
    row = pl.BlockSpec((FFN_ROWS, D_MODEL), lambda i: (i, 0))
    return pl.pallas_call(
        functools.partial(_ffn_kernel, final_norm=final_norm),
        grid=(n // FFN_ROWS,),
        in_specs=[row, _resident((1, D_MODEL)), _resident((D_MODEL, D_FF)), _resident((D_MODEL, D_FF)),
                  _resident((D_FF, D_MODEL)), _resident((1, D_MODEL))],
        out_specs=row,
        out_shape=jax.ShapeDtypeStruct((n, D_MODEL), F32),
        compiler_params=pltpu.CompilerParams(dimension_semantics=("arbitrary",),
                                             vmem_limit_bytes=VMEM_LIMIT_BYTES),
        name="ffn",
    )(x, g, wg, wu, wd, fg)


def _proj_kernel(x_ref, g_ref, wqkv_ref, wz_ref, wab_ref, wdfq_ref, wdfkv_ref, cw_ref,
                 qkv_ref, z_ref, ab_ref, df_ref, ext_ref, *, tiles_per_seq):
    halo = SUBLANES

    @pl.when(pl.program_id(0) % tiles_per_seq == 0)
    def _():
        ext_ref[0:halo, :] = jnp.zeros((halo, 3 * DN_WIDTH), F32)

    h = _rms(x_ref[...], g_ref[...], RMS_EPS).astype(BF16)
    pre = _dot(h, wqkv_ref[...])
    ext_ref[halo:halo + PROJ_ROWS, :] = pre
    cw = cw_ref[...]
    y = cw[CONV_WIDTH - 1:CONV_WIDTH, :] * pre
    for j in range(CONV_WIDTH - 1):
        y = y + cw[j:j + 1, :] * ext_ref[pl.ds(halo - (CONV_WIDTH - 1) + j, PROJ_ROWS), :]
    ext_ref[0:halo, :] = ext_ref[PROJ_ROWS:PROJ_ROWS + halo, :]
    qkv_ref[...] = (y * _sigmoid(y)).astype(BF16)
    z_ref[...] = _dot(h, wz_ref[...]).astype(BF16)
    ab_ref[...] = _dot(h, wab_ref[...])
    df_ref[:, :DIFF_WIDTH] = (_dot(h, wdfq_ref[...]) * (DIFF_HEAD_DIM ** -0.5 * LOG2E)).astype(BF16)
    df_ref[:, DIFF_WIDTH:] = _dot(h, wdfkv_ref[...]).astype(BF16)


def _proj(x1, g, wqkv, wz, wab, wdfq, wdfkv, cw, seq):
    n = x1.shape[0]
    row = lambda w: pl.BlockSpec((PROJ_ROWS, w), lambda i: (i, 0))
    return pl.pallas_call(
        functools.partial(_proj_kernel, tiles_per_seq=seq // PROJ_ROWS),
        grid=(n // PROJ_ROWS,),
        in_specs=[row(D_MODEL), _resident((1, D_MODEL)), _resident(wqkv.shape), _resident(wz.shape),
                  _resident(wab.shape), _resident(wdfq.shape), _resident(wdfkv.shape), _resident(cw.shape)],
        out_specs=[row(3 * DN_WIDTH), row(DN_WIDTH), row(LANES), row(3 * DIFF_WIDTH)],
        out_shape=[jax.ShapeDtypeStruct((n, 3 * DN_WIDTH), BF16),
                   jax.ShapeDtypeStruct((n, DN_WIDTH), BF16),
                   jax.ShapeDtypeStruct((n, LANES), F32),
                   jax.ShapeDtypeStruct((n, 3 * DIFF_WIDTH), BF16)],
        scratch_shapes=[pltpu.VMEM((PROJ_ROWS + SUBLANES, 3 * DN_WIDTH), F32)],
        compiler_params=pltpu.CompilerParams(dimension_semantics=("arbitrary",),
                                             vmem_limit_bytes=VMEM_LIMIT_BYTES),
        name="proj",
    )(x1, g, wqkv, wz, wab, wdfq, wdfkv, cw)


def _gdn_kernel(q_ref, k_ref, v_ref, ab_ref, z_ref, alog_ref, dtb_ref, onorm_ref, o_ref, s_ref, *, batch):
    c = DN_CHUNK
    hd = DN_HEAD_DIM
    units = [(b, p) for b in range(batch) for p in range(DN_PAIRS)]

    @pl.when(pl.program_id(0) == 0)
    def _():
        s_ref[...] = jnp.zeros(s_ref.shape, F32)

    row = lax.broadcasted_iota(jnp.int32, (c, LANES), 0)
    lane = lax.broadcasted_iota(jnp.int32, (c, LANES), 1)
    col = lane & (hd - 1)
    even_bf = jnp.where(lane < hd, 1.0, 0.0).astype(BF16)
    odd_bf = jnp.where(lane < hd, 0.0, 1.0).astype(BF16)
    incl = col <= row
    strict = col < row
    same16 = (row >> 4) == (col >> 4)
    same32 = (row >> 5) == (col >> 5)
    eye = jnp.where(col == row, 1.0, 0.0).astype(F32)
    r2 = lax.broadcasted_iota(jnp.int32, (LANES, LANES), 0)
    l2 = lax.broadcasted_iota(jnp.int32, (LANES, LANES), 1)
    same_head = (r2 >> 6) == (l2 >> 6)
    head_ones = jnp.where(same_head, 1.0, 0.0).astype(BF16)
    head_ones2 = jnp.concatenate([head_ones, head_ones], axis=0)
    r3 = lax.broadcasted_iota(jnp.int32, (3 * LANES, LANES), 0) & (LANES - 1)
    l3 = lax.broadcasted_iota(jnp.int32, (3 * LANES, LANES), 1)
    rc = lax.broadcasted_iota(jnp.int32, (c, 3 * c), 0)
    cc = lax.broadcasted_iota(jnp.int32, (c, 3 * c), 1) & (c - 1)
    lower3 = jnp.where(cc <= rc, 1.0, 0.0).astype(BF16)
    ones3 = jnp.ones((c, 3 * c), BF16)

    def stack(x):
        xb = x.astype(BF16)
        return jnp.concatenate([xb * even_bf, xb * odd_bf], axis=0)

    def mm(a, b):
        return _dot(a.astype(BF16), b.astype(BF16))

    def mm_nt(a, b):
        return _dot_nt(a.astype(BF16), b.astype(BF16))

    def head_sums(x):
        return _dot(jnp.concatenate(_split_bf16(x, 2), axis=1), head_ones2)

    def exact_rows(lhs3, x):
        return _dot(lhs3, jnp.concatenate(_split_bf16(x, 3), axis=0))

    ab3 = [jnp.concatenate(_split_bf16(ab_ref[b], 3), axis=1) for b in range(batch)]
    q_t, k_t, v_t, g_t, beta_t = [], [], [], [], []
    for b, p in units:
        lanes = slice(p * LANES, (p + 1) * LANES)
        expand_a = jnp.where(r3 == 2 * p + (l3 >> 6), 1.0, 0.0).astype(BF16)
        expand_b = jnp.where(r3 == DN_HEADS + 2 * p + (l3 >> 6), 1.0, 0.0).astype(BF16)
        sp_in = _dot(ab3[b], expand_a) + dtb_ref[0:1, lanes]
        softplus = jnp.maximum(sp_in, 0.0) + jnp.log(1.0 + jnp.exp(-jnp.abs(sp_in)))
        g_t.append(-jnp.exp(alog_ref[0:1, lanes]) * softplus)
        beta_t.append(_sigmoid(_dot(ab3[b], expand_b)))
        q = q_ref[b, :, lanes].astype(F32)
        k = k_ref[b, :, lanes].astype(F32)
        q_t.append(q * lax.rsqrt(head_sums(q * q) + L2_EPS) * (hd ** -0.5))
        k_t.append(k * lax.rsqrt(head_sums(k * k) + L2_EPS))
        v_t.append(v_ref[b, :, lanes].astype(F32))

    states = [s_ref[i] for i in range(len(units))]
    outs = [[] for _ in units]
    for n0 in range(0, GDN_ROWS // c, GDN_PREP_CHUNKS):
        items = [(u, n) for n in range(n0, n0 + GDN_PREP_CHUNKS) for u in range(len(units))]

        def take(per_unit):
            return [per_unit[u][n * c:(n + 1) * c] for u, n in items]

        q = take(q_t)
        k = take(k_t)
        v = take(v_t)
        beta = take(beta_t)
        gc = [exact_rows(lower3, x) for x in take(g_t)]
        gc_row = [exact_rows(ones3, x * eye) for x in gc]
        decay = [jnp.where(incl, jnp.exp(jnp.where(incl, a - b_, 0.0)), 0.0) for a, b_ in zip(gc, gc_row)]
        exp_gc = [jnp.exp(x) for x in gc]
        kb = [a * b_ for a, b_ in zip(k, beta)]
        kq = [mm_nt(jnp.concatenate([a, b_], axis=0), stack(k_)) for a, b_, k_ in zip(kb, q, k)]
        a_mat = [jnp.where(strict, x[:c] * d, 0.0) for x, d in zip(kq, decay)]
        attn = [x[c:] * d for x, d in zip(kq, decay)]

        neg_d = [jnp.where(strict & same16, -x, 0.0) for x in a_mat]
        t_inv = [eye + x for x in neg_d]
        power = [mm(x, stack(x)) for x in neg_d]
        for _ in range(2):
            both = [mm(jnp.concatenate([pw, t], axis=0), stack(pw)) for pw, t in zip(power, t_inv)]
            power = [x[:c] for x in both]
            t_inv = [t + x[c:] for t, x in zip(t_inv, both)]
        t_inv = [t + mm(t, stack(pw)) for t, pw in zip(t_inv, power)]
        for block in (strict & same32 & ~same16, strict & ~same32):
            left = [mm(t, stack(jnp.where(block, x, 0.0))) for t, x in zip(t_inv, a_mat)]
            t_inv = [t - mm(x, stack(t)) for t, x in zip(t_inv, left)]

        uw = [mm(t, jnp.concatenate([stack(v_ * b_), stack(kb_ * e)], axis=1))
              for t, v_, b_, kb_, e in zip(t_inv, v, beta, kb, exp_gc)]
        q_dec = [a * e for a, e in zip(q, exp_gc)]
        gc_last = [x[c - 1:c, :] for x in gc]
        k_dec_t = [(k_ * jnp.exp(gl - g_)).T for k_, gl, g_ in zip(k, gc_last, gc)]

        for first in range(0, len(items), len(units)):
            sel = slice(first, first + len(units))
            ws = [mm(jnp.concatenate([x[:, LANES:], qd], axis=0), st)
                  for x, qd, st in zip(uw[sel], q_dec[sel], states)]
            v_new = [x[:, :LANES] - y[:c] for x, y in zip(uw[sel], ws)]
            o_n = [y[c:] + mm(a, stack(vn)) for y, a, vn in zip(ws, attn[sel], v_new)]
            for lst, x in zip(outs, o_n):
                lst.append(x)
            states = [st * jnp.exp(gl) + jnp.where(same_head, mm(kt, vn), 0.0)
                      for st, gl, kt, vn in zip(states, gc_last[sel], k_dec_t[sel], v_new)]

    for i, (b, p) in enumerate(units):
        lanes = slice(p * LANES, (p + 1) * LANES)
        s_ref[i] = states[i]
        o = jnp.concatenate(outs[i], axis=0)
        ms = head_sums(o * o) * (1.0 / hd)
        z = z_ref[b, :, lanes].astype(F32)
        o_ref[b, :, lanes] = (o * lax.rsqrt(ms + RMS_EPS) * onorm_ref[0:1, lanes] * (z * _sigmoid(z))).astype(BF16)


def _gdn(qkv, ab, z, alog_lanes, dtb_lanes, onorm_lanes, batch, seq):
    qkv3 = qkv.reshape(batch, seq, 3 * DN_WIDTH)

    def tile(width, col):
        return pl.BlockSpec((batch, GDN_ROWS, width), lambda t: (0, t, col))

    return pl.pallas_call(
        functools.partial(_gdn_kernel, batch=batch),
        grid=(seq // GDN_ROWS,),
        in_specs=[tile(DN_WIDTH, 0), tile(DN_WIDTH, 1), tile(DN_WIDTH, 2), tile(LANES, 0), tile(DN_WIDTH, 0),
                  _resident(alog_lanes.shape), _resident(dtb_lanes.shape), _resident(onorm_lanes.shape)],
        out_specs=tile(DN_WIDTH, 0),
        out_shape=jax.ShapeDtypeStruct((batch, seq, DN_WIDTH), BF16),
        scratch_shapes=[pltpu.VMEM((batch * DN_PAIRS, LANES, LANES), F32)],
        compiler_params=pltpu.CompilerParams(dimension_semantics=("arbitrary",),
                                             vmem_limit_bytes=VMEM_LIMIT_BYTES),
        name="gdn",
    )(qkv3, qkv3, qkv3, ab.reshape(batch, seq, LANES), z.reshape(batch, seq, DN_WIDTH),
      alog_lanes, dtb_lanes, onorm_lanes).reshape(batch * seq, DN_WIDTH)


def _attn_kernel(off_ref, coef_ref, q_ref, k_ref, v_ref, qcols_ref, kcols_ref, vcols_ref, lam_ref, subln_ref,
                 o_ref, qa_ref, m_ref, acc_ref, k2max_ref, *, lam_init):
    h = pl.program_id(1)
    i = pl.program_id(2)
    d = DIFF_HEAD_DIM
    tq = ATTN_Q
    tk = ATTN_K

    q = q_ref[...]
    first = lax.broadcasted_iota(jnp.int32, q.shape, 1) < d
    qcols = jnp.broadcast_to(qcols_ref[0, 0:1, :], (tq, LANES))
    qa_ref[0:tq, :] = jnp.concatenate([jnp.where(first, q, 0), qcols], axis=1)
    qa_ref[tq:2 * tq, :] = jnp.concatenate([jnp.where(first, 0, q), qcols], axis=1)
    m_ref[...] = jnp.full(m_ref.shape, -jnp.inf, F32)
    acc_ref[...] = jnp.zeros(acc_ref.shape, F32)
    kcols = kcols_ref[...]
    vcols = vcols_ref[...]

    def max_sq_norm(x):
        sq = x.astype(F32) * x.astype(F32)
        r = lax.broadcasted_iota(jnp.int32, (LANES, LANES), 0)
        c = lax.broadcasted_iota(jnp.int32, (LANES, LANES), 1)
        same_half = jnp.where((r < d) == (c < d), 1.0, 0.0).astype(BF16)
        half_sums = _dot(sq.astype(BF16), same_half)
        return jnp.max(jnp.max(half_sums, axis=0, keepdims=True), axis=1, keepdims=True)

    @pl.when(i == 0)
    def _():
        k2max_ref[...] = jnp.broadcast_to(max_sq_norm(k_ref[...]), k2max_ref.shape)

    qk_bound = jnp.sqrt(max_sq_norm(q) * k2max_ref[0:1, 0:1]) * NORM_SLACK
    gap = (SKIP_LOG2 + 2.0 * qk_bound) / coef_ref[h]
    first_block = jnp.floor(((i * tq).astype(F32) - gap) * (1.0 / tk))
    j0 = jnp.clip(first_block, 0.0, i.astype(F32)).astype(jnp.int32)[0, 0]

    def kv_steps(blocks):
        scores, values, offs = [], [], []
        for j, masked in blocks:
            start = pl.multiple_of(j * tk, tk)
            ka = jnp.concatenate([k_ref[pl.ds(start, tk), :], kcols], axis=1)
            rg = ATTN_DIAG_GROUP if masked else ATTN_ROW_GROUP
            groups = range(0, 2 * tq, rg)
            widths = [(r0 % tq) + rg if masked else tk for r0 in groups]
            scores.append([(r0, _dot_nt(qa_ref[r0:r0 + rg, :], ka[:w])) for r0, w in zip(groups, widths)])
            values.append(jnp.concatenate([v_ref[pl.ds(start, tk), :], vcols], axis=1))
            offs.append(off_ref[h] * j.astype(F32))
        for (_, masked), block_scores, va, off in zip(blocks, scores, values, offs):
            for r0, s in block_scores:
                rg, width = s.shape
                rows = slice(r0, r0 + rg)
                if masked:
                    qrow = lax.broadcasted_iota(jnp.int32, (rg, width), 0) + (r0 % tq)
                    kcol = lax.broadcasted_iota(jnp.int32, (rg, width), 1)
                    s = jnp.where(kcol <= qrow, s, -jnp.inf)
                m_old = m_ref[rows, :]
                m_blk = jnp.broadcast_to(jnp.max(s, axis=-1, keepdims=True), (rg, LANES)) + off
                m_new = jnp.maximum(m_old, m_blk)
                alpha = jnp.exp2(m_old - m_new)
                shift = m_new - off
                p = jnp.concatenate([jnp.exp2(s[:, c0:c0 + LANES] - shift) for c0 in range(0, width, LANES)],
                                    axis=1).astype(BF16)
                acc_ref[rows, :] = (jnp.concatenate([alpha, alpha], axis=1) * acc_ref[rows, :]
                                    + _dot(p, va[:width]))
                m_ref[rows, :] = m_new

    def body4(t, carry):
        kv_steps([(j0 + 4 * t + u, False) for u in range(4)])
        return carry

    def body(t, carry):
        kv_steps([(j0 + 2 * t, False), (j0 + 2 * t + 1, False)])
        return carry

    n_past = i - j0
    n_quads = n_past // 4
    lax.fori_loop(0, n_quads, body4, 0)
    lax.fori_loop(2 * n_quads, n_past // 2, body, 0)

    @pl.when(n_past % 2 == 1)
    def _():
        kv_steps([(i - 1, False), (i, True)])

    @pl.when(n_past % 2 == 0)
    def _():
        kv_steps([(i, True)])

    lam_rows = lam_ref[...]
    lam = (jnp.exp(jnp.sum(lam_rows[0:1] * lam_rows[1:2], axis=-1, keepdims=True))
           - jnp.exp(jnp.sum(lam_rows[2:3] * lam_rows[3:4], axis=-1, keepdims=True)) + lam_init)
    acc1 = acc_ref[0:tq, :]
    acc2 = acc_ref[tq:2 * tq, :]
    o = acc1[:, :LANES] / acc1[:, LANES:] - lam * (acc2[:, :LANES] / acc2[:, LANES:])
    o_ref[...] = (_rms(o, subln_ref[...], SUBLN_EPS) * (1.0 - lam_init)).astype(BF16)


def _attn(slopes, df, lam_rows, subln, batch, seq, lam_init):
    assert ATTN_Q == ATTN_K and ATTN_K <= 256 * POS_LO and ATTN_Q % ATTN_ROW_GROUP == 0
    nq = seq // ATTN_Q
    df3 = df.reshape(batch, seq, 3 * DIFF_WIDTH)
    coef = slopes * LOG2E
    c_parts = jnp.stack(_split_bf16(coef, 3), axis=1)
    qcols = jnp.zeros((DIFF_HEADS, SUBLANES, LANES), BF16).at[:, :, 0:6].set(
        jnp.broadcast_to(jnp.tile(c_parts, (1, 2))[:, None, :], (DIFF_HEADS, SUBLANES, 6)))
    pos = np.arange(ATTN_K)
    kcols_np = np.zeros((ATTN_K, LANES), np.float32)
    kcols_np[:, 0:3] = (pos // POS_LO * POS_LO)[:, None]
    kcols_np[:, 3:6] = (pos % POS_LO)[:, None]
    kcols = jnp.asarray(kcols_np, BF16)
    vcols = jnp.ones((ATTN_K, LANES), BF16)
    block_off = coef * ATTN_K

    seq_block = lambda col0: pl.BlockSpec((None, seq, LANES), lambda b, h, i: (b, 0, col0 + h))
    const = lambda shape: pl.BlockSpec(shape, lambda b, h, i: (0,) * len(shape))
    return pl.pallas_call(
        functools.partial(_attn_kernel, lam_init=lam_init),
        grid=(batch, DIFF_HEADS, nq),
        in_specs=[pl.BlockSpec(memory_space=pltpu.SMEM), pl.BlockSpec(memory_space=pltpu.SMEM),
                  pl.BlockSpec((None, ATTN_Q, LANES), lambda b, h, i: (b, i, h)),
                  seq_block(DIFF_HEADS), seq_block(2 * DIFF_HEADS),
                  pl.BlockSpec((1, SUBLANES, LANES), lambda b, h, i: (h, 0, 0)),
                  const((ATTN_K, LANES)), const((ATTN_K, LANES)),
                  const((SUBLANES, LANES)), const((1, DIFF_V_DIM))],
        out_specs=pl.BlockSpec((None, ATTN_Q, LANES), lambda b, h, i: (b, i, h)),
        out_shape=jax.ShapeDtypeStruct((batch, seq, DIFF_WIDTH), BF16),
        scratch_shapes=[pltpu.VMEM((2 * ATTN_Q, 2 * LANES), BF16),
                        pltpu.VMEM((2 * ATTN_Q, LANES), F32),
                        pltpu.VMEM((2 * ATTN_Q, 2 * LANES), F32),
                        pltpu.VMEM((SUBLANES, LANES), F32)],
        compiler_params=pltpu.CompilerParams(dimension_semantics=("arbitrary", "arbitrary", "arbitrary"),
                                             vmem_limit_bytes=VMEM_LIMIT_BYTES),
        name="diff_attn",
    )(block_off, coef, df3, df3, df3, qcols, kcols, vcols, lam_rows, subln).reshape(batch * seq, DIFF_WIDTH)


def _merge_ffn_kernel(x_ref, g_ref, oa_ref, ob_ref, wa_ref, wb_ref, wgate_ref, wout_ref,
                      fg_norm_ref, wg_ref, wu_ref, wd_ref, fg_ref, o_ref, *, final_norm):
    x = x_ref[...]
    h = _rms(x, g_ref[...], RMS_EPS).astype(BF16)
    gates = _sigmoid(_dot(h, wgate_ref[...]))
    y_a = _dot(oa_ref[...], wa_ref[...])
    y_b = _dot(ob_ref[...], wb_ref[...])
    mixed = (gates[:, :D_MODEL] * y_a + gates[:, D_MODEL:] * y_b).astype(BF16)
    x = x + _dot(mixed, wout_ref[...])
    o_ref[...] = _ffn_half_step(x, fg_norm_ref, wg_ref, wu_ref, wd_ref, fg_ref, final_norm)


def _merge_ffn(x1, g, o_a, o_b, wa, wb, wgate, wout, ffn_g, wg, wu, wd, fg, final_norm):
    n = x1.shape[0]
    row = lambda w: pl.BlockSpec((MERGE_ROWS, w), lambda i: (i, 0))
    return pl.pallas_call(
        functools.partial(_merge_ffn_kernel, final_norm=final_norm),
        grid=(n // MERGE_ROWS,),
        in_specs=[row(D_MODEL), _resident((1, D_MODEL)), row(DN_WIDTH), row(DIFF_WIDTH),
                  _resident(wa.shape), _resident(wb.shape), _resident(wgate.shape), _resident(wout.shape),
                  _resident((1, D_MODEL)), _resident(wg.shape), _resident(wu.shape), _resident(wd.shape),
                  _resident((1, D_MODEL))],
        out_specs=row(D_MODEL),
        out_shape=jax.ShapeDtypeStruct((n, D_MODEL), F32),
        compiler_params=pltpu.CompilerParams(dimension_semantics=("arbitrary",),
                                             vmem_limit_bytes=VMEM_LIMIT_BYTES),
        name="merge_ffn",
    )(x1, g, o_a, o_b, wa, wb, wgate, wout, ffn_g, wg, wu, wd, fg)


def _lambda_init(layer_idx):
    return 0.8 - 0.6 * math.exp(-0.3 * layer_idx)


def _head_lanes(per_head):
    lanes = jnp.repeat(per_head.astype(F32), DN_HEAD_DIM)
    return jnp.broadcast_to(lanes[None, :], (SUBLANES, DN_WIDTH))


def kernel(x, ffn1_norm, ffn1_w_gate, ffn1_w_up, ffn1_w_down, mix_norm, w_in, conv_qkv, dn_a_log, dn_dt_bias, dn_out_norm, diff_lambda_q1, diff_lambda_k1, diff_lambda_q2, diff_lambda_k2, diff_subln, w_branch_a, w_branch_b, w_out, ffn2_norm, ffn2_w_gate, ffn2_w_up, ffn2_w_down, final_norm):
    batch, seq, _ = x.shape
    depth = ffn1_norm.shape[0]
    n = batch * seq
    xs = x.reshape(n, D_MODEL)
    slopes = jnp.asarray([2.0 ** (-8.0 * (i + 1) / DIFF_HEADS) for i in range(DIFF_HEADS)], F32)
    final_g = final_norm.reshape(1, D_MODEL)

    c_qkv = 3 * DN_WIDTH
    c_z = c_qkv + DN_WIDTH
    c_ab = c_z + 2 * DN_HEADS
    c_dfq = c_ab + DIFF_WIDTH
    c_df = c_ab + 3 * DIFF_WIDTH

    for l in range(depth):
        xs = _ffn(xs, ffn1_norm[l].reshape(1, D_MODEL), ffn1_w_gate[l].astype(BF16), ffn1_w_up[l].astype(BF16),
                  ffn1_w_down[l].astype(BF16), final_g, final_norm=False)

        def w_cols(c0, c1):
            return w_in[l, :, c0:c1].astype(BF16)

        wab = jnp.pad(w_cols(c_z, c_ab), ((0, 0), (0, LANES - 2 * DN_HEADS)))
        qkv, z, ab, df = _proj(xs, mix_norm[l].reshape(1, D_MODEL), w_cols(0, c_qkv), w_cols(c_qkv, c_z), wab,
                               w_cols(c_ab, c_dfq), w_cols(c_dfq, c_df), conv_qkv[l], seq)

        o_a = _gdn(qkv, ab, z, _head_lanes(dn_a_log[l]), _head_lanes(dn_dt_bias[l]),
                   jnp.broadcast_to(jnp.tile(dn_out_norm[l].astype(F32), DN_HEADS)[None, :], (SUBLANES, DN_WIDTH)),
                   batch, seq)

        lam_rows = jnp.zeros((SUBLANES, LANES), F32).at[0:4, :DIFF_HEAD_DIM].set(
            jnp.stack([diff_lambda_q1[l], diff_lambda_k1[l], diff_lambda_q2[l], diff_lambda_k2[l]]).astype(F32))
        o_b = _attn(slopes, df, lam_rows, diff_subln[l].reshape(1, DIFF_V_DIM), batch, seq, _lambda_init(l))

        xs = _merge_ffn(xs, mix_norm[l].reshape(1, D_MODEL), o_a, o_b, w_branch_a[l].astype(BF16),
                        w_branch_b[l].astype(BF16), w_cols(c_df, c_df + 2 * D_MODEL), w_out[l].astype(BF16),
                        ffn2_norm[l].reshape(1, D_MODEL), ffn2_w_gate[l].astype(BF16), ffn2_w_up[l].astype(BF16),
                        ffn2_w_down[l].astype(BF16), final_g, final_norm=(l == depth - 1))

    return xs.reshape(batch, seq, D_MODEL)
```

```python
import functools
import math

import jax
import jax.numpy as jnp
import numpy as np
from jax import lax
from jax.experimental import pallas as pl
from jax.experimental.pallas import tpu as pltpu

F32 = jnp.float32
BF16 = jnp.bfloat16

D_MODEL = 1024
D_FF = 2816
RMS_EPS = 1e-6
SUBLN_EPS = 1e-5
L2_EPS = 1e-6
CONV_WIDTH = 4
DN_HEADS = 8
DN_HEAD_DIM = 64
DN_WIDTH = DN_HEADS * DN_HEAD_DIM
DN_CHUNK = 64
DN_PAIRS = DN_HEADS // 2
DIFF_HEADS = 4
DIFF_HEAD_DIM = 64
DIFF_V_DIM = 2 * DIFF_HEAD_DIM
DIFF_WIDTH = DIFF_HEADS * DIFF_V_DIM
LANES = 128
SUBLANES = 8
LOG2E = math.log2(math.e)

VMEM_LIMIT_BYTES = 56 * 1024 * 1024

FFN_ROWS = 512
FF_CHUNKS = ((0, 512), (512, 1024), (1024, 1536), (1536, 2048), (2048, 2560), (2560, 2816))
PROJ_ROWS = 512
GDN_ROWS = 256
GDN_PREP_CHUNKS = 2
ATTN_Q = 512
ATTN_K = 512
ATTN_ROW_GROUP = 512
ATTN_DIAG_GROUP = 128
MERGE_ROWS = 512
CAST_STEPS = 8
POS_LO = 16
SKIP_LOG2 = 160.0
NORM_SLACK = 1.01


def _rms(x, g, eps):
    return x * lax.rsqrt(jnp.mean(x * x, axis=-1, keepdims=True) + eps) * g


def _sigmoid(x):
    return 1.0 / (1.0 + jnp.exp(-x))


def _dot(a, b):
    return jnp.dot(a, b, preferred_element_type=F32)


def _dot_nt(a, b):
    return lax.dot_general(a, b, (((1,), (1,)), ((), ())), preferred_element_type=F32)


def _split_bf16(x, terms):
    parts = []
    rest = x
    for t in range(terms):
        part = rest.astype(BF16)
        parts.append(part)
        if t + 1 < terms:
            rest = rest - part.astype(F32)
    return parts


def _resident(shape):
    return pl.BlockSpec(shape, lambda *_: (0,) * len(shape), pipeline_mode=pl.Buffered(1))


def _cast_kernel(*refs):
    half = len(refs) // 2
    for src, dst in zip(refs[:half], refs[half:]):
        dst[...] = src[...].astype(BF16)


def _to_bf16(*arrays):
    specs = [pl.BlockSpec((a.shape[0] // CAST_STEPS, a.shape[1]), lambda i: (i, 0)) for a in arrays]
    return pl.pallas_call(
        _cast_kernel,
        grid=(CAST_STEPS,),
        in_specs=specs,
        out_specs=specs,
        out_shape=[jax.ShapeDtypeStruct(a.shape, BF16) for a in arrays],
        compiler_params=pltpu.CompilerParams(dimension_semantics=("arbitrary",),
                                             vmem_limit_bytes=VMEM_LIMIT_BYTES),
        name="cast_weights",
    )(*arrays)


def _split_w_in_kernel(w_ref, qkv_ref, z_ref, ab_ref, dfq_ref, dfkv_ref, gates_ref):
    c_z = 3 * DN_WIDTH
    c_ab = c_z + DN_WIDTH
    c_dfq = c_ab + 2 * DN_HEADS
    c_dfkv = c_dfq + DIFF_WIDTH
    c_gates = c_dfkv + 2 * DIFF_WIDTH
    w = w_ref[...]
    qkv_ref[...] = w[:, :c_z].astype(BF16)
    z_ref[...] = w[:, c_z:c_ab].astype(BF16)
    lane = lax.broadcasted_iota(jnp.int32, (w.shape[0], LANES), 1)
    ab_ref[...] = jnp.where(lane < 2 * DN_HEADS, w[:, c_ab:c_ab + LANES], 0.0).astype(BF16)
    dfq_ref[...] = w[:, c_dfq:c_dfkv].astype(BF16)
    dfkv_ref[...] = w[:, c_dfkv:c_gates].astype(BF16)
    gates_ref[...] = w[:, c_gates:c_gates + 2 * D_MODEL].astype(BF16)


def _split_w_in(w):
    rows = w.shape[0] // CAST_STEPS
    widths = (3 * DN_WIDTH, DN_WIDTH, LANES, DIFF_WIDTH, 2 * DIFF_WIDTH, 2 * D_MODEL)
    return pl.pallas_call(
        _split_w_in_kernel,
        grid=(CAST_STEPS,),
        in_specs=[pl.BlockSpec((rows, w.shape[1]), lambda i: (i, 0))],
        out_specs=[pl.BlockSpec((rows, c), lambda i: (i, 0)) for c in widths],
        out_shape=[jax.ShapeDtypeStruct((w.shape[0], c), BF16) for c in widths],
        compiler_params=pltpu.CompilerParams(dimension_semantics=("arbitrary",),
                                             vmem_limit_bytes=VMEM_LIMIT_BYTES),
        name="split_w_in",
    )(w)


def _ffn_half_step(x, g_ref, wg_ref, wu_ref, wd_ref, fg_ref, final_norm):
    h = _rms(x, g_ref[...], RMS_EPS).astype(BF16)
    acc = None
    for c0, c1 in FF_CHUNKS:
        gate = _dot(h, wg_ref[:, c0:c1])
        up = _dot(h, wu_ref[:, c0:c1])
        act = (gate * _sigmoid(gate) * up).astype(BF16)
        part = _dot(act, wd_ref[c0:c1, :])
        acc = part if acc is None else acc + part
    y = x + 0.5 * acc
    if final_norm:
        y = _rms(y, fg_ref[...], RMS_EPS)
    return y


def _ffn_kernel(x_ref, g_ref, wg_ref, wu_ref, wd_ref, fg_ref, o_ref, *, final_norm):
    o_ref[...] = _ffn_half_step(x_ref[...], g_ref, wg_ref, wu_ref, wd_ref, fg_ref, final_norm)


def _ffn(x, g, wg, wu, wd, fg, final_norm):
    n = x.shape[0]
    row = pl.BlockSpec((FFN_ROWS, D_MODEL), lambda i: (i, 0))
    return pl.pallas_call(
        functools.partial(_ffn_kernel, final_norm=final_norm),
        grid=(n // FFN_ROWS,),
        in_specs=[row, _resident((1, D_MODEL)), _resident((D_MODEL, D_FF)), _resident((D_MODEL, D_FF)),
                  _resident((D_FF, D_MODEL)), _resident((1, D_MODEL))],
        out_specs=row,
        out_shape=jax.ShapeDtypeStruct((n, D_MODEL), F32),
        compiler_params=pltpu.CompilerParams(dimension_semantics=("arbitrary",),
                                             vmem_limit_bytes=VMEM_LIMIT_BYTES),
        name="ffn",
    )(x, g, wg, wu, wd, fg)


def _proj_kernel(x_ref, g_ref, wqkv_ref, wz_ref, wab_ref, wdfq_ref, wdfkv_ref, cw_ref,
                 qkv_ref, z_ref, ab_ref, df_ref, ext_ref, *, tiles_per_seq):
    halo = SUBLANES

    @pl.when(pl.program_id(0) % tiles_per_seq == 0)
    def _():
        ext_ref[0:halo, :] = jnp.zeros((halo, 3 * DN_WIDTH), F32)

    h = _rms(x_ref[...], g_ref[...], RMS_EPS).astype(BF16)
    pre = _dot(h, wqkv_ref[...])
    ext_ref[halo:halo + PROJ_ROWS, :] = pre
    cw = cw_ref[...]
    y = cw[CONV_WIDTH - 1:CONV_WIDTH, :] * pre
    for j in range(CONV_WIDTH - 1):
        y = y + cw[j:j + 1, :] * ext_ref[pl.ds(halo - (CONV_WIDTH - 1) + j, PROJ_ROWS), :]
    ext_ref[0:halo, :] = ext_ref[PROJ_ROWS:PROJ_ROWS + halo, :]
    qkv_ref[...] = (y * _sigmoid(y)).astype(BF16)
    z_ref[...] = _dot(h, wz_ref[...]).astype(BF16)
    ab_ref[...] = _dot(h, wab_ref[...])
    df_ref[:, :DIFF_WIDTH] = (_dot(h, wdfq_ref[...]) * (DIFF_HEAD_DIM ** -0.5 * LOG2E)).astype(BF16)
    df_ref[:, DIFF_WIDTH:] = _dot(h, wdfkv_ref[...]).astype(BF16)


def _proj(x1, g, wqkv, wz, wab, wdfq, wdfkv, cw, seq):
    n = x1.shape[0]
    row = lambda w: pl.BlockSpec((PROJ_ROWS, w), lambda i: (i, 0))
    return pl.pallas_call(
        functools.partial(_proj_kernel, tiles_per_seq=seq // PROJ_ROWS),
        grid=(n // PROJ_ROWS,),
        in_specs=[row(D_MODEL), _resident((1, D_MODEL)), _resident(wqkv.shape), _resident(wz.shape),
                  _resident(wab.shape), _resident(wdfq.shape), _resident(wdfkv.shape), _resident(cw.shape)],
        out_specs=[row(3 * DN_WIDTH), row(DN_WIDTH), row(LANES), row(3 * DIFF_WIDTH)],
        out_shape=[jax.ShapeDtypeStruct((n, 3 * DN_WIDTH), BF16),
                   jax.ShapeDtypeStruct((n, DN_WIDTH), BF16),
                   jax.ShapeDtypeStruct((n, LANES), F32),
                   jax.ShapeDtypeStruct((n, 3 * DIFF_WIDTH), BF16)],
        scratch_shapes=[pltpu.VMEM((PROJ_ROWS + SUBLANES, 3 * DN_WIDTH), F32)],
        compiler_params=pltpu.CompilerParams(dimension_semantics=("arbitrary",),
                                             vmem_limit_bytes=VMEM_LIMIT_BYTES),
        name="proj",
    )(x1, g, wqkv, wz, wab, wdfq, wdfkv, cw)


def _gdn_kernel(q_ref, k_ref, v_ref, ab_ref, z_ref, alog_ref, dtb_ref, onorm_ref, o_ref, s_ref, *, batch):
    c = DN_CHUNK
    hd = DN_HEAD_DIM
    units = [(b, p) for b in range(batch) for p in range(DN_PAIRS)]

    @pl.when(pl.program_id(0) == 0)
    def _():
        s_ref[...] = jnp.zeros(s_ref.shape, F32)

    row = lax.broadcasted_iota(jnp.int32, (c, LANES), 0)
    lane = lax.broadcasted_iota(jnp.int32, (c, LANES), 1)
    col = lane & (hd - 1)
    even_bf = jnp.where(lane < hd, 1.0, 0.0).astype(BF16)
    odd_bf = jnp.where(lane < hd, 0.0, 1.0).astype(BF16)
    incl = col <= row
    strict = col < row
    same16 = (row >> 4) == (col >> 4)
    same32 = (row >> 5) == (col >> 5)
    eye = jnp.where(col == row, 1.0, 0.0).astype(F32)
    r2 = lax.broadcasted_iota(jnp.int32, (LANES, LANES), 0)
    l2 = lax.broadcasted_iota(jnp.int32, (LANES, LANES), 1)
    same_head = (r2 >> 6) == (l2 >> 6)
    head_ones = jnp.where(same_head, 1.0, 0.0).astype(BF16)
    head_ones2 = jnp.concatenate([head_ones, head_ones], axis=0)
    r3 = lax.broadcasted_iota(jnp.int32, (3 * LANES, LANES), 0) & (LANES - 1)
    l3 = lax.broadcasted_iota(jnp.int32, (3 * LANES, LANES), 1)
    rc = lax.broadcasted_iota(jnp.int32, (c, 3 * c), 0)
    cc = lax.broadcasted_iota(jnp.int32, (c, 3 * c), 1) & (c - 1)
    lower3 = jnp.where(cc <= rc, 1.0, 0.0).astype(BF16)
    ones3 = jnp.ones((c, 3 * c), BF16)

    def stack(x):
        xb = x.astype(BF16)
        return jnp.concatenate([xb * even_bf, xb * odd_bf], axis=0)

    def mm(a, b):
        return _dot(a.astype(BF16), b.astype(BF16))

    def mm_nt(a, b):
        return _dot_nt(a.astype(BF16), b.astype(BF16))

    def head_sums(x):
        return _dot(jnp.concatenate(_split_bf16(x, 2), axis=1), head_ones2)

    def exact_rows(lhs3, x):
        return _dot(lhs3, jnp.concatenate(_split_bf16(x, 3), axis=0))

    ab3 = [jnp.concatenate(_split_bf16(ab_ref[b], 3), axis=1) for b in range(batch)]
    q_t, k_t, v_t, g_t, beta_t = [], [], [], [], []
    for b, p in units:
        lanes = slice(p * LANES, (p + 1) * LANES)
        expand_a = jnp.where(r3 == 2 * p + (l3 >> 6), 1.0, 0.0).astype(BF16)
        expand_b = jnp.where(r3 == DN_HEADS + 2 * p + (l3 >> 6), 1.0, 0.0).astype(BF16)
        sp_in = _dot(ab3[b], expand_a) + dtb_ref[0:1, lanes]
        softplus = jnp.maximum(sp_in, 0.0) + jnp.log(1.0 + jnp.exp(-jnp.abs(sp_in)))
        g_t.append(-jnp.exp(alog_ref[0:1, lanes]) * softplus)
        beta_t.append(_sigmoid(_dot(ab3[b], expand_b)))
        q = q_ref[b, :, lanes].astype(F32)
        k = k_ref[b, :, lanes].astype(F32)
        q_t.append(q * lax.rsqrt(head_sums(q * q) + L2_EPS) * (hd ** -0.5))
        k_t.append(k * lax.rsqrt(head_sums(k * k) + L2_EPS))
        v_t.append(v_ref[b, :, lanes].astype(F32))

    states = [s_ref[i] for i in range(len(units))]
    outs = [[] for _ in units]
    for n0 in range(0, GDN_ROWS // c, GDN_PREP_CHUNKS):
        items = [(u, n) for n in range(n0, n0 + GDN_PREP_CHUNKS) for u in range(len(units))]

        def take(per_unit):
            return [per_unit[u][n * c:(n + 1) * c] for u, n in items]

        q = take(q_t)
        k = take(k_t)
        v = take(v_t)
        beta = take(beta_t)
        gc = [exact_rows(lower3, x) for x in take(g_t)]
        gc_row = [exact_rows(ones3, x * eye) for x in gc]
        decay = [jnp.where(incl, jnp.exp(jnp.where(incl, a - b_, 0.0)), 0.0) for a, b_ in zip(gc, gc_row)]
        exp_gc = [jnp.exp(x) for x in gc]
        kb = [a * b_ for a, b_ in zip(k, beta)]
        kq = [mm_nt(jnp.concatenate([a, b_], axis=0), stack(k_)) for a, b_, k_ in zip(kb, q, k)]
        a_mat = [jnp.where(strict, x[:c] * d, 0.0) for x, d in zip(kq, decay)]
        attn = [x[c:] * d for x, d in zip(kq, decay)]

        neg_d = [jnp.where(strict & same16, -x, 0.0) for x in a_mat]
        t_inv = [eye + x for x in neg_d]
        power = [mm(x, stack(x)) for x in neg_d]
        for _ in range(2):
            both = [mm(jnp.concatenate([pw, t], axis=0), stack(pw)) for pw, t in zip(power, t_inv)]
            power = [x[:c] for x in both]
            t_inv = [t + x[c:] for t, x in zip(t_inv, both)]
        t_inv = [t + mm(t, stack(pw)) for t, pw in zip(t_inv, power)]
        for block in (strict & same32 & ~same16, strict & ~same32):
            left = [mm(t, stack(jnp.where(block, x, 0.0))) for t, x in zip(t_inv, a_mat)]
            t_inv = [t - mm(x, stack(t)) for t, x in zip(t_inv, left)]

        uw = [mm(t, jnp.concatenate([stack(v_ * b_), stack(kb_ * e)], axis=1))
              for t, v_, b_, kb_, e in zip(t_inv, v, beta, kb, exp_gc)]
        q_dec = [a * e for a, e in zip(q, exp_gc)]
        gc_last = [x[c - 1:c, :] for x in gc]
        k_dec_t = [(k_ * jnp.exp(gl - g_)).T for k_, gl, g_ in zip(k, gc_last, gc)]

        for first in range(0, len(items), len(units)):
            sel = slice(first, first + len(units))
            ws = [mm(jnp.concatenate([x[:, LANES:], qd], axis=0), st)
                  for x, qd, st in zip(uw[sel], q_dec[sel], states)]
            v_new = [x[:, :LANES] - y[:c] for x, y in zip(uw[sel], ws)]
            o_n = [y[c:] + mm(a, stack(vn)) for y, a, vn in zip(ws, attn[sel], v_new)]
            for lst, x in zip(outs, o_n):
                lst.append(x)
            states = [st * jnp.exp(gl) + jnp.where(same_head, mm(kt, vn), 0.0)
                      for st, gl, kt, vn in zip(states, gc_last[sel], k_dec_t[sel], v_new)]

    for i, (b, p) in enumerate(units):
        lanes = slice(p * LANES, (p + 1) * LANES)
        s_ref[i] = states[i]
        o = jnp.concatenate(outs[i], axis=0)
        ms = head_sums(o * o) * (1.0 / hd)
        z = z_ref[b, :, lanes].astype(F32)
        o_ref[b, :, lanes] = (o * lax.rsqrt(ms + RMS_EPS) * onorm_ref[0:1, lanes] * (z * _sigmoid(z))).astype(BF16)


def _gdn(qkv, ab, z, alog_lanes, dtb_lanes, onorm_lanes, batch, seq):
    qkv3 = qkv.reshape(batch, seq, 3 * DN_WIDTH)

    def tile(width, col):
        return pl.BlockSpec((batch, GDN_ROWS, width), lambda t: (0, t, col))

    return pl.pallas_call(
        functools.partial(_gdn_kernel, batch=batch),
        grid=(seq // GDN_ROWS,),
        in_specs=[tile(DN_WIDTH, 0), tile(DN_WIDTH, 1), tile(DN_WIDTH, 2), tile(LANES, 0), tile(DN_WIDTH, 0),
                  _resident(alog_lanes.shape), _resident(dtb_lanes.shape), _resident(onorm_lanes.shape)],
        out_specs=tile(DN_WIDTH, 0),
        out_shape=jax.ShapeDtypeStruct((batch, seq, DN_WIDTH), BF16),
        scratch_shapes=[pltpu.VMEM((batch * DN_PAIRS, LANES, LANES), F32)],
        compiler_params=pltpu.CompilerParams(dimension_semantics=("arbitrary",),
                                             vmem_limit_bytes=VMEM_LIMIT_BYTES),
        name="gdn",
    )(qkv3, qkv3, qkv3, ab.reshape(batch, seq, LANES), z.reshape(batch, seq, DN_WIDTH),
      alog_lanes, dtb_lanes, onorm_lanes).reshape(batch * seq, DN_WIDTH)


def _attn_kernel(off_ref, coef_ref, q_ref, k_ref, v_ref, qcols_ref, kcols_ref, vcols_ref, lam_ref, subln_ref,
                 o_ref, qa_ref, m_ref, acc_ref, k2max_ref, *, lam_init):
    h = pl.program_id(1)
    i = pl.program_id(2)
    d = DIFF_HEAD_DIM
    tq = ATTN_Q
    tk = ATTN_K

    q = q_ref[...]
    first = lax.broadcasted_iota(jnp.int32, q.shape, 1) < d
    qcols = jnp.broadcast_to(qcols_ref[0, 0:1, :], (tq, LANES))
    qa_ref[0:tq, :] = jnp.concatenate([jnp.where(first, q, 0), qcols], axis=1)
    qa_ref[tq:2 * tq, :] = jnp.concatenate([jnp.where(first, 0, q), qcols], axis=1)
    m_ref[...] = jnp.full(m_ref.shape, -jnp.inf, F32)
    acc_ref[...] = jnp.zeros(acc_ref.shape, F32)
    kcols = kcols_ref[...]
    vcols = vcols_ref[...]

    def max_sq_norm(x):
        sq = x.astype(F32) * x.astype(F32)
        r = lax.broadcasted_iota(jnp.int32, (LANES, LANES), 0)
        c = lax.broadcasted_iota(jnp.int32, (LANES, LANES), 1)
        same_half = jnp.where((r < d) == (c < d), 1.0, 0.0).astype(BF16)
        half_sums = _dot(sq.astype(BF16), same_half)
        return jnp.max(jnp.max(half_sums, axis=0, keepdims=True), axis=1, keepdims=True)

    @pl.when(i == 0)
    def _():
        k2max_ref[...] = jnp.broadcast_to(max_sq_norm(k_ref[...]), k2max_ref.shape)

    qk_bound = jnp.sqrt(max_sq_norm(q) * k2max_ref[0:1, 0:1]) * NORM_SLACK
    gap = (SKIP_LOG2 + 2.0 * qk_bound) / coef_ref[h]
    first_block = jnp.floor(((i * tq).astype(F32) - gap) * (1.0 / tk))
    j0 = jnp.clip(first_block, 0.0, i.astype(F32)).astype(jnp.int32)[0, 0]

    def kv_steps(blocks):
        scores, values, offs = [], [], []
        for j, masked in blocks:
            start = pl.multiple_of(j * tk, tk)
            ka = jnp.concatenate([k_ref[pl.ds(start, tk), :], kcols], axis=1)
            rg = ATTN_DIAG_GROUP if masked else ATTN_ROW_GROUP
            groups = range(0, 2 * tq, rg)
            widths = [(r0 % tq) + rg if masked else tk for r0 in groups]
            scores.append([(r0, _dot_nt(qa_ref[r0:r0 + rg, :], ka[:w])) for r0, w in zip(groups, widths)])
            values.append(jnp.concatenate([v_ref[pl.ds(start, tk), :], vcols], axis=1))
            offs.append(off_ref[h] * j.astype(F32))
        for (_, masked), block_scores, va, off in zip(blocks, scores, values, offs):
            for r0, s in block_scores:
                rg, width = s.shape
                rows = slice(r0, r0 + rg)
                if masked:
                    qrow = lax.broadcasted_iota(jnp.int32, (rg, width), 0) + (r0 % tq)
                    kcol = lax.broadcasted_iota(jnp.int32, (rg, width), 1)
                    s = jnp.where(kcol <= qrow, s, -jnp.inf)
                m_old = m_ref[rows, :]
                m_blk = jnp.broadcast_to(jnp.max(s, axis=-1, keepdims=True), (rg, LANES)) + off
                m_new = jnp.maximum(m_old, m_blk)
                alpha = jnp.exp2(m_old - m_new)
                shift = m_new - off
                p = jnp.concatenate([jnp.exp2(s[:, c0:c0 + LANES] - shift) for c0 in range(0, width, LANES)],
                                    axis=1).astype(BF16)
                acc_ref[rows, :] = (jnp.concatenate([alpha, alpha], axis=1) * acc_ref[rows, :]
                                    + _dot(p, va[:width]))
                m_ref[rows, :] = m_new

    def body4(t, carry):
        kv_steps([(j0 + 4 * t + u, False) for u in range(4)])
        return carry

    def body(t, carry):
        kv_steps([(j0 + 2 * t, False), (j0 + 2 * t + 1, False)])
        return carry

    n_past = i - j0
    n_quads = n_past // 4
    lax.fori_loop(0, n_quads, body4, 0)
    lax.fori_loop(2 * n_quads, n_past // 2, body, 0)

    @pl.when(n_past % 2 == 1)
    def _():
        kv_steps([(i - 1, False), (i, True)])

    @pl.when(n_past % 2 == 0)
    def _():
        kv_steps([(i, True)])

    lam_rows = lam_ref[...]
    lam = (jnp.exp(jnp.sum(lam_rows[0:1] * lam_rows[1:2], axis=-1, keepdims=True))
           - jnp.exp(jnp.sum(lam_rows[2:3] * lam_rows[3:4], axis=-1, keepdims=True)) + lam_init)
    acc1 = acc_ref[0:tq, :]
    acc2 = acc_ref[tq:2 * tq, :]
    o = acc1[:, :LANES] / acc1[:, LANES:] - lam * (acc2[:, :LANES] / acc2[:, LANES:])
    o_ref[...] = (_rms(o, subln_ref[...], SUBLN_EPS) * (1.0 - lam_init)).astype(BF16)


def _attn(slopes, df, lam_rows, subln, batch, seq, lam_init):
    assert ATTN_Q == ATTN_K and ATTN_K <= 256 * POS_LO and ATTN_Q % ATTN_ROW_GROUP == 0
    nq = seq // ATTN_Q
    df3 = df.reshape(batch, seq, 3 * DIFF_WIDTH)
    coef = slopes * LOG2E
    c_parts = jnp.stack(_split_bf16(coef, 3), axis=1)
    qcols = jnp.zeros((DIFF_HEADS, SUBLANES, LANES), BF16).at[:, :, 0:6].set(
        jnp.broadcast_to(jnp.tile(c_parts, (1, 2))[:, None, :], (DIFF_HEADS, SUBLANES, 6)))
    pos = np.arange(ATTN_K)
    kcols_np = np.zeros((ATTN_K, LANES), np.float32)
    kcols_np[:, 0:3] = (pos // POS_LO * POS_LO)[:, None]
    kcols_np[:, 3:6] = (pos % POS_LO)[:, None]
    kcols = jnp.asarray(kcols_np, BF16)
    vcols = jnp.ones((ATTN_K, LANES), BF16)
    block_off = coef * ATTN_K

    seq_block = lambda col0: pl.BlockSpec((None, seq, LANES), lambda b, h, i: (b, 0, col0 + h))
    const = lambda shape: pl.BlockSpec(shape, lambda b, h, i: (0,) * len(shape))
    return pl.pallas_call(
        functools.partial(_attn_kernel, lam_init=lam_init),
        grid=(batch, DIFF_HEADS, nq),
        in_specs=[pl.BlockSpec(memory_space=pltpu.SMEM), pl.BlockSpec(memory_space=pltpu.SMEM),
                  pl.BlockSpec((None, ATTN_Q, LANES), lambda b, h, i: (b, i, h)),
                  seq_block(DIFF_HEADS), seq_block(2 * DIFF_HEADS),
                  pl.BlockSpec((1, SUBLANES, LANES), lambda b, h, i: (h, 0, 0)),
                  const((ATTN_K, LANES)), const((ATTN_K, LANES)),
                  const((SUBLANES, LANES)), const((1, DIFF_V_DIM))],
        out_specs=pl.BlockSpec((None, ATTN_Q, LANES), lambda b, h, i: (b, i, h)),
        out_shape=jax.ShapeDtypeStruct((batch, seq, DIFF_WIDTH), BF16),
        scratch_shapes=[pltpu.VMEM((2 * ATTN_Q, 2 * LANES), BF16),
                        pltpu.VMEM((2 * ATTN_Q, LANES), F32),
                        pltpu.VMEM((2 * ATTN_Q, 2 * LANES), F32),
                        pltpu.VMEM((SUBLANES, LANES), F32)],
        compiler_params=pltpu.CompilerParams(dimension_semantics=("arbitrary", "arbitrary", "arbitrary"),
                                             vmem_limit_bytes=VMEM_LIMIT_BYTES),
        name="diff_attn",
    )(block_off, coef, df3, df3, df3, qcols, kcols, vcols, lam_rows, subln).reshape(batch * seq, DIFF_WIDTH)


def _merge_ffn_kernel(x_ref, g_ref, oa_ref, ob_ref, wa_ref, wb_ref, wgate_ref, wout_ref,
                      fg_norm_ref, wg_ref, wu_ref, wd_ref, fg_ref, o_ref, *, final_norm):
    x = x_ref[...]
    h = _rms(x, g_ref[...], RMS_EPS).astype(BF16)
    gates = _sigmoid(_dot(h, wgate_ref[...]))
    y_a = _dot(oa_ref[...], wa_ref[...])
    y_b = _dot(ob_ref[...], wb_ref[...])
    mixed = (gates[:, :D_MODEL] * y_a + gates[:, D_MODEL:] * y_b).astype(BF16)
    x = x + _dot(mixed, wout_ref[...])
    o_ref[...] = _ffn_half_step(x, fg_norm_ref, wg_ref, wu_ref, wd_ref, fg_ref, final_norm)


def _merge_ffn(x1, g, o_a, o_b, wa, wb, wgate, wout, ffn_g, wg, wu, wd, fg, final_norm):
    n = x1.shape[0]
    row = lambda w: pl.BlockSpec((MERGE_ROWS, w), lambda i: (i, 0))
    return pl.pallas_call(
        functools.partial(_merge_ffn_kernel, final_norm=final_norm),
        grid=(n // MERGE_ROWS,),
        in_specs=[row(D_MODEL), _resident((1, D_MODEL)), row(DN_WIDTH), row(DIFF_WIDTH),
                  _resident(wa.shape), _resident(wb.shape), _resident(wgate.shape), _resident(wout.shape),
                  _resident((1, D_MODEL)), _resident(wg.shape), _resident(wu.shape), _resident(wd.shape),
                  _resident((1, D_MODEL))],
        out_specs=row(D_MODEL),
        out_shape=jax.ShapeDtypeStruct((n, D_MODEL), F32),
        compiler_params=pltpu.CompilerParams(dimension_semantics=("arbitrary",),
                                             vmem_limit_bytes=VMEM_LIMIT_BYTES),
        name="merge_ffn",
    )(x1, g, o_a, o_b, wa, wb, wgate, wout, ffn_g, wg, wu, wd, fg)


def _lambda_init(layer_idx):
    return 0.8 - 0.6 * math.exp(-0.3 * layer_idx)


def _head_lanes(per_head):
    lanes = jnp.repeat(per_head.astype(F32), DN_HEAD_DIM)
    return jnp.broadcast_to(lanes[None, :], (SUBLANES, DN_WIDTH))


def kernel(x, ffn1_norm, ffn1_w_gate, ffn1_w_up, ffn1_w_down, mix_norm, w_in, conv_qkv, dn_a_log, dn_dt_bias, dn_out_norm, diff_lambda_q1, diff_lambda_k1, diff_lambda_q2, diff_lambda_k2, diff_subln, w_branch_a, w_branch_b, w_out, ffn2_norm, ffn2_w_gate, ffn2_w_up, ffn2_w_down, final_norm):
    batch, seq, _ = x.shape
    depth = ffn1_norm.shape[0]
    n = batch * seq
    xs = x.reshape(n, D_MODEL)
    slopes = jnp.asarray([2.0 ** (-8.0 * (i + 1) / DIFF_HEADS) for i in range(DIFF_HEADS)], F32)
    final_g = final_norm.reshape(1, D_MODEL)

    for l in range(depth):
        (f1_gate, f1_up, f1_down, f2_gate, f2_up, f2_down, wa, wb, wout) = _to_bf16(
            ffn1_w_gate[l], ffn1_w_up[l], ffn1_w_down[l], ffn2_w_gate[l], ffn2_w_up[l], ffn2_w_down[l],
            w_branch_a[l], w_branch_b[l], w_out[l])
        wqkv, wz, wab, wdfq, wdfkv, wgates = _split_w_in(w_in[l])

        xs = _ffn(xs, ffn1_norm[l].reshape(1, D_MODEL), f1_gate, f1_up, f1_down, final_g, final_norm=False)

        qkv, z, ab, df = _proj(xs, mix_norm[l].reshape(1, D_MODEL), wqkv, wz, wab, wdfq, wdfkv, conv_qkv[l], seq)

        o_a = _gdn(qkv, ab, z, _head_lanes(dn_a_log[l]), _head_lanes(dn_dt_bias[l]),
                   jnp.broadcast_to(jnp.tile(dn_out_norm[l].astype(F32), DN_HEADS)[None, :], (SUBLANES, DN_WIDTH)),
                   batch, seq)

        lam_rows = jnp.zeros((SUBLANES, LANES), F32).at[0:4, :DIFF_HEAD_DIM].set(
            jnp.stack([diff_lambda_q1[l], diff_lambda_k1[l], diff_lambda_q2[l], diff_lambda_k2[l]]).astype(F32))
        o_b = _attn(slopes, df, lam_rows, diff_subln[l].reshape(1, DIFF_V_DIM), batch, seq, _lambda_init(l))

        xs = _merge_ffn(xs, mix_norm[l].reshape(1, D_MODEL), o_a, o_b, wa, wb, wgates, wout,
                        ffn2_norm[l].reshape(1, D_MODEL), f2_gate, f2_up, f2_down, final_g,
                        final_norm=(l == depth - 1))

    return xs.reshape(batch, seq, D_MODEL)
```

```python
import functools
import math

import jax
import jax.numpy as jnp
import numpy as np
from jax import lax
from jax.experimental import pallas as pl
from jax.experimental.pallas import tpu as pltpu

F32 = jnp.float32
BF16 = jnp.bfloat16

D_MODEL = 1024
D_FF = 2816
RMS_EPS = 1e-6
SUBLN_EPS = 1e-5
L2_EPS = 1e-6
CONV_WIDTH = 4
DN_HEADS = 8
DN_HEAD_DIM = 64
DN_WIDTH = DN_HEADS * DN_HEAD_DIM
DN_CHUNK = 64
DN_PAIRS = DN_HEADS // 2
DIFF_HEADS = 4
DIFF_HEAD_DIM = 64
DIFF_V_DIM = 2 * DIFF_HEAD_DIM
DIFF_WIDTH = DIFF_HEADS * DIFF_V_DIM
LANES = 128
SUBLANES = 8
LOG2E = math.log2(math.e)

VMEM_LIMIT_BYTES = 56 * 1024 * 1024

FFN_ROWS = 512
FF_CHUNKS = ((0, 512), (512, 1024), (1024, 1536), (1536, 2048), (2048, 2560), (2560, 2816))
PROJ_ROWS = 512
GDN_ROWS = 256
GDN_PREP_CHUNKS = 2
ATTN_Q = 512
ATTN_K = 512
ATTN_ROW_GROUP = 512
ATTN_DIAG_GROUP = 128
MERGE_ROWS = 512
CAST_STEPS = 8
POS_LO = 16
SKIP_LOG2 = 160.0
NORM_SLACK = 1.01


def _rms(x, g, eps):
    return x * lax.rsqrt(jnp.mean(x * x, axis=-1, keepdims=True) + eps) * g


def _sigmoid(x):
    return 1.0 / (1.0 + jnp.exp(-x))


def _dot(a, b):
    return jnp.dot(a, b, preferred_element_type=F32)


def _dot_nt(a, b):
    return lax.dot_general(a, b, (((1,), (1,)), ((), ())), preferred_element_type=F32)


def _split_bf16(x, terms):
    parts = []
    rest = x
    for t in range(terms):
        part = rest.astype(BF16)
        parts.append(part)
        if t + 1 < terms:
            rest = rest - part.astype(F32)
    return parts


def _resident(shape):
    return pl.BlockSpec(shape, lambda *_: (0,) * len(shape), pipeline_mode=pl.Buffered(1))


def _cast_kernel(*refs):
    half = len(refs) // 2
    for src, dst in zip(refs[:half], refs[half:]):
        dst[...] = src[...].astype(BF16)


def _to_bf16(*arrays):
    specs = [pl.BlockSpec((a.shape[0] // CAST_STEPS, a.shape[1]), lambda i: (i, 0)) for a in arrays]
    return pl.pallas_call(
        _cast_kernel,
        grid=(CAST_STEPS,),
        in_specs=specs,
        out_specs=specs,
        out_shape=[jax.ShapeDtypeStruct(a.shape, BF16) for a in arrays],
        compiler_params=pltpu.CompilerParams(dimension_semantics=("arbitrary",),
                                             vmem_limit_bytes=VMEM_LIMIT_BYTES),
        name="cast_weights",
    )(*arrays)


def _split_w_in_kernel(w_ref, qkv_ref, z_ref, ab_ref, dfq_ref, dfkv_ref, gates_ref):
    c_z = 3 * DN_WIDTH
    c_ab = c_z + DN_WIDTH
    c_dfq = c_ab + 2 * DN_HEADS
    c_dfkv = c_dfq + DIFF_WIDTH
    c_gates = c_dfkv + 2 * DIFF_WIDTH
    w = w_ref[...]
    qkv_ref[...] = w[:, :c_z].astype(BF16)
    z_ref[...] = w[:, c_z:c_ab].astype(BF16)
    lane = lax.broadcasted_iota(jnp.int32, (w.shape[0], LANES), 1)
    ab_ref[...] = jnp.where(lane < 2 * DN_HEADS, w[:, c_ab:c_ab + LANES], 0.0).astype(BF16)
    dfq_ref[...] = w[:, c_dfq:c_dfkv].astype(BF16)
    dfkv_ref[...] = w[:, c_dfkv:c_gates].astype(BF16)
    gates_ref[...] = w[:, c_gates:c_gates + 2 * D_MODEL].astype(BF16)


def _split_w_in(w, layer):
    rows = w.shape[1] // CAST_STEPS
    widths = (3 * DN_WIDTH, DN_WIDTH, LANES, DIFF_WIDTH, 2 * DIFF_WIDTH, 2 * D_MODEL)
    return pl.pallas_call(
        _split_w_in_kernel,
        grid=(CAST_STEPS,),
        in_specs=[pl.BlockSpec((None, rows, w.shape[2]), lambda i: (layer, i, 0))],
        out_specs=[pl.BlockSpec((rows, c), lambda i: (i, 0)) for c in widths],
        out_shape=[jax.ShapeDtypeStruct((w.shape[1], c), BF16) for c in widths],
        compiler_params=pltpu.CompilerParams(dimension_semantics=("arbitrary",),
                                             vmem_limit_bytes=VMEM_LIMIT_BYTES),
        name="split_w_in",
    )(w)


def _ffn_half_step(x, g_ref, wg_ref, wu_ref, wd_ref, fg_ref, final_norm):
    h = _rms(x, g_ref[...], RMS_EPS).astype(BF16)
    acc = None
    for c0, c1 in FF_CHUNKS:
        gate = _dot(h, wg_ref[:, c0:c1])
        up = _dot(h, wu_ref[:, c0:c1])
        act = (gate * _sigmoid(gate) * up).astype(BF16)
        part = _dot(act, wd_ref[c0:c1, :])
        acc = part if acc is None else acc + part
    y = x + 0.5 * acc
    if final_norm:
        y = _rms(y, fg_ref[...], RMS_EPS)
    return y


def _ffn_kernel(x_ref, g_ref, wg_ref, wu_ref, wd_ref, fg_ref, o_ref, *, final_norm):
    o_ref[...] = _ffn_half_step(x_ref[...], g_ref, wg_ref, wu_ref, wd_ref, fg_ref, final_norm)


def _ffn(x, g, wg, wu, wd, fg, final_norm):
    n = x.shape[0]
    row = pl.BlockSpec((FFN_ROWS, D_MODEL), lambda i: (i, 0))
    return pl.pallas_call(
        functools.partial(_ffn_kernel, final_norm=final_norm),
        grid=(n // FFN_ROWS,),
        in_specs=[row, _resident((1, D_MODEL)), _resident((D_MODEL, D_FF)), _resident((D_MODEL, D_FF)),
                  _resident((D_FF, D_MODEL)), _resident((1, D_MODEL))],
        out_specs=row,
        out_shape=jax.ShapeDtypeStruct((n, D_MODEL), F32),
        compiler_params=pltpu.CompilerParams(dimension_semantics=("arbitrary",),
                                             vmem_limit_bytes=VMEM_LIMIT_BYTES),
        name="ffn",
    )(x, g, wg, wu, wd, fg)


def _proj_kernel(x_ref, g_ref, wqkv_ref, wz_ref, wab_ref, wdfq_ref, wdfkv_ref, cw_ref,
                 qkv_ref, z_ref, ab_ref, df_ref, ext_ref, *, tiles_per_seq):
    halo = SUBLANES

    @pl.when(pl.program_id(0) % tiles_per_seq == 0)
    def _():
        ext_ref[0:halo, :] = jnp.zeros((halo, 3 * DN_WIDTH), F32)

    h = _rms(x_ref[...], g_ref[...], RMS_EPS).astype(BF16)
    pre = _dot(h, wqkv_ref[...])
    ext_ref[halo:halo + PROJ_ROWS, :] = pre
    cw = cw_ref[...]
    y = cw[CONV_WIDTH - 1:CONV_WIDTH, :] * pre
    for j in range(CONV_WIDTH - 1):
        y = y + cw[j:j + 1, :] * ext_ref[pl.ds(halo - (CONV_WIDTH - 1) + j, PROJ_ROWS), :]
    ext_ref[0:halo, :] = ext_ref[PROJ_ROWS:PROJ_ROWS + halo, :]
    qkv_ref[...] = (y * _sigmoid(y)).astype(BF16)
    z_ref[...] = _dot(h, wz_ref[...]).astype(BF16)
    ab_ref[...] = _dot(h, wab_ref[...])
    df_ref[:, :DIFF_WIDTH] = (_dot(h, wdfq_ref[...]) * (DIFF_HEAD_DIM ** -0.5 * LOG2E)).astype(BF16)
    df_ref[:, DIFF_WIDTH:] = _dot(h, wdfkv_ref[...]).astype(BF16)


def _proj(x1, g, wqkv, wz, wab, wdfq, wdfkv, cw, seq):
    n = x1.shape[0]
    row = lambda w: pl.BlockSpec((PROJ_ROWS, w), lambda i: (i, 0))
    return pl.pallas_call(
        functools.partial(_proj_kernel, tiles_per_seq=seq // PROJ_ROWS),
        grid=(n // PROJ_ROWS,),
        in_specs=[row(D_MODEL), _resident((1, D_MODEL)), _resident(wqkv.shape), _resident(wz.shape),
                  _resident(wab.shape), _resident(wdfq.shape), _resident(wdfkv.shape), _resident(cw.shape)],
        out_specs=[row(3 * DN_WIDTH), row(DN_WIDTH), row(LANES), row(3 * DIFF_WIDTH)],
        out_shape=[jax.ShapeDtypeStruct((n, 3 * DN_WIDTH), BF16),
                   jax.ShapeDtypeStruct((n, DN_WIDTH), BF16),
                   jax.ShapeDtypeStruct((n, LANES), F32),
                   jax.ShapeDtypeStruct((n, 3 * DIFF_WIDTH), BF16)],
        scratch_shapes=[pltpu.VMEM((PROJ_ROWS + SUBLANES, 3 * DN_WIDTH), F32)],
        compiler_params=pltpu.CompilerParams(dimension_semantics=("arbitrary",),
                                             vmem_limit_bytes=VMEM_LIMIT_BYTES),
        name="proj",
    )(x1, g, wqkv, wz, wab, wdfq, wdfkv, cw)


def _gdn_kernel(q_ref, k_ref, v_ref, ab_ref, z_ref, alog_ref, dtb_ref, onorm_ref, o_ref, s_ref, *, batch):
    c = DN_CHUNK
    hd = DN_HEAD_DIM
    units = [(b, p) for b in range(batch) for p in range(DN_PAIRS)]

    @pl.when(pl.program_id(0) == 0)
    def _():
        s_ref[...] = jnp.zeros(s_ref.shape, F32)

    row = lax.broadcasted_iota(jnp.int32, (c, LANES), 0)
    lane = lax.broadcasted_iota(jnp.int32, (c, LANES), 1)
    col = lane & (hd - 1)
    even_bf = jnp.where(lane < hd, 1.0, 0.0).astype(BF16)
    odd_bf = jnp.where(lane < hd, 0.0, 1.0).astype(BF16)
    incl = col <= row
    strict = col < row
    same16 = (row >> 4) == (col >> 4)
    same32 = (row >> 5) == (col >> 5)
    eye = jnp.where(col == row, 1.0, 0.0).astype(F32)
    r2 = lax.broadcasted_iota(jnp.int32, (LANES, LANES), 0)
    l2 = lax.broadcasted_iota(jnp.int32, (LANES, LANES), 1)
    same_head = (r2 >> 6) == (l2 >> 6)
    head_ones = jnp.where(same_head, 1.0, 0.0).astype(BF16)
    head_ones2 = jnp.concatenate([head_ones, head_ones], axis=0)
    r3 = lax.broadcasted_iota(jnp.int32, (3 * LANES, LANES), 0) & (LANES - 1)
    l3 = lax.broadcasted_iota(jnp.int32, (3 * LANES, LANES), 1)
    rc = lax.broadcasted_iota(jnp.int32, (c, 3 * c), 0)
    cc = lax.broadcasted_iota(jnp.int32, (c, 3 * c), 1) & (c - 1)
    lower3 = jnp.where(cc <= rc, 1.0, 0.0).astype(BF16)
    ones3 = jnp.ones((c, 3 * c), BF16)

    def stack(x):
        xb = x.astype(BF16)
        return jnp.concatenate([xb * even_bf, xb * odd_bf], axis=0)

    def mm(a, b):
        return _dot(a.astype(BF16), b.astype(BF16))

    def mm_nt(a, b):
        return _dot_nt(a.astype(BF16), b.astype(BF16))

    def head_sums(x):
        return _dot(jnp.concatenate(_split_bf16(x, 2), axis=1), head_ones2)

    def exact_rows(lhs3, x):
        return _dot(lhs3, jnp.concatenate(_split_bf16(x, 3), axis=0))

    ab3 = [jnp.concatenate(_split_bf16(ab_ref[b], 3), axis=1) for b in range(batch)]
    q_t, k_t, v_t, g_t, beta_t = [], [], [], [], []
    for b, p in units:
        lanes = slice(p * LANES, (p + 1) * LANES)
        expand_a = jnp.where(r3 == 2 * p + (l3 >> 6), 1.0, 0.0).astype(BF16)
        expand_b = jnp.where(r3 == DN_HEADS + 2 * p + (l3 >> 6), 1.0, 0.0).astype(BF16)
        sp_in = _dot(ab3[b], expand_a) + dtb_ref[0:1, lanes]
        softplus = jnp.maximum(sp_in, 0.0) + jnp.log(1.0 + jnp.exp(-jnp.abs(sp_in)))
        g_t.append(-jnp.exp(alog_ref[0:1, lanes]) * softplus)
        beta_t.append(_sigmoid(_dot(ab3[b], expand_b)))
        q = q_ref[b, :, lanes].astype(F32)
        k = k_ref[b, :, lanes].astype(F32)
        q_t.append(q * lax.rsqrt(head_sums(q * q) + L2_EPS) * (hd ** -0.5))
        k_t.append(k * lax.rsqrt(head_sums(k * k) + L2_EPS))
        v_t.append(v_ref[b, :, lanes].astype(F32))

    states = [s_ref[i] for i in range(len(units))]
    outs = [[] for _ in units]
    for n0 in range(0, GDN_ROWS // c, GDN_PREP_CHUNKS):
        items = [(u, n) for n in range(n0, n0 + GDN_PREP_CHUNKS) for u in range(len(units))]

        def take(per_unit):
            return [per_unit[u][n * c:(n + 1) * c] for u, n in items]

        q = take(q_t)
        k = take(k_t)
        v = take(v_t)
        beta = take(beta_t)
        gc = [exact_rows(lower3, x) for x in take(g_t)]
        gc_row = [exact_rows(ones3, x * eye) for x in gc]
        decay = [jnp.where(incl, jnp.exp(jnp.where(incl, a - b_, 0.0)), 0.0) for a, b_ in zip(gc, gc_row)]
        exp_gc = [jnp.exp(x) for x in gc]
        kb = [a * b_ for a, b_ in zip(k, beta)]
        kq = [mm_nt(jnp.concatenate([a, b_], axis=0), stack(k_)) for a, b_, k_ in zip(kb, q, k)]
        a_mat = [jnp.where(strict, x[:c] * d, 0.0) for x, d in zip(kq, decay)]
        attn = [x[c:] * d for x, d in zip(kq, decay)]

        neg_d = [jnp.where(strict & same16, -x, 0.0) for x in a_mat]
        t_inv = [eye + x for x in neg_d]
        power = [mm(x, stack(x)) for x in neg_d]
        for _ in range(2):
            both = [mm(jnp.concatenate([pw, t], axis=0), stack(pw)) for pw, t in zip(power, t_inv)]
            power = [x[:c] for x in both]
            t_inv = [t + x[c:] for t, x in zip(t_inv, both)]
        t_inv = [t + mm(t, stack(pw)) for t, pw in zip(t_inv, power)]
        for block in (strict & same32 & ~same16, strict & ~same32):
            left = [mm(t, stack(jnp.where(block, x, 0.0))) for t, x in zip(t_inv, a_mat)]
            t_inv = [t - mm(x, stack(t)) for t, x in zip(t_inv, left)]

        uw = [mm(t, jnp.concatenate([stack(v_ * b_), stack(kb_ * e)], axis=1))
              for t, v_, b_, kb_, e in zip(t_inv, v, beta, kb, exp_gc)]
        q_dec = [a * e for a, e in zip(q, exp_gc)]
        gc_last = [x[c - 1:c, :] for x in gc]
        k_dec_t = [(k_ * jnp.exp(gl - g_)).T for k_, gl, g_ in zip(k, gc_last, gc)]

        for first in range(0, len(items), len(units)):
            sel = slice(first, first + len(units))
            ws = [mm(jnp.concatenate([x[:, LANES:], qd], axis=0), st)
                  for x, qd, st in zip(uw[sel], q_dec[sel], states)]
            v_new = [x[:, :LANES] - y[:c] for x, y in zip(uw[sel], ws)]
            o_n = [y[c:] + mm(a, stack(vn)) for y, a, vn in zip(ws, attn[sel], v_new)]
            for lst, x in zip(outs, o_n):
                lst.append(x)
            states = [st * jnp.exp(gl) + jnp.where(same_head, mm(kt, vn), 0.0)
                      for st, gl, kt, vn in zip(states, gc_last[sel], k_dec_t[sel], v_new)]

    for i, (b, p) in enumerate(units):
        lanes = slice(p * LANES, (p + 1) * LANES)
        s_ref[i] = states[i]
        o = jnp.concatenate(outs[i], axis=0)
        ms = head_sums(o * o) * (1.0 / hd)
        z = z_ref[b, :, lanes].astype(F32)
        o_ref[b, :, lanes] = (o * lax.rsqrt(ms + RMS_EPS) * onorm_ref[0:1, lanes] * (z * _sigmoid(z))).astype(BF16)


def _gdn(qkv, ab, z, alog_lanes, dtb_lanes, onorm_lanes, batch, seq):
    qkv3 = qkv.reshape(batch, seq, 3 * DN_WIDTH)

    def tile(width, col):
        return pl.BlockSpec((batch, GDN_ROWS, width), lambda t: (0, t, col))

    return pl.pallas_call(
        functools.partial(_gdn_kernel, batch=batch),
        grid=(seq // GDN_ROWS,),
        in_specs=[tile(DN_WIDTH, 0), tile(DN_WIDTH, 1), tile(DN_WIDTH, 2), tile(LANES, 0), tile(DN_WIDTH, 0),
                  _resident(alog_lanes.shape), _resident(dtb_lanes.shape), _resident(onorm_lanes.shape)],
        out_specs=tile(DN_WIDTH, 0),
        out_shape=jax.ShapeDtypeStruct((batch, seq, DN_WIDTH), BF16),
        scratch_shapes=[pltpu.VMEM((batch * DN_PAIRS, LANES, LANES), F32)],
        compiler_params=pltpu.CompilerParams(dimension_semantics=("arbitrary",),
                                             vmem_limit_bytes=VMEM_LIMIT_BYTES),
        name="gdn",
    )(qkv3, qkv3, qkv3, ab.reshape(batch, seq, LANES), z.reshape(batch, seq, DN_WIDTH),
      alog_lanes, dtb_lanes, onorm_lanes).reshape(batch * seq, DN_WIDTH)


def _attn_kernel(off_ref, coef_ref, q_ref, k_ref, v_ref, qcols_ref, kcols_ref, vcols_ref, lam_ref, subln_ref,
                 o_ref, qa_ref, m_ref, acc_ref, k2max_ref, *, lam_init):
    h = pl.program_id(1)
    i = pl.program_id(2)
    d = DIFF_HEAD_DIM
    tq = ATTN_Q
    tk = ATTN_K

    q = q_ref[...]
    first = lax.broadcasted_iota(jnp.int32, q.shape, 1) < d
    qcols = jnp.broadcast_to(qcols_ref[0, 0:1, :], (tq, LANES))
    qa_ref[0:tq, :] = jnp.concatenate([jnp.where(first, q, 0), qcols], axis=1)
    qa_ref[tq:2 * tq, :] = jnp.concatenate([jnp.where(first, 0, q), qcols], axis=1)
    m_ref[...] = jnp.full(m_ref.shape, -jnp.inf, F32)
    acc_ref[...] = jnp.zeros(acc_ref.shape, F32)
    kcols = kcols_ref[...]
    vcols = vcols_ref[...]

    def max_sq_norm(x):
        sq = x.astype(F32) * x.astype(F32)
        r = lax.broadcasted_iota(jnp.int32, (LANES, LANES), 0)
        c = lax.broadcasted_iota(jnp.int32, (LANES, LANES), 1)
        same_half = jnp.where((r < d) == (c < d), 1.0, 0.0).astype(BF16)
        half_sums = _dot(sq.astype(BF16), same_half)
        return jnp.max(jnp.max(half_sums, axis=0, keepdims=True), axis=1, keepdims=True)

    @pl.when(i == 0)
    def _():
        k2max_ref[...] = jnp.broadcast_to(max_sq_norm(k_ref[...]), k2max_ref.shape)

    qk_bound = jnp.sqrt(max_sq_norm(q) * k2max_ref[0:1, 0:1]) * NORM_SLACK
    gap = (SKIP_LOG2 + 2.0 * qk_bound) / coef_ref[h]
    first_block = jnp.floor(((i * tq).astype(F32) - gap) * (1.0 / tk))
    j0 = jnp.clip(first_block, 0.0, i.astype(F32)).astype(jnp.int32)[0, 0]

    def kv_steps(blocks):
        scores, values, offs = [], [], []
        for j, masked in blocks:
            start = pl.multiple_of(j * tk, tk)
            ka = jnp.concatenate([k_ref[pl.ds(start, tk), :], kcols], axis=1)
            rg = ATTN_DIAG_GROUP if masked else ATTN_ROW_GROUP
            groups = range(0, 2 * tq, rg)
            widths = [(r0 % tq) + rg if masked else tk for r0 in groups]
            scores.append([(r0, _dot_nt(qa_ref[r0:r0 + rg, :], ka[:w])) for r0, w in zip(groups, widths)])
            values.append(jnp.concatenate([v_ref[pl.ds(start, tk), :], vcols], axis=1))
            offs.append(off_ref[h] * j.astype(F32))
        for (_, masked), block_scores, va, off in zip(blocks, scores, values, offs):
            for r0, s in block_scores:
                rg, width = s.shape
                rows = slice(r0, r0 + rg)
                if masked:
                    qrow = lax.broadcasted_iota(jnp.int32, (rg, width), 0) + (r0 % tq)
                    kcol = lax.broadcasted_iota(jnp.int32, (rg, width), 1)
                    s = jnp.where(kcol <= qrow, s, -jnp.inf)
                m_old = m_ref[rows, :]
                m_blk = jnp.broadcast_to(jnp.max(s, axis=-1, keepdims=True), (rg, LANES)) + off
                m_new = jnp.maximum(m_old, m_blk)
                alpha = jnp.exp2(m_old - m_new)
                shift = m_new - off
                p = jnp.concatenate([jnp.exp2(s[:, c0:c0 + LANES] - shift) for c0 in range(0, width, LANES)],
                                    axis=1).astype(BF16)
                acc_ref[rows, :] = (jnp.concatenate([alpha, alpha], axis=1) * acc_ref[rows, :]
                                    + _dot(p, va[:width]))
                m_ref[rows, :] = m_new

    def body4(t, carry):
        kv_steps([(j0 + 4 * t + u, False) for u in range(4)])
        return carry

    def body(t, carry):
        kv_steps([(j0 + 2 * t, False), (j0 + 2 * t + 1, False)])
        return carry

    kv_steps([(i, True)])
    n_past = i - j0
    n_quads = n_past // 4
    lax.fori_loop(0, n_quads, body4, 0)
    lax.fori_loop(2 * n_quads, n_past // 2, body, 0)

    @pl.when(n_past % 2 == 1)
    def _():
        kv_steps([(i - 1, False)])

    lam_rows = lam_ref[...]
    lam = (jnp.exp(jnp.sum(lam_rows[0:1] * lam_rows[1:2], axis=-1, keepdims=True))
           - jnp.exp(jnp.sum(lam_rows[2:3] * lam_rows[3:4], axis=-1, keepdims=True)) + lam_init)
    acc1 = acc_ref[0:tq, :]
    acc2 = acc_ref[tq:2 * tq, :]
    o = acc1[:, :LANES] / acc1[:, LANES:] - lam * (acc2[:, :LANES] / acc2[:, LANES:])
    o_ref[...] = (_rms(o, subln_ref[...], SUBLN_EPS) * (1.0 - lam_init)).astype(BF16)


def _attn(slopes, df, lam_rows, subln, batch, seq, lam_init):
    assert ATTN_Q == ATTN_K and ATTN_K <= 256 * POS_LO and ATTN_Q % ATTN_ROW_GROUP == 0
    nq = seq // ATTN_Q
    df3 = df.reshape(batch, seq, 3 * DIFF_WIDTH)
    coef = slopes * LOG2E
    c_parts = jnp.stack(_split_bf16(coef, 3), axis=1)
    qcols = jnp.zeros((DIFF_HEADS, SUBLANES, LANES), BF16).at[:, :, 0:6].set(
        jnp.broadcast_to(jnp.tile(c_parts, (1, 2))[:, None, :], (DIFF_HEADS, SUBLANES, 6)))
    pos = np.arange(ATTN_K)
    kcols_np = np.zeros((ATTN_K, LANES), np.float32)
    kcols_np[:, 0:3] = (pos // POS_LO * POS_LO)[:, None]
    kcols_np[:, 3:6] = (pos % POS_LO)[:, None]
    kcols = jnp.asarray(kcols_np, BF16)
    vcols = jnp.ones((ATTN_K, LANES), BF16)
    block_off = coef * ATTN_K

    seq_block = lambda col0: pl.BlockSpec((None, seq, LANES), lambda b, h, i: (b, 0, col0 + h))
    const = lambda shape: pl.BlockSpec(shape, lambda b, h, i: (0,) * len(shape))
    return pl.pallas_call(
        functools.partial(_attn_kernel, lam_init=lam_init),
        grid=(batch, DIFF_HEADS, nq),
        in_specs=[pl.BlockSpec(memory_space=pltpu.SMEM), pl.BlockSpec(memory_space=pltpu.SMEM),
                  pl.BlockSpec((None, ATTN_Q, LANES), lambda b, h, i: (b, i, h)),
                  seq_block(DIFF_HEADS), seq_block(2 * DIFF_HEADS),
                  pl.BlockSpec((1, SUBLANES, LANES), lambda b, h, i: (h, 0, 0)),
                  const((ATTN_K, LANES)), const((ATTN_K, LANES)),
                  const((SUBLANES, LANES)), const((1, DIFF_V_DIM))],
        out_specs=pl.BlockSpec((None, ATTN_Q, LANES), lambda b, h, i: (b, i, h)),
        out_shape=jax.ShapeDtypeStruct((batch, seq, DIFF_WIDTH), BF16),
        scratch_shapes=[pltpu.VMEM((2 * ATTN_Q, 2 * LANES), BF16),
                        pltpu.VMEM((2 * ATTN_Q, LANES), F32),
                        pltpu.VMEM((2 * ATTN_Q, 2 * LANES), F32),
                        pltpu.VMEM((SUBLANES, LANES), F32)],
        compiler_params=pltpu.CompilerParams(dimension_semantics=("arbitrary", "arbitrary", "arbitrary"),
                                             vmem_limit_bytes=VMEM_LIMIT_BYTES),
        name="diff_attn",
    )(block_off, coef, df3, df3, df3, qcols, kcols, vcols, lam_rows, subln).reshape(batch * seq, DIFF_WIDTH)


def _merge_ffn_kernel(x_ref, g_ref, oa_ref, ob_ref, wa_ref, wb_ref, wgate_ref, wout_ref,
                      fg_norm_ref, wg_ref, wu_ref, wd_ref, fg_ref, o_ref, *, final_norm):
    x = x_ref[...]
    h = _rms(x, g_ref[...], RMS_EPS).astype(BF16)
    gates = _sigmoid(_dot(h, wgate_ref[...]))
    y_a = _dot(oa_ref[...], wa_ref[...])
    y_b = _dot(ob_ref[...], wb_ref[...])
    mixed = (gates[:, :D_MODEL] * y_a + gates[:, D_MODEL:] * y_b).astype(BF16)
    x = x + _dot(mixed, wout_ref[...])
    o_ref[...] = _ffn_half_step(x, fg_norm_ref, wg_ref, wu_ref, wd_ref, fg_ref, final_norm)


def _merge_ffn(x1, g, o_a, o_b, wa, wb, wgate, wout, ffn_g, wg, wu, wd, fg, final_norm):
    n = x1.shape[0]
    row = lambda w: pl.BlockSpec((MERGE_ROWS, w), lambda i: (i, 0))
    return pl.pallas_call(
        functools.partial(_merge_ffn_kernel, final_norm=final_norm),
        grid=(n // MERGE_ROWS,),
        in_specs=[row(D_MODEL), _resident((1, D_MODEL)), row(DN_WIDTH), row(DIFF_WIDTH),
                  _resident(wa.shape), _resident(wb.shape), _resident(wgate.shape), _resident(wout.shape),
                  _resident((1, D_MODEL)), _resident(wg.shape), _resident(wu.shape), _resident(wd.shape),
                  _resident((1, D_MODEL))],
        out_specs=row(D_MODEL),
        out_shape=jax.ShapeDtypeStruct((n, D_MODEL), F32),
        compiler_params=pltpu.CompilerParams(dimension_semantics=("arbitrary",),
                                             vmem_limit_bytes=VMEM_LIMIT_BYTES),
        name="merge_ffn",
    )(x1, g, o_a, o_b, wa, wb, wgate, wout, ffn_g, wg, wu, wd, fg)


def _lambda_init(layer_idx):
    return 0.8 - 0.6 * math.exp(-0.3 * layer_idx)


def _head_lanes(per_head):
    lanes = jnp.repeat(per_head.astype(F32), DN_HEAD_DIM)
    return jnp.broadcast_to(lanes[None, :], (SUBLANES, DN_WIDTH))


def kernel(x, ffn1_norm, ffn1_w_gate, ffn1_w_up, ffn1_w_down, mix_norm, w_in, conv_qkv, dn_a_log, dn_dt_bias, dn_out_norm, diff_lambda_q1, diff_lambda_k1, diff_lambda_q2, diff_lambda_k2, diff_subln, w_branch_a, w_branch_b, w_out, ffn2_norm, ffn2_w_gate, ffn2_w_up, ffn2_w_down, final_norm):
    batch, seq, _ = x.shape
    depth = ffn1_norm.shape[0]
    n = batch * seq
    xs = x.reshape(n, D_MODEL)
    slopes = jnp.asarray([2.0 ** (-8.0 * (i + 1) / DIFF_HEADS) for i in range(DIFF_HEADS)], F32)
    final_g = final_norm.reshape(1, D_MODEL)

    for l in range(depth):
        (f1_gate, f1_up, f1_down, f2_gate, f2_up, f2_down, wa, wb, wout) = _to_bf16(
            ffn1_w_gate[l], ffn1_w_up[l], ffn1_w_down[l], ffn2_w_gate[l], ffn2_w_up[l], ffn2_w_down[l],
            w_branch_a[l], w_branch_b[l], w_out[l])
        wqkv, wz, wab, wdfq, wdfkv, wgates = _split_w_in(w_in, l)

        xs = _ffn(xs, ffn1_norm[l].reshape(1, D_MODEL), f1_gate, f1_up, f1_down, final_g, final_norm=False)

        qkv, z, ab, df = _proj(xs, mix_norm[l].reshape(1, D_MODEL), wqkv, wz, wab, wdfq, wdfkv, conv_qkv[l], seq)

        o_a = _gdn(qkv, ab, z, _head_lanes(dn_a_log[l]), _head_lanes(dn_dt_bias[l]),
                   jnp.broadcast_to(jnp.tile(dn_out_norm[l].astype(F32), DN_HEADS)[None, :], (SUBLANES, DN_WIDTH)),
                   batch, seq)

        lam_rows = jnp.zeros((SUBLANES, LANES), F32).at[0:4, :DIFF_HEAD_DIM].set(
            jnp.stack([diff_lambda_q1[l], diff_lambda_k1[l], diff_lambda_q2[l], diff_lambda_k2[l]]).astype(F32))
        o_b = _attn(slopes, df, lam_rows, diff_subln[l].reshape(1, DIFF_V_DIM), batch, seq, _lambda_init(l))

        xs = _merge_ffn(xs, mix_norm[l].reshape(1, D_MODEL), o_a, o_b, wa, wb, wgates, wout,
                        ffn2_norm[l].reshape(1, D_MODEL), f2_gate, f2_up, f2_down, final_g,
                        final_norm=(l == depth - 1))

    return xs.reshape(batch, seq, D_MODEL)
```

```python
import functools
import math

import jax
import jax.numpy as jnp
import numpy as np
from jax import lax
from jax.experimental import pallas as pl
from jax.experimental.pallas import tpu as pltpu

F32 = jnp.float32
BF16 = jnp.bfloat16

D_MODEL = 1024
D_FF = 2816
RMS_EPS = 1e-6
SUBLN_EPS = 1e-5
L2_EPS = 1e-6
CONV_WIDTH = 4
DN_HEADS = 8
DN_HEAD_DIM = 64
DN_WIDTH = DN_HEADS * DN_HEAD_DIM
DN_CHUNK = 64
DN_PAIRS = DN_HEADS // 2
DIFF_HEADS = 4
DIFF_HEAD_DIM = 64
DIFF_V_DIM = 2 * DIFF_HEAD_DIM
DIFF_WIDTH = DIFF_HEADS * DIFF_V_DIM
LANES = 128
SUBLANES = 8
LOG2E = math.log2(math.e)

VMEM_LIMIT_BYTES = 56 * 1024 * 1024

FFN_ROWS = 512
FF_CHUNKS = ((0, 512), (512, 1024), (1024, 1536), (1536, 2048), (2048, 2560), (2560, 2816))
PROJ_ROWS = 512
GDN_ROWS = 512
GDN_PREP_CHUNKS = 2
ATTN_Q = 512
ATTN_K = 512
ATTN_ROW_GROUP = 512
ATTN_DIAG_GROUP = 128
MERGE_ROWS = 512
CAST_STEPS = 8
POS_LO = 16
SKIP_LOG2 = 160.0
NORM_SLACK = 1.01


def _rms(x, g, eps):
    return x * lax.rsqrt(jnp.mean(x * x, axis=-1, keepdims=True) + eps) * g


def _sigmoid(x):
    return 1.0 / (1.0 + jnp.exp(-x))


def _dot(a, b):
    return jnp.dot(a, b, preferred_element_type=F32)


def _dot_nt(a, b):
    return lax.dot_general(a, b, (((1,), (1,)), ((), ())), preferred_element_type=F32)


def _split_bf16(x, terms):
    parts = []
    rest = x
    for t in range(terms):
        part = rest.astype(BF16)
        parts.append(part)
        if t + 1 < terms:
            rest = rest - part.astype(F32)
    return parts


def _resident(shape):
    return pl.BlockSpec(shape, lambda *_: (0,) * len(shape), pipeline_mode=pl.Buffered(1))


def _cast_kernel(*refs):
    half = len(refs) // 2
    for src, dst in zip(refs[:half], refs[half:]):
        dst[...] = src[...].astype(BF16)


def _to_bf16(*arrays):
    specs = [pl.BlockSpec((a.shape[0] // CAST_STEPS, a.shape[1]), lambda i: (i, 0)) for a in arrays]
    return pl.pallas_call(
        _cast_kernel,
        grid=(CAST_STEPS,),
        in_specs=specs,
        out_specs=specs,
        out_shape=[jax.ShapeDtypeStruct(a.shape, BF16) for a in arrays],
        compiler_params=pltpu.CompilerParams(dimension_semantics=("arbitrary",),
                                             vmem_limit_bytes=VMEM_LIMIT_BYTES),
        name="cast_weights",
    )(*arrays)


def _split_w_in_kernel(w_ref, qkv_ref, z_ref, ab_ref, dfq_ref, dfkv_ref, gates_ref):
    c_z = 3 * DN_WIDTH
    c_ab = c_z + DN_WIDTH
    c_dfq = c_ab + 2 * DN_HEADS
    c_dfkv = c_dfq + DIFF_WIDTH
    c_gates = c_dfkv + 2 * DIFF_WIDTH
    w = w_ref[...]
    qkv_ref[...] = w[:, :c_z].astype(BF16)
    z_ref[...] = w[:, c_z:c_ab].astype(BF16)
    lane = lax.broadcasted_iota(jnp.int32, (w.shape[0], LANES), 1)
    ab_ref[...] = jnp.where(lane < 2 * DN_HEADS, w[:, c_ab:c_ab + LANES], 0.0).astype(BF16)
    dfq_ref[...] = w[:, c_dfq:c_dfkv].astype(BF16)
    dfkv_ref[...] = w[:, c_dfkv:c_gates].astype(BF16)
    gates_ref[...] = w[:, c_gates:c_gates + 2 * D_MODEL].astype(BF16)


def _split_w_in(w, layer):
    rows = w.shape[1] // CAST_STEPS
    widths = (3 * DN_WIDTH, DN_WIDTH, LANES, DIFF_WIDTH, 2 * DIFF_WIDTH, 2 * D_MODEL)
    return pl.pallas_call(
        _split_w_in_kernel,
        grid=(CAST_STEPS,),
        in_specs=[pl.BlockSpec((None, rows, w.shape[2]), lambda i: (layer, i, 0))],
        out_specs=[pl.BlockSpec((rows, c), lambda i: (i, 0)) for c in widths],
        out_shape=[jax.ShapeDtypeStruct((w.shape[1], c), BF16) for c in widths],
        compiler_params=pltpu.CompilerParams(dimension_semantics=("arbitrary",),
                                             vmem_limit_bytes=VMEM_LIMIT_BYTES),
        name="split_w_in",
    )(w)


def _ffn_half_step(x, g_ref, wg_ref, wu_ref, wd_ref, fg_ref, final_norm):
    h = _rms(x, g_ref[...], RMS_EPS).astype(BF16)
    acc = None
    for c0, c1 in FF_CHUNKS:
        gate = _dot(h, wg_ref[:, c0:c1])
        up = _dot(h, wu_ref[:, c0:c1])
        act = (gate * _sigmoid(gate) * up).astype(BF16)
        part = _dot(act, wd_ref[c0:c1, :])
        acc = part if acc is None else acc + part
    y = x + 0.5 * acc
    if final_norm:
        y = _rms(y, fg_ref[...], RMS_EPS)
    return y


def _ffn_kernel(x_ref, g_ref, wg_ref, wu_ref, wd_ref, fg_ref, o_ref, *, final_norm):
    o_ref[...] = _ffn_half_step(x_ref[...], g_ref, wg_ref, wu_ref, wd_ref, fg_ref, final_norm)


def _ffn(x, g, wg, wu, wd, fg, final_norm):
    n = x.shape[0]
    row = pl.BlockSpec((FFN_ROWS, D_MODEL), lambda i: (i, 0))
    return pl.pallas_call(
        functools.partial(_ffn_kernel, final_norm=final_norm),
        grid=(n // FFN_ROWS,),
        in_specs=[row, _resident((1, D_MODEL)), _resident((D_MODEL, D_FF)), _resident((D_MODEL, D_FF)),
                  _resident((D_FF, D_MODEL)), _resident((1, D_MODEL))],
        out_specs=row,
        out_shape=jax.ShapeDtypeStruct((n, D_MODEL), F32),
        compiler_params=pltpu.CompilerParams(dimension_semantics=("arbitrary",),
                                             vmem_limit_bytes=VMEM_LIMIT_BYTES),
        name="ffn",
    )(x, g, wg, wu, wd, fg)


def _proj_kernel(x_ref, g_ref, wqkv_ref, wz_ref, wab_ref, wdfq_ref, wdfkv_ref, cw_ref,
                 qkv_ref, z_ref, ab_ref, df_ref, ext_ref, *, tiles_per_seq):
    halo = SUBLANES

    @pl.when(pl.program_id(0) % tiles_per_seq == 0)
    def _():
        ext_ref[0:halo, :] = jnp.zeros((halo, 3 * DN_WIDTH), F32)

    h = _rms(x_ref[...], g_ref[...], RMS_EPS).astype(BF16)
    pre = _dot(h, wqkv_ref[...])
    ext_ref[halo:halo + PROJ_ROWS, :] = pre
    cw = cw_ref[...]
    y = cw[CONV_WIDTH - 1:CONV_WIDTH, :] * pre
    for j in range(CONV_WIDTH - 1):
        y = y + cw[j:j + 1, :] * ext_ref[pl.ds(halo - (CONV_WIDTH - 1) + j, PROJ_ROWS), :]
    ext_ref[0:halo, :] = ext_ref[PROJ_ROWS:PROJ_ROWS + halo, :]
    qkv_ref[...] = (y * _sigmoid(y)).astype(BF16)
    z_ref[...] = _dot(h, wz_ref[...]).astype(BF16)
    ab_ref[...] = _dot(h, wab_ref[...])
    df_ref[:, :DIFF_WIDTH] = (_dot(h, wdfq_ref[...]) * (DIFF_HEAD_DIM ** -0.5 * LOG2E)).astype(BF16)
    df_ref[:, DIFF_WIDTH:] = _dot(h, wdfkv_ref[...]).astype(BF16)


def _proj(x1, g, wqkv, wz, wab, wdfq, wdfkv, cw, seq):
    n = x1.shape[0]
    row = lambda w: pl.BlockSpec((PROJ_ROWS, w), lambda i: (i, 0))
    return pl.pallas_call(
        functools.partial(_proj_kernel, tiles_per_seq=seq // PROJ_ROWS),
        grid=(n // PROJ_ROWS,),
        in_specs=[row(D_MODEL), _resident((1, D_MODEL)), _resident(wqkv.shape), _resident(wz.shape),
                  _resident(wab.shape), _resident(wdfq.shape), _resident(wdfkv.shape), _resident(cw.shape)],
        out_specs=[row(3 * DN_WIDTH), row(DN_WIDTH), row(LANES), row(3 * DIFF_WIDTH)],
        out_shape=[jax.ShapeDtypeStruct((n, 3 * DN_WIDTH), BF16),
                   jax.ShapeDtypeStruct((n, DN_WIDTH), BF16),
                   jax.ShapeDtypeStruct((n, LANES), F32),
                   jax.ShapeDtypeStruct((n, 3 * DIFF_WIDTH), BF16)],
        scratch_shapes=[pltpu.VMEM((PROJ_ROWS + SUBLANES, 3 * DN_WIDTH), F32)],
        compiler_params=pltpu.CompilerParams(dimension_semantics=("arbitrary",),
                                             vmem_limit_bytes=VMEM_LIMIT_BYTES),
        name="proj",
    )(x1, g, wqkv, wz, wab, wdfq, wdfkv, cw)


def _gdn_kernel(q_ref, k_ref, v_ref, ab_ref, z_ref, alog_ref, dtb_ref, onorm_ref, o_ref, s_ref, *, batch):
    c = DN_CHUNK
    hd = DN_HEAD_DIM
    units = [(b, p) for b in range(batch) for p in range(DN_PAIRS)]

    @pl.when(pl.program_id(0) == 0)
    def _():
        s_ref[...] = jnp.zeros(s_ref.shape, F32)

    row = lax.broadcasted_iota(jnp.int32, (c, LANES), 0)
    lane = lax.broadcasted_iota(jnp.int32, (c, LANES), 1)
    col = lane & (hd - 1)
    even_bf = jnp.where(lane < hd, 1.0, 0.0).astype(BF16)
    odd_bf = jnp.where(lane < hd, 0.0, 1.0).astype(BF16)
    incl = col <= row
    strict = col < row
    same16 = (row >> 4) == (col >> 4)
    same32 = (row >> 5) == (col >> 5)
    eye = jnp.where(col == row, 1.0, 0.0).astype(F32)
    r2 = lax.broadcasted_iota(jnp.int32, (LANES, LANES), 0)
    l2 = lax.broadcasted_iota(jnp.int32, (LANES, LANES), 1)
    same_head = (r2 >> 6) == (l2 >> 6)
    head_ones = jnp.where(same_head, 1.0, 0.0).astype(BF16)
    head_ones2 = jnp.concatenate([head_ones, head_ones], axis=0)
    r3 = lax.broadcasted_iota(jnp.int32, (3 * LANES, LANES), 0) & (LANES - 1)
    l3 = lax.broadcasted_iota(jnp.int32, (3 * LANES, LANES), 1)
    rc = lax.broadcasted_iota(jnp.int32, (c, 3 * c), 0)
    cc = lax.broadcasted_iota(jnp.int32, (c, 3 * c), 1) & (c - 1)
    lower3 = jnp.where(cc <= rc, 1.0, 0.0).astype(BF16)
    ones3 = jnp.ones((c, 3 * c), BF16)

    def stack(x):
        xb = x.astype(BF16)
        return jnp.concatenate([xb * even_bf, xb * odd_bf], axis=0)

    def mm(a, b):
        return _dot(a.astype(BF16), b.astype(BF16))

    def mm_nt(a, b):
        return _dot_nt(a.astype(BF16), b.astype(BF16))

    def head_sums(x):
        return _dot(jnp.concatenate(_split_bf16(x, 2), axis=1), head_ones2)

    def exact_rows(lhs3, x):
        return _dot(lhs3, jnp.concatenate(_split_bf16(x, 3), axis=0))

    ab3 = [jnp.concatenate(_split_bf16(ab_ref[b], 3), axis=1) for b in range(batch)]
    q_t, k_t, v_t, g_t, beta_t = [], [], [], [], []
    for b, p in units:
        lanes = slice(p * LANES, (p + 1) * LANES)
        expand_a = jnp.where(r3 == 2 * p + (l3 >> 6), 1.0, 0.0).astype(BF16)
        expand_b = jnp.where(r3 == DN_HEADS + 2 * p + (l3 >> 6), 1.0, 0.0).astype(BF16)
        sp_in = _dot(ab3[b], expand_a) + dtb_ref[0:1, lanes]
        softplus = jnp.maximum(sp_in, 0.0) + jnp.log(1.0 + jnp.exp(-jnp.abs(sp_in)))
        g_t.append(-jnp.exp(alog_ref[0:1, lanes]) * softplus)
        beta_t.append(_sigmoid(_dot(ab3[b], expand_b)))
        q = q_ref[b, :, lanes].astype(F32)
        k = k_ref[b, :, lanes].astype(F32)
        q_t.append(q * lax.rsqrt(head_sums(q * q) + L2_EPS) * (hd ** -0.5))
        k_t.append(k * lax.rsqrt(head_sums(k * k) + L2_EPS))
        v_t.append(v_ref[b, :, lanes].astype(F32))

    states = [s_ref[i] for i in range(len(units))]
    outs = [[] for _ in units]
    for n0 in range(0, GDN_ROWS // c, GDN_PREP_CHUNKS):
        items = [(u, n) for n in range(n0, n0 + GDN_PREP_CHUNKS) for u in range(len(units))]

        def take(per_unit):
            return [per_unit[u][n * c:(n + 1) * c] for u, n in items]

        q = take(q_t)
        k = take(k_t)
        v = take(v_t)
        beta = take(beta_t)
        gc = [exact_rows(lower3, x) for x in take(g_t)]
        gc_row = [exact_rows(ones3, x * eye) for x in gc]
        decay = [jnp.where(incl, jnp.exp(jnp.where(incl, a - b_, 0.0)), 0.0) for a, b_ in zip(gc, gc_row)]
        exp_gc = [jnp.exp(x) for x in gc]
        kb = [a * b_ for a, b_ in zip(k, beta)]
        kq = [mm_nt(jnp.concatenate([a, b_], axis=0), stack(k_)) for a, b_, k_ in zip(kb, q, k)]
        a_mat = [jnp.where(strict, x[:c] * d, 0.0) for x, d in zip(kq, decay)]
        attn = [x[c:] * d for x, d in zip(kq, decay)]

        neg_d = [jnp.where(strict & same16, -x, 0.0) for x in a_mat]
        t_inv = [eye + x for x in neg_d]
        power = [mm(x, stack(x)) for x in neg_d]
        for _ in range(2):
            both = [mm(jnp.concatenate([pw, t], axis=0), stack(pw)) for pw, t in zip(power, t_inv)]
            power = [x[:c] for x in both]
            t_inv = [t + x[c:] for t, x in zip(t_inv, both)]
        t_inv = [t + mm(t, stack(pw)) for t, pw in zip(t_inv, power)]
        for block in (strict & same32 & ~same16, strict & ~same32):
            left = [mm(t, stack(jnp.where(block, x, 0.0))) for t, x in zip(t_inv, a_mat)]
            t_inv = [t - mm(x, stack(t)) for t, x in zip(t_inv, left)]

        uw = [mm(t, jnp.concatenate([stack(v_ * b_), stack(kb_ * e)], axis=1))
              for t, v_, b_, kb_, e in zip(t_inv, v, beta, kb, exp_gc)]
        q_dec = [a * e for a, e in zip(q, exp_gc)]
        gc_last = [x[c - 1:c, :] for x in gc]
        k_dec_t = [(k_ * jnp.exp(gl - g_)).T for k_, gl, g_ in zip(k, gc_last, gc)]

        for first in range(0, len(items), len(units)):
            sel = slice(first, first + len(units))
            ws = [mm(jnp.concatenate([x[:, LANES:], qd], axis=0), st)
                  for x, qd, st in zip(uw[sel], q_dec[sel], states)]
            v_new = [x[:, :LANES] - y[:c] for x, y in zip(uw[sel], ws)]
            o_n = [y[c:] + mm(a, stack(vn)) for y, a, vn in zip(ws, attn[sel], v_new)]
            for lst, x in zip(outs, o_n):
                lst.append(x)
            states = [st * jnp.exp(gl) + jnp.where(same_head, mm(kt, vn), 0.0)
                      for st, gl, kt, vn in zip(states, gc_last[sel], k_dec_t[sel], v_new)]

    for i, (b, p) in enumerate(units):
        lanes = slice(p * LANES, (p + 1) * LANES)
        s_ref[i] = states[i]
        o = jnp.concatenate(outs[i], axis=0)
        ms = head_sums(o * o) * (1.0 / hd)
        z = z_ref[b, :, lanes].astype(F32)
        o_ref[b, :, lanes] = (o * lax.rsqrt(ms + RMS_EPS) * onorm_ref[0:1, lanes] * (z * _sigmoid(z))).astype(BF16)


def _gdn(qkv, ab, z, alog_lanes, dtb_lanes, onorm_lanes, batch, seq):
    qkv3 = qkv.reshape(batch, seq, 3 * DN_WIDTH)

    def tile(width, col):
        return pl.BlockSpec((batch, GDN_ROWS, width), lambda t: (0, t, col))

    return pl.pallas_call(
        functools.partial(_gdn_kernel, batch=batch),
        grid=(seq // GDN_ROWS,),
        in_specs=[tile(DN_WIDTH, 0), tile(DN_WIDTH, 1), tile(DN_WIDTH, 2), tile(LANES, 0), tile(DN_WIDTH, 0),
                  _resident(alog_lanes.shape), _resident(dtb_lanes.shape), _resident(onorm_lanes.shape)],
        out_specs=tile(DN_WIDTH, 0),
        out_shape=jax.ShapeDtypeStruct((batch, seq, DN_WIDTH), BF16),
        scratch_shapes=[pltpu.VMEM((batch * DN_PAIRS, LANES, LANES), F32)],
        compiler_params=pltpu.CompilerParams(dimension_semantics=("arbitrary",),
                                             vmem_limit_bytes=VMEM_LIMIT_BYTES),
        name="gdn",
    )(qkv3, qkv3, qkv3, ab.reshape(batch, seq, LANES), z.reshape(batch, seq, DN_WIDTH),
      alog_lanes, dtb_lanes, onorm_lanes).reshape(batch * seq, DN_WIDTH)


def _attn_kernel(off_ref, coef_ref, q_ref, k_ref, v_ref, qcols_ref, kcols_ref, vcols_ref, lam_ref, subln_ref,
                 o_ref, qa_ref, m_ref, acc_ref, k2max_ref, *, lam_init):
    h = pl.program_id(1)
    i = pl.program_id(2)
    d = DIFF_HEAD_DIM
    tq = ATTN_Q
    tk = ATTN_K

    q = q_ref[...]
    first = lax.broadcasted_iota(jnp.int32, q.shape, 1) < d
    qcols = jnp.broadcast_to(qcols_ref[0, 0:1, :], (tq, LANES))
    qa_ref[0:tq, :] = jnp.concatenate([jnp.where(first, q, 0), qcols], axis=1)
    qa_ref[tq:2 * tq, :] = jnp.concatenate([jnp.where(first, 0, q), qcols], axis=1)
    m_ref[...] = jnp.full(m_ref.shape, -jnp.inf, F32)
    acc_ref[...] = jnp.zeros(acc_ref.shape, F32)
    kcols = kcols_ref[...]
    vcols = vcols_ref[...]

    def max_sq_norm(x):
        sq = x.astype(F32) * x.astype(F32)
        r = lax.broadcasted_iota(jnp.int32, (LANES, LANES), 0)
        c = lax.broadcasted_iota(jnp.int32, (LANES, LANES), 1)
        same_half = jnp.where((r < d) == (c < d), 1.0, 0.0).astype(BF16)
        half_sums = _dot(sq.astype(BF16), same_half)
        return jnp.max(jnp.max(half_sums, axis=0, keepdims=True), axis=1, keepdims=True)

    @pl.when(i == 0)
    def _():
        k2max_ref[...] = jnp.broadcast_to(max_sq_norm(k_ref[...]), k2max_ref.shape)

    qk_bound = jnp.sqrt(max_sq_norm(q) * k2max_ref[0:1, 0:1]) * NORM_SLACK
    gap = (SKIP_LOG2 + 2.0 * qk_bound) / coef_ref[h]
    first_block = jnp.floor(((i * tq).astype(F32) - gap) * (1.0 / tk))
    j0 = jnp.clip(first_block, 0.0, i.astype(F32)).astype(jnp.int32)[0, 0]

    def kv_steps(blocks):
        scores, values, offs = [], [], []
        for j, masked in blocks:
            start = pl.multiple_of(j * tk, tk)
            ka = jnp.concatenate([k_ref[pl.ds(start, tk), :], kcols], axis=1)
            rg = ATTN_DIAG_GROUP if masked else ATTN_ROW_GROUP
            groups = range(0, 2 * tq, rg)
            widths = [(r0 % tq) + rg if masked else tk for r0 in groups]
            scores.append([(r0, _dot_nt(qa_ref[r0:r0 + rg, :], ka[:w])) for r0, w in zip(groups, widths)])
            values.append(jnp.concatenate([v_ref[pl.ds(start, tk), :], vcols], axis=1))
            offs.append(off_ref[h] * j.astype(F32))
        for (_, masked), block_scores, va, off in zip(blocks, scores, values, offs):
            for r0, s in block_scores:
                rg, width = s.shape
                rows = slice(r0, r0 + rg)
                if masked:
                    qrow = lax.broadcasted_iota(jnp.int32, (rg, width), 0) + (r0 % tq)
                    kcol = lax.broadcasted_iota(jnp.int32, (rg, width), 1)
                    s = jnp.where(kcol <= qrow, s, -jnp.inf)
                m_old = m_ref[rows, :]
                m_blk = jnp.broadcast_to(jnp.max(s, axis=-1, keepdims=True), (rg, LANES)) + off
                m_new = jnp.maximum(m_old, m_blk)
                alpha = jnp.exp2(m_old - m_new)
                shift = m_new - off
                p = jnp.concatenate([jnp.exp2((s[:, c0:c0 + LANES] - shift).astype(BF16))
                                     for c0 in range(0, width, LANES)], axis=1)
                acc_ref[rows, :] = (jnp.concatenate([alpha, alpha], axis=1) * acc_ref[rows, :]
                                    + _dot(p, va[:width]))
                m_ref[rows, :] = m_new

    def body4(t, carry):
        kv_steps([(j0 + 4 * t + u, False) for u in range(4)])
        return carry

    def body(t, carry):
        kv_steps([(j0 + 2 * t, False), (j0 + 2 * t + 1, False)])
        return carry

    kv_steps([(i, True)])
    n_past = i - j0
    n_quads = n_past // 4
    lax.fori_loop(0, n_quads, body4, 0)
    lax.fori_loop(2 * n_quads, n_past // 2, body, 0)

    @pl.when(n_past % 2 == 1)
    def _():
        kv_steps([(i - 1, False)])

    lam_rows = lam_ref[...]
    lam = (jnp.exp(jnp.sum(lam_rows[0:1] * lam_rows[1:2], axis=-1, keepdims=True))
           - jnp.exp(jnp.sum(lam_rows[2:3] * lam_rows[3:4], axis=-1, keepdims=True)) + lam_init)
    acc1 = acc_ref[0:tq, :]
    acc2 = acc_ref[tq:2 * tq, :]
    o = acc1[:, :LANES] / acc1[:, LANES:] - lam * (acc2[:, :LANES] / acc2[:, LANES:])
    o_ref[...] = (_rms(o, subln_ref[...], SUBLN_EPS) * (1.0 - lam_init)).astype(BF16)


def _attn(slopes, df, lam_rows, subln, batch, seq, lam_init):
    assert ATTN_Q == ATTN_K and ATTN_K <= 256 * POS_LO and ATTN_Q % ATTN_ROW_GROUP == 0
    nq = seq // ATTN_Q
    df3 = df.reshape(batch, seq, 3 * DIFF_WIDTH)
    coef = slopes * LOG2E
    c_parts = jnp.stack(_split_bf16(coef, 3), axis=1)
    qcols = jnp.zeros((DIFF_HEADS, SUBLANES, LANES), BF16).at[:, :, 0:6].set(
        jnp.broadcast_to(jnp.tile(c_parts, (1, 2))[:, None, :], (DIFF_HEADS, SUBLANES, 6)))
    pos = np.arange(ATTN_K)
    kcols_np = np.zeros((ATTN_K, LANES), np.float32)
    kcols_np[:, 0:3] = (pos // POS_LO * POS_LO)[:, None]
    kcols_np[:, 3:6] = (pos % POS_LO)[:, None]
    kcols = jnp.asarray(kcols_np, BF16)
    vcols = jnp.ones((ATTN_K, LANES), BF16)
    block_off = coef * ATTN_K

    seq_block = lambda col0: pl.BlockSpec((None, seq, LANES), lambda b, h, i: (b, 0, col0 + h))
    const = lambda shape: pl.BlockSpec(shape, lambda b, h, i: (0,) * len(shape))
    return pl.pallas_call(
        functools.partial(_attn_kernel, lam_init=lam_init),
        grid=(batch, DIFF_HEADS, nq),
        in_specs=[pl.BlockSpec(memory_space=pltpu.SMEM), pl.BlockSpec(memory_space=pltpu.SMEM),
                  pl.BlockSpec((None, ATTN_Q, LANES), lambda b, h, i: (b, i, h)),
                  seq_block(DIFF_HEADS), seq_block(2 * DIFF_HEADS),
                  pl.BlockSpec((1, SUBLANES, LANES), lambda b, h, i: (h, 0, 0)),
                  const((ATTN_K, LANES)), const((ATTN_K, LANES)),
                  const((SUBLANES, LANES)), const((1, DIFF_V_DIM))],
        out_specs=pl.BlockSpec((None, ATTN_Q, LANES), lambda b, h, i: (b, i, h)),
        out_shape=jax.ShapeDtypeStruct((batch, seq, DIFF_WIDTH), BF16),
        scratch_shapes=[pltpu.VMEM((2 * ATTN_Q, 2 * LANES), BF16),
                        pltpu.VMEM((2 * ATTN_Q, LANES), F32),
                        pltpu.VMEM((2 * ATTN_Q, 2 * LANES), F32),
                        pltpu.VMEM((SUBLANES, LANES), F32)],
        compiler_params=pltpu.CompilerParams(dimension_semantics=("arbitrary", "arbitrary", "arbitrary"),
                                             vmem_limit_bytes=VMEM_LIMIT_BYTES),
        name="diff_attn",
    )(block_off, coef, df3, df3, df3, qcols, kcols, vcols, lam_rows, subln).reshape(batch * seq, DIFF_WIDTH)


def _merge_ffn_kernel(x_ref, g_ref, oa_ref, ob_ref, wa_ref, wb_ref, wgate_ref, wout_ref,
                      fg_norm_ref, wg_ref, wu_ref, wd_ref, fg_ref, o_ref, *, final_norm):
    x = x_ref[...]
    h = _rms(x, g_ref[...], RMS_EPS).astype(BF16)
    gates = _sigmoid(_dot(h, wgate_ref[...]))
    y_a = _dot(oa_ref[...], wa_ref[...])
    y_b = _dot(ob_ref[...], wb_ref[...])
    mixed = (gates[:, :D_MODEL] * y_a + gates[:, D_MODEL:] * y_b).astype(BF16)
    x = x + _dot(mixed, wout_ref[...])
    o_ref[...] = _ffn_half_step(x, fg_norm_ref, wg_ref, wu_ref, wd_ref, fg_ref, final_norm)


def _merge_ffn(x1, g, o_a, o_b, wa, wb, wgate, wout, ffn_g, wg, wu, wd, fg, final_norm):
    n = x1.shape[0]
    row = lambda w: pl.BlockSpec((MERGE_ROWS, w), lambda i: (i, 0))
    return pl.pallas_call(
        functools.partial(_merge_ffn_kernel, final_norm=final_norm),
        grid=(n // MERGE_ROWS,),
        in_specs=[row(D_MODEL), _resident((1, D_MODEL)), row(DN_WIDTH), row(DIFF_WIDTH),
                  _resident(wa.shape), _resident(wb.shape), _resident(wgate.shape), _resident(wout.shape),
                  _resident((1, D_MODEL)), _resident(wg.shape), _resident(wu.shape), _resident(wd.shape),
                  _resident((1, D_MODEL))],
        out_specs=row(D_MODEL),
        out_shape=jax.ShapeDtypeStruct((n, D_MODEL), F32),
        compiler_params=pltpu.CompilerParams(dimension_semantics=("arbitrary",),
                                             vmem_limit_bytes=VMEM_LIMIT_BYTES),
        name="merge_ffn",
    )(x1, g, o_a, o_b, wa, wb, wgate, wout, ffn_g, wg, wu, wd, fg)


def _lambda_init(layer_idx):
    return 0.8 - 0.6 * math.exp(-0.3 * layer_idx)


def _head_lanes(per_head):
    lanes = jnp.repeat(per_head.astype(F32), DN_HEAD_DIM)
    return jnp.broadcast_to(lanes[None, :], (SUBLANES, DN_WIDTH))


def kernel(x, ffn1_norm, ffn1_w_gate, ffn1_w_up, ffn1_w_down, mix_norm, w_in, conv_qkv, dn_a_log, dn_dt_bias, dn_out_norm, diff_lambda_q1, diff_lambda_k1, diff_lambda_q2, diff_lambda_k2, diff_subln, w_branch_a, w_branch_b, w_out, ffn2_norm, ffn2_w_gate, ffn2_w_up, ffn2_w_down, final_norm):
    batch, seq, _ = x.shape
    depth = ffn1_norm.shape[0]
    n = batch * seq
    xs = x.reshape(n, D_MODEL)
    slopes = jnp.asarray([2.0 ** (-8.0 * (i + 1) / DIFF_HEADS) for i in range(DIFF_HEADS)], F32)
    final_g = final_norm.reshape(1, D_MODEL)

    for l in range(depth):
        (f1_gate, f1_up, f1_down, f2_gate, f2_up, f2_down, wa, wb, wout) = _to_bf16(
            ffn1_w_gate[l], ffn1_w_up[l], ffn1_w_down[l], ffn2_w_gate[l], ffn2_w_up[l], ffn2_w_down[l],
            w_branch_a[l], w_branch_b[l], w_out[l])
        wqkv, wz, wab, wdfq, wdfkv, wgates = _split_w_in(w_in, l)

        xs = _ffn(xs, ffn1_norm[l].reshape(1, D_MODEL), f1_gate, f1_up, f1_down, final_g, final_norm=False)

        qkv, z, ab, df = _proj(xs, mix_norm[l].reshape(1, D_MODEL), wqkv, wz, wab, wdfq, wdfkv, conv_qkv[l], seq)

        o_a = _gdn(qkv, ab, z, _head_lanes(dn_a_log[l]), _head_lanes(dn_dt_bias[l]),
                   jnp.broadcast_to(jnp.tile(dn_out_norm[l].astype(F32), DN_HEADS)[None, :], (SUBLANES, DN_WIDTH)),
                   batch, seq)

        lam_rows = jnp.zeros((SUBLANES, LANES), F32).at[0:4, :DIFF_HEAD_DIM].set(
            jnp.stack([diff_lambda_q1[l], diff_lambda_k1[l], diff_lambda_q2[l], diff_lambda_k2[l]]).astype(F32))
        o_b = _attn(slopes, df, lam_rows, diff_subln[l].reshape(1, DIFF_V_DIM), batch, seq, _lambda_init(l))

        xs = _merge_ffn(xs, mix_norm[l].reshape(1, D_MODEL), o_a, o_b, wa, wb, wgates, wout,
                        ffn2_norm[l].reshape(1, D_MODEL), f2_gate, f2_up, f2_down, final_g,
                        final_norm=(l == depth - 1))

    return xs.reshape(batch, seq, D_MODEL)
```

```python
import functools
import math

import jax
import jax.numpy as jnp
import numpy as np
from jax import lax
from jax.experimental import pallas as pl
from jax.experimental.pallas import tpu as pltpu

F32 = jnp.float32
BF16 = jnp.bfloat16

D_MODEL = 1024
D_FF = 2816
RMS_EPS = 1e-6
SUBLN_EPS = 1e-5
L2_EPS = 1e-6
CONV_WIDTH = 4
DN_HEADS = 8
DN_HEAD_DIM = 64
DN_WIDTH = DN_HEADS * DN_HEAD_DIM
DN_CHUNK = 64
DN_PAIRS = DN_HEADS // 2
DIFF_HEADS = 4
DIFF_HEAD_DIM = 64
DIFF_V_DIM = 2 * DIFF_HEAD_DIM
DIFF_WIDTH = DIFF_HEADS * DIFF_V_DIM
LANES = 128
SUBLANES = 8
LOG2E = math.log2(math.e)

VMEM_LIMIT_BYTES = 56 * 1024 * 1024

FFN_ROWS = 512
FF_CHUNKS = ((0, 512), (512, 1024), (1024, 1536), (1536, 2048), (2048, 2560), (2560, 2816))
PROJ_ROWS = 512
GDN_ROWS = 512
GDN_PREP_CHUNKS = 2
ATTN_Q = 512
ATTN_K = 512
ATTN_SPAN = 2
ATTN_ROW_GROUP = 512
ATTN_DIAG_GROUP = 128
MERGE_ROWS = 512
CAST_STEPS = 8
POS_LO = 16
SKIP_LOG2 = 160.0
NORM_SLACK = 1.01


def _rms(x, g, eps):
    return x * lax.rsqrt(jnp.mean(x * x, axis=-1, keepdims=True) + eps) * g


def _sigmoid(x):
    return 1.0 / (1.0 + jnp.exp(-x))


def _dot(a, b):
    return jnp.dot(a, b, preferred_element_type=F32)


def _dot_nt(a, b):
    return lax.dot_general(a, b, (((1,), (1,)), ((), ())), preferred_element_type=F32)


def _split_bf16(x, terms):
    parts = []
    rest = x
    for t in range(terms):
        part = rest.astype(BF16)
        parts.append(part)
        if t + 1 < terms:
            rest = rest - part.astype(F32)
    return parts


def _resident(shape):
    return pl.BlockSpec(shape, lambda *_: (0,) * len(shape), pipeline_mode=pl.Buffered(1))


def _cast_kernel(*refs):
    half = len(refs) // 2
    for src, dst in zip(refs[:half], refs[half:]):
        dst[...] = src[...].astype(BF16)


def _to_bf16(*arrays):
    specs = [pl.BlockSpec((a.shape[0] // CAST_STEPS, a.shape[1]), lambda i: (i, 0)) for a in arrays]
    return pl.pallas_call(
        _cast_kernel,
        grid=(CAST_STEPS,),
        in_specs=specs,
        out_specs=specs,
        out_shape=[jax.ShapeDtypeStruct(a.shape, BF16) for a in arrays],
        compiler_params=pltpu.CompilerParams(dimension_semantics=("arbitrary",),
                                             vmem_limit_bytes=VMEM_LIMIT_BYTES),
        name="cast_weights",
    )(*arrays)


def _split_w_in_kernel(w_ref, qkv_ref, z_ref, ab_ref, dfq_ref, dfkv_ref, gates_ref):
    c_z = 3 * DN_WIDTH
    c_ab = c_z + DN_WIDTH
    c_dfq = c_ab + 2 * DN_HEADS
    c_dfkv = c_dfq + DIFF_WIDTH
    c_gates = c_dfkv + 2 * DIFF_WIDTH
    w = w_ref[...]
    qkv_ref[...] = w[:, :c_z].astype(BF16)
    z_ref[...] = w[:, c_z:c_ab].astype(BF16)
    lane = lax.broadcasted_iota(jnp.int32, (w.shape[0], LANES), 1)
    ab_ref[...] = jnp.where(lane < 2 * DN_HEADS, w[:, c_ab:c_ab + LANES], 0.0).astype(BF16)
    dfq_ref[...] = w[:, c_dfq:c_dfkv].astype(BF16)
    dfkv_ref[...] = w[:, c_dfkv:c_gates].astype(BF16)
    gates_ref[...] = w[:, c_gates:c_gates + 2 * D_MODEL].astype(BF16)


def _split_w_in(w, layer):
    rows = w.shape[1] // CAST_STEPS
    widths = (3 * DN_WIDTH, DN_WIDTH, LANES, DIFF_WIDTH, 2 * DIFF_WIDTH, 2 * D_MODEL)
    return pl.pallas_call(
        _split_w_in_kernel,
        grid=(CAST_STEPS,),
        in_specs=[pl.BlockSpec((None, rows, w.shape[2]), lambda i: (layer, i, 0))],
        out_specs=[pl.BlockSpec((rows, c), lambda i: (i, 0)) for c in widths],
        out_shape=[jax.ShapeDtypeStruct((w.shape[1], c), BF16) for c in widths],
        compiler_params=pltpu.CompilerParams(dimension_semantics=("arbitrary",),
                                             vmem_limit_bytes=VMEM_LIMIT_BYTES),
        name="split_w_in",
    )(w)


def _ffn_half_step(x, g_ref, wg_ref, wu_ref, wd_ref, fg_ref, final_norm):
    h = _rms(x, g_ref[...], RMS_EPS).astype(BF16)
    acc = None
    for c0, c1 in FF_CHUNKS:
        gate = _dot(h, wg_ref[:, c0:c1])
        up = _dot(h, wu_ref[:, c0:c1])
        act = (gate * _sigmoid(gate) * up).astype(BF16)
        part = _dot(act, wd_ref[c0:c1, :])
        acc = part if acc is None else acc + part
    y = x + 0.5 * acc
    if final_norm:
        y = _rms(y, fg_ref[...], RMS_EPS)
    return y


def _ffn_kernel(x_ref, g_ref, wg_ref, wu_ref, wd_ref, fg_ref, o_ref, *, final_norm):
    o_ref[...] = _ffn_half_step(x_ref[...], g_ref, wg_ref, wu_ref, wd_ref, fg_ref, final_norm)


def _ffn(x, g, wg, wu, wd, fg, final_norm):
    n = x.shape[0]
    row = pl.BlockSpec((FFN_ROWS, D_MODEL), lambda i: (i, 0))
    return pl.pallas_call(
        functools.partial(_ffn_kernel, final_norm=final_norm),
        grid=(n // FFN_ROWS,),
        in_specs=[row, _resident((1, D_MODEL)), _resident((D_MODEL, D_FF)), _resident((D_MODEL, D_FF)),
                  _resident((D_FF, D_MODEL)), _resident((1, D_MODEL))],
        out_specs=row,
        out_shape=jax.ShapeDtypeStruct((n, D_MODEL), F32),
        compiler_params=pltpu.CompilerParams(dimension_semantics=("arbitrary",),
                                             vmem_limit_bytes=VMEM_LIMIT_BYTES),
        name="ffn",
    )(x, g, wg, wu, wd, fg)


def _proj_kernel(x_ref, g_ref, wqkv_ref, wz_ref, wab_ref, wdfq_ref, wdfkv_ref, cw_ref,
                 qkv_ref, z_ref, ab_ref, df_ref, ext_ref, *, tiles_per_seq):
    halo = SUBLANES

    @pl.when(pl.program_id(0) % tiles_per_seq == 0)
    def _():
        ext_ref[0:halo, :] = jnp.zeros((halo, 3 * DN_WIDTH), F32)

    h = _rms(x_ref[...], g_ref[...], RMS_EPS).astype(BF16)
    pre = _dot(h, wqkv_ref[...])
    ext_ref[halo:halo + PROJ_ROWS, :] = pre
    cw = cw_ref[...]
    y = cw[CONV_WIDTH - 1:CONV_WIDTH, :] * pre
    for j in range(CONV_WIDTH - 1):
        y = y + cw[j:j + 1, :] * ext_ref[pl.ds(halo - (CONV_WIDTH - 1) + j, PROJ_ROWS), :]
    ext_ref[0:halo, :] = ext_ref[PROJ_ROWS:PROJ_ROWS + halo, :]
    qkv_ref[...] = (y * _sigmoid(y)).astype(BF16)
    z_ref[...] = _dot(h, wz_ref[...]).astype(BF16)
    ab_ref[...] = _dot(h, wab_ref[...])
    df_ref[:, :DIFF_WIDTH] = (_dot(h, wdfq_ref[...]) * (DIFF_HEAD_DIM ** -0.5 * LOG2E)).astype(BF16)
    df_ref[:, DIFF_WIDTH:] = _dot(h, wdfkv_ref[...]).astype(BF16)


def _proj(x1, g, wqkv, wz, wab, wdfq, wdfkv, cw, seq):
    n = x1.shape[0]
    row = lambda w: pl.BlockSpec((PROJ_ROWS, w), lambda i: (i, 0))
    return pl.pallas_call(
        functools.partial(_proj_kernel, tiles_per_seq=seq // PROJ_ROWS),
        grid=(n // PROJ_ROWS,),
        in_specs=[row(D_MODEL), _resident((1, D_MODEL)), _resident(wqkv.shape), _resident(wz.shape),
                  _resident(wab.shape), _resident(wdfq.shape), _resident(wdfkv.shape), _resident(cw.shape)],
        out_specs=[row(3 * DN_WIDTH), row(DN_WIDTH), row(LANES), row(3 * DIFF_WIDTH)],
        out_shape=[jax.ShapeDtypeStruct((n, 3 * DN_WIDTH), BF16),
                   jax.ShapeDtypeStruct((n, DN_WIDTH), BF16),
                   jax.ShapeDtypeStruct((n, LANES), F32),
                   jax.ShapeDtypeStruct((n, 3 * DIFF_WIDTH), BF16)],
        scratch_shapes=[pltpu.VMEM((PROJ_ROWS + SUBLANES, 3 * DN_WIDTH), F32)],
        compiler_params=pltpu.CompilerParams(dimension_semantics=("arbitrary",),
                                             vmem_limit_bytes=VMEM_LIMIT_BYTES),
        name="proj",
    )(x1, g, wqkv, wz, wab, wdfq, wdfkv, cw)


def _gdn_kernel(q_ref, k_ref, v_ref, ab_ref, z_ref, alog_ref, dtb_ref, onorm_ref, o_ref, s_ref, *, batch):
    c = DN_CHUNK
    hd = DN_HEAD_DIM
    units = [(b, p) for b in range(batch) for p in range(DN_PAIRS)]

    @pl.when(pl.program_id(0) == 0)
    def _():
        s_ref[...] = jnp.zeros(s_ref.shape, F32)

    row = lax.broadcasted_iota(jnp.int32, (c, LANES), 0)
    lane = lax.broadcasted_iota(jnp.int32, (c, LANES), 1)
    col = lane & (hd - 1)
    even_bf = jnp.where(lane < hd, 1.0, 0.0).astype(BF16)
    odd_bf = jnp.where(lane < hd, 0.0, 1.0).astype(BF16)
    incl = col <= row
    strict = col < row
    same16 = (row >> 4) == (col >> 4)
    same32 = (row >> 5) == (col >> 5)
    eye = jnp.where(col == row, 1.0, 0.0).astype(F32)
    r2 = lax.broadcasted_iota(jnp.int32, (LANES, LANES), 0)
    l2 = lax.broadcasted_iota(jnp.int32, (LANES, LANES), 1)
    same_head = (r2 >> 6) == (l2 >> 6)
    head_ones = jnp.where(same_head, 1.0, 0.0).astype(BF16)
    head_ones2 = jnp.concatenate([head_ones, head_ones], axis=0)
    r3 = lax.broadcasted_iota(jnp.int32, (3 * LANES, LANES), 0) & (LANES - 1)
    l3 = lax.broadcasted_iota(jnp.int32, (3 * LANES, LANES), 1)
    rc = lax.broadcasted_iota(jnp.int32, (c, 3 * c), 0)
    cc = lax.broadcasted_iota(jnp.int32, (c, 3 * c), 1) & (c - 1)
    lower3 = jnp.where(cc <= rc, 1.0, 0.0).astype(BF16)
    ones3 = jnp.ones((c, 3 * c), BF16)

    def stack(x):
        xb = x.astype(BF16)
        return jnp.concatenate([xb * even_bf, xb * odd_bf], axis=0)

    def mm(a, b):
        return _dot(a.astype(BF16), b.astype(BF16))

    def mm_nt(a, b):
        return _dot_nt(a.astype(BF16), b.astype(BF16))

    def head_sums(x):
        return _dot(jnp.concatenate(_split_bf16(x, 2), axis=1), head_ones2)

    def exact_rows(lhs3, x):
        return _dot(lhs3, jnp.concatenate(_split_bf16(x, 3), axis=0))

    ab3 = [jnp.concatenate(_split_bf16(ab_ref[b], 3), axis=1) for b in range(batch)]
    q_t, k_t, v_t, g_t, beta_t = [], [], [], [], []
    for b, p in units:
        lanes = slice(p * LANES, (p + 1) * LANES)
        expand_a = jnp.where(r3 == 2 * p + (l3 >> 6), 1.0, 0.0).astype(BF16)
        expand_b = jnp.where(r3 == DN_HEADS + 2 * p + (l3 >> 6), 1.0, 0.0).astype(BF16)
        sp_in = _dot(ab3[b], expand_a) + dtb_ref[0:1, lanes]
        softplus = jnp.maximum(sp_in, 0.0) + jnp.log(1.0 + jnp.exp(-jnp.abs(sp_in)))
        g_t.append(-jnp.exp(alog_ref[0:1, lanes]) * softplus)
        beta_t.append(_sigmoid(_dot(ab3[b], expand_b)))
        q = q_ref[b, :, lanes].astype(F32)
        k = k_ref[b, :, lanes].astype(F32)
        q_t.append(q * lax.rsqrt(head_sums(q * q) + L2_EPS) * (hd ** -0.5))
        k_t.append(k * lax.rsqrt(head_sums(k * k) + L2_EPS))
        v_t.append(v_ref[b, :, lanes].astype(F32))

    states = [s_ref[i] for i in range(len(units))]
    outs = [[] for _ in units]
    for n0 in range(0, GDN_ROWS // c, GDN_PREP_CHUNKS):
        items = [(u, n) for n in range(n0, n0 + GDN_PREP_CHUNKS) for u in range(len(units))]

        def take(per_unit):
            return [per_unit[u][n * c:(n + 1) * c] for u, n in items]

        q = take(q_t)
        k = take(k_t)
        v = take(v_t)
        beta = take(beta_t)
        gc = [exact_rows(lower3, x) for x in take(g_t)]
        gc_row = [exact_rows(ones3, x * eye) for x in gc]
        decay = [jnp.where(incl, jnp.exp(jnp.where(incl, a - b_, 0.0)), 0.0) for a, b_ in zip(gc, gc_row)]
        exp_gc = [jnp.exp(x) for x in gc]
        kb = [a * b_ for a, b_ in zip(k, beta)]
        kq = [mm_nt(jnp.concatenate([a, b_], axis=0), stack(k_)) for a, b_, k_ in zip(kb, q, k)]
        a_mat = [jnp.where(strict, x[:c] * d, 0.0) for x, d in zip(kq, decay)]
        attn = [x[c:] * d for x, d in zip(kq, decay)]

        neg_d = [jnp.where(strict & same16, -x, 0.0) for x in a_mat]
        t_inv = [eye + x for x in neg_d]
        power = [mm(x, stack(x)) for x in neg_d]
        for _ in range(2):
            both = [mm(jnp.concatenate([pw, t], axis=0), stack(pw)) for pw, t in zip(power, t_inv)]
            power = [x[:c] for x in both]
            t_inv = [t + x[c:] for t, x in zip(t_inv, both)]
        t_inv = [t + mm(t, stack(pw)) for t, pw in zip(t_inv, power)]
        for block in (strict & same32 & ~same16, strict & ~same32):
            left = [mm(t, stack(jnp.where(block, x, 0.0))) for t, x in zip(t_inv, a_mat)]
            t_inv = [t - mm(x, stack(t)) for t, x in zip(t_inv, left)]

        uw = [mm(t, jnp.concatenate([stack(v_ * b_), stack(kb_ * e)], axis=1))
              for t, v_, b_, kb_, e in zip(t_inv, v, beta, kb, exp_gc)]
        q_dec = [a * e for a, e in zip(q, exp_gc)]
        gc_last = [x[c - 1:c, :] for x in gc]
        k_dec_t = [(k_ * jnp.exp(gl - g_)).T for k_, gl, g_ in zip(k, gc_last, gc)]

        for first in range(0, len(items), len(units)):
            sel = slice(first, first + len(units))
            ws = [mm(jnp.concatenate([x[:, LANES:], qd], axis=0), st)
                  for x, qd, st in zip(uw[sel], q_dec[sel], states)]
            v_new = [x[:, :LANES] - y[:c] for x, y in zip(uw[sel], ws)]
            o_n = [y[c:] + mm(a, stack(vn)) for y, a, vn in zip(ws, attn[sel], v_new)]
            for lst, x in zip(outs, o_n):
                lst.append(x)
            states = [st * jnp.exp(gl) + jnp.where(same_head, mm(kt, vn), 0.0)
                      for st, gl, kt, vn in zip(states, gc_last[sel], k_dec_t[sel], v_new)]

    for i, (b, p) in enumerate(units):
        lanes = slice(p * LANES, (p + 1) * LANES)
        s_ref[i] = states[i]
        o = jnp.concatenate(outs[i], axis=0)
        ms = head_sums(o * o) * (1.0 / hd)
        z = z_ref[b, :, lanes].astype(F32)
        o_ref[b, :, lanes] = (o * lax.rsqrt(ms + RMS_EPS) * onorm_ref[0:1, lanes] * (z * _sigmoid(z))).astype(BF16)


def _gdn(qkv, ab, z, alog_lanes, dtb_lanes, onorm_lanes, batch, seq):
    qkv3 = qkv.reshape(batch, seq, 3 * DN_WIDTH)

    def tile(width, col):
        return pl.BlockSpec((batch, GDN_ROWS, width), lambda t: (0, t, col))

    return pl.pallas_call(
        functools.partial(_gdn_kernel, batch=batch),
        grid=(seq // GDN_ROWS,),
        in_specs=[tile(DN_WIDTH, 0), tile(DN_WIDTH, 1), tile(DN_WIDTH, 2), tile(LANES, 0), tile(DN_WIDTH, 0),
                  _resident(alog_lanes.shape), _resident(dtb_lanes.shape), _resident(onorm_lanes.shape)],
        out_specs=tile(DN_WIDTH, 0),
        out_shape=jax.ShapeDtypeStruct((batch, seq, DN_WIDTH), BF16),
        scratch_shapes=[pltpu.VMEM((batch * DN_PAIRS, LANES, LANES), F32)],
        compiler_params=pltpu.CompilerParams(dimension_semantics=("arbitrary",),
                                             vmem_limit_bytes=VMEM_LIMIT_BYTES),
        name="gdn",
    )(qkv3, qkv3, qkv3, ab.reshape(batch, seq, LANES), z.reshape(batch, seq, DN_WIDTH),
      alog_lanes, dtb_lanes, onorm_lanes).reshape(batch * seq, DN_WIDTH)


def _attn_kernel(off_ref, coef_ref, q_ref, k_ref, v_ref, qcols_ref, kcols_ref, vcols_ref, lam_ref, subln_ref,
                 o_ref, qa_ref, m_ref, acc_ref, k2max_ref, *, lam_init):
    h = pl.program_id(1)
    i = pl.program_id(2)
    d = DIFF_HEAD_DIM
    tq = ATTN_Q
    tk = ATTN_K

    q = q_ref[...]
    first = lax.broadcasted_iota(jnp.int32, q.shape, 1) < d
    qcols = jnp.broadcast_to(qcols_ref[0, 0:1, :], (tq, LANES))
    qa_ref[0:tq, :] = jnp.concatenate([jnp.where(first, q, 0), qcols], axis=1)
    qa_ref[tq:2 * tq, :] = jnp.concatenate([jnp.where(first, 0, q), qcols], axis=1)
    m_ref[...] = jnp.full(m_ref.shape, -jnp.inf, F32)
    acc_ref[...] = jnp.zeros(acc_ref.shape, F32)
    kcols = kcols_ref[...]
    vcols = vcols_ref[...]

    def max_sq_norm(x):
        sq = x.astype(F32) * x.astype(F32)
        r = lax.broadcasted_iota(jnp.int32, (LANES, LANES), 0)
        c = lax.broadcasted_iota(jnp.int32, (LANES, LANES), 1)
        same_half = jnp.where((r < d) == (c < d), 1.0, 0.0).astype(BF16)
        half_sums = _dot(sq.astype(BF16), same_half)
        return jnp.max(jnp.max(half_sums, axis=0, keepdims=True), axis=1, keepdims=True)

    @pl.when(i == 0)
    def _():
        k2max_ref[...] = jnp.broadcast_to(max_sq_norm(k_ref[...]), k2max_ref.shape)

    qk_bound = jnp.sqrt(max_sq_norm(q) * k2max_ref[0:1, 0:1]) * NORM_SLACK
    gap = (SKIP_LOG2 + 2.0 * qk_bound) / coef_ref[h]
    first_block = jnp.floor(((i * tq).astype(F32) - gap) * (1.0 / tk))
    j0 = jnp.clip(first_block, 0.0, i.astype(F32)).astype(jnp.int32)[0, 0]

    def kv_steps(blocks):
        scores, values, offs = [], [], []
        for j, span, masked in blocks:
            start = pl.multiple_of(j * tk, tk)
            keys = span * tk
            ka = jnp.concatenate([k_ref[pl.ds(start, keys), :], kcols[:keys]], axis=1)
            rg = ATTN_DIAG_GROUP if masked else ATTN_ROW_GROUP
            groups = range(0, 2 * tq, rg)
            widths = [(r0 % tq) + rg if masked else keys for r0 in groups]
            scores.append([(r0, _dot_nt(qa_ref[r0:r0 + rg, :], ka[:w])) for r0, w in zip(groups, widths)])
            values.append(jnp.concatenate([v_ref[pl.ds(start, keys), :], vcols[:keys]], axis=1))
            offs.append(off_ref[h] * j.astype(F32))
        for (_, _, masked), block_scores, va, off in zip(blocks, scores, values, offs):
            for r0, s in block_scores:
                rg, width = s.shape
                rows = slice(r0, r0 + rg)
                if masked:
                    qrow = lax.broadcasted_iota(jnp.int32, (rg, width), 0) + (r0 % tq)
                    kcol = lax.broadcasted_iota(jnp.int32, (rg, width), 1)
                    s = jnp.where(kcol <= qrow, s, -jnp.inf)
                m_old = m_ref[rows, :]
                m_blk = jnp.broadcast_to(jnp.max(s, axis=-1, keepdims=True), (rg, LANES)) + off
                m_new = jnp.maximum(m_old, m_blk)
                alpha = jnp.exp2(m_old - m_new)
                shift = m_new - off
                p = jnp.concatenate([jnp.exp2((s[:, c0:c0 + LANES] - shift).astype(BF16))
                                     for c0 in range(0, width, LANES)], axis=1)
                acc_ref[rows, :] = (jnp.concatenate([alpha, alpha], axis=1) * acc_ref[rows, :]
                                    + _dot(p, va[:width]))
                m_ref[rows, :] = m_new

    def body4(t, carry):
        kv_steps([(j0 + 4 * t, ATTN_SPAN, False), (j0 + 4 * t + 2, ATTN_SPAN, False)])
        return carry

    def body(t, carry):
        kv_steps([(j0 + 2 * t, ATTN_SPAN, False)])
        return carry

    kv_steps([(i, 1, True)])
    n_past = i - j0
    n_quads = n_past // 4
    lax.fori_loop(0, n_quads, body4, 0)
    lax.fori_loop(2 * n_quads, n_past // 2, body, 0)

    @pl.when(n_past % 2 == 1)
    def _():
        kv_steps([(i - 1, 1, False)])

    lam_rows = lam_ref[...]
    lam = (jnp.exp(jnp.sum(lam_rows[0:1] * lam_rows[1:2], axis=-1, keepdims=True))
           - jnp.exp(jnp.sum(lam_rows[2:3] * lam_rows[3:4], axis=-1, keepdims=True)) + lam_init)
    acc1 = acc_ref[0:tq, :]
    acc2 = acc_ref[tq:2 * tq, :]
    o = acc1[:, :LANES] / acc1[:, LANES:] - lam * (acc2[:, :LANES] / acc2[:, LANES:])
    o_ref[...] = (_rms(o, subln_ref[...], SUBLN_EPS) * (1.0 - lam_init)).astype(BF16)


def _attn(slopes, df, lam_rows, subln, batch, seq, lam_init):
    span_keys = ATTN_SPAN * ATTN_K
    assert ATTN_Q == ATTN_K and span_keys <= 256 * POS_LO and ATTN_Q % ATTN_ROW_GROUP == 0
    nq = seq // ATTN_Q
    df3 = df.reshape(batch, seq, 3 * DIFF_WIDTH)
    coef = slopes * LOG2E
    c_parts = jnp.stack(_split_bf16(coef, 3), axis=1)
    qcols = jnp.zeros((DIFF_HEADS, SUBLANES, LANES), BF16).at[:, :, 0:6].set(
        jnp.broadcast_to(jnp.tile(c_parts, (1, 2))[:, None, :], (DIFF_HEADS, SUBLANES, 6)))
    pos = np.arange(span_keys)
    kcols_np = np.zeros((span_keys, LANES), np.float32)
    kcols_np[:, 0:3] = (pos // POS_LO * POS_LO)[:, None]
    kcols_np[:, 3:6] = (pos % POS_LO)[:, None]
    kcols = jnp.asarray(kcols_np, BF16)
    vcols = jnp.ones((span_keys, LANES), BF16)
    block_off = coef * ATTN_K

    seq_block = lambda col0: pl.BlockSpec((None, seq, LANES), lambda b, h, i: (b, 0, col0 + h))
    const = lambda shape: pl.BlockSpec(shape, lambda b, h, i: (0,) * len(shape))
    return pl.pallas_call(
        functools.partial(_attn_kernel, lam_init=lam_init),
        grid=(batch, DIFF_HEADS, nq),
        in_specs=[pl.BlockSpec(memory_space=pltpu.SMEM), pl.BlockSpec(memory_space=pltpu.SMEM),
                  pl.BlockSpec((None, ATTN_Q, LANES), lambda b, h, i: (b, i, h)),
                  seq_block(DIFF_HEADS), seq_block(2 * DIFF_HEADS),
                  pl.BlockSpec((1, SUBLANES, LANES), lambda b, h, i: (h, 0, 0)),
                  const((span_keys, LANES)), const((span_keys, LANES)),
                  const((SUBLANES, LANES)), const((1, DIFF_V_DIM))],
        out_specs=pl.BlockSpec((None, ATTN_Q, LANES), lambda b, h, i: (b, i, h)),
        out_shape=jax.ShapeDtypeStruct((batch, seq, DIFF_WIDTH), BF16),
        scratch_shapes=[pltpu.VMEM((2 * ATTN_Q, 2 * LANES), BF16),
                        pltpu.VMEM((2 * ATTN_Q, LANES), F32),
                        pltpu.VMEM((2 * ATTN_Q, 2 * LANES), F32),
                        pltpu.VMEM((SUBLANES, LANES), F32)],
        compiler_params=pltpu.CompilerParams(dimension_semantics=("arbitrary", "arbitrary", "arbitrary"),
                                             vmem_limit_bytes=VMEM_LIMIT_BYTES),
        name="diff_attn",
    )(block_off, coef, df3, df3, df3, qcols, kcols, vcols, lam_rows, subln).reshape(batch * seq, DIFF_WIDTH)


def _merge_ffn_kernel(x_ref, g_ref, oa_ref, ob_ref, wa_ref, wb_ref, wgate_ref, wout_ref,
                      fg_norm_ref, wg_ref, wu_ref, wd_ref, fg_ref, o_ref, *, final_norm):
    x = x_ref[...]
    h = _rms(x, g_ref[...], RMS_EPS).astype(BF16)
    gates = _sigmoid(_dot(h, wgate_ref[...]))
    y_a = _dot(oa_ref[...], wa_ref[...])
    y_b = _dot(ob_ref[...], wb_ref[...])
    mixed = (gates[:, :D_MODEL] * y_a + gates[:, D_MODEL:] * y_b).astype(BF16)
    x = x + _dot(mixed, wout_ref[...])
    o_ref[...] = _ffn_half_step(x, fg_norm_ref, wg_ref, wu_ref, wd_ref, fg_ref, final_norm)


def _merge_ffn(x1, g, o_a, o_b, wa, wb, wgate, wout, ffn_g, wg, wu, wd, fg, final_norm):
    n = x1.shape[0]
    row = lambda w: pl.BlockSpec((MERGE_ROWS, w), lambda i: (i, 0))
    return pl.pallas_call(
        functools.partial(_merge_ffn_kernel, final_norm=final_norm),
        grid=(n // MERGE_ROWS,),
        in_specs=[row(D_MODEL), _resident((1, D_MODEL)), row(DN_WIDTH), row(DIFF_WIDTH),
                  _resident(wa.shape), _resident(wb.shape), _resident(wgate.shape), _resident(wout.shape),
                  _resident((1, D_MODEL)), _resident(wg.shape), _resident(wu.shape), _resident(wd.shape),
                  _resident((1, D_MODEL))],
        out_specs=row(D_MODEL),
        out_shape=jax.ShapeDtypeStruct((n, D_MODEL), F32),
        compiler_params=pltpu.CompilerParams(dimension_semantics=("arbitrary",),
                                             vmem_limit_bytes=VMEM_LIMIT_BYTES),
        name="merge_ffn",
    )(x1, g, o_a, o_b, wa, wb, wgate, wout, ffn_g, wg, wu, wd, fg)


def _lambda_init(layer_idx):
    return 0.8 - 0.6 * math.exp(-0.3 * layer_idx)


def _head_lanes(per_head):
    lanes = jnp.repeat(per_head.astype(F32), DN_HEAD_DIM)
    return jnp.broadcast_to(lanes[None, :], (SUBLANES, DN_WIDTH))


def kernel(x, ffn1_norm, ffn1_w_gate, ffn1_w_up, ffn1_w_down, mix_norm, w_in, conv_qkv, dn_a_log, dn_dt_bias, dn_out_norm, diff_lambda_q1, diff_lambda_k1, diff_lambda_q2, diff_lambda_k2, diff_subln, w_branch_a, w_branch_b, w_out, ffn2_norm, ffn2_w_gate, ffn2_w_up, ffn2_w_down, final_norm):
    batch, seq, _ = x.shape
    depth = ffn1_norm.shape[0]
    n = batch * seq
    xs = x.reshape(n, D_MODEL)
    slopes = jnp.asarray([2.0 ** (-8.0 * (i + 1) / DIFF_HEADS) for i in range(DIFF_HEADS)], F32)
    final_g = final_norm.reshape(1, D_MODEL)

    for l in range(depth):
        (f1_gate, f1_up, f1_down, f2_gate, f2_up, f2_down, wa, wb, wout) = _to_bf16(
            ffn1_w_gate[l], ffn1_w_up[l], ffn1_w_down[l], ffn2_w_gate[l], ffn2_w_up[l], ffn2_w_down[l],
            w_branch_a[l], w_branch_b[l], w_out[l])
        wqkv, wz, wab, wdfq, wdfkv, wgates = _split_w_in(w_in, l)

        xs = _ffn(xs, ffn1_norm[l].reshape(1, D_MODEL), f1_gate, f1_up, f1_down, final_g, final_norm=False)

        qkv, z, ab, df = _proj(xs, mix_norm[l].reshape(1, D_MODEL), wqkv, wz, wab, wdfq, wdfkv, conv_qkv[l], seq)

        o_a = _gdn(qkv, ab, z, _head_lanes(dn_a_log[l]), _head_lanes(dn_dt_bias[l]),
                   jnp.broadcast_to(jnp.tile(dn_out_norm[l].astype(F32), DN_HEADS)[None, :], (SUBLANES, DN_WIDTH)),
                   batch, seq)

        lam_rows = jnp.zeros((SUBLANES, LANES), F32).at[0:4, :DIFF_HEAD_DIM].set(
            jnp.stack([diff_lambda_q1[l], diff_lambda_k1[l], diff_lambda_q2[l], diff_lambda_k2[l]]).astype(F32))
        o_b = _attn(slopes, df, lam_rows, diff_subln[l].reshape(1, DIFF_V_DIM), batch, seq, _lambda_init(l))

        xs = _merge_ffn(xs, mix_norm[l].reshape(1, D_MODEL), o_a, o_b, wa, wb, wgates, wout,
                        ffn2_norm[l].reshape(1, D_MODEL), f2_gate, f2_up, f2_down, final_g,
                        final_norm=(l == depth - 1))

    return xs.reshape(batch, seq, D_MODEL)
```

```python
import functools
import math

import jax
import jax.numpy as jnp
import numpy as np
from jax import lax
from jax.experimental import pallas as pl
from jax.experimental.pallas import tpu as pltpu

F32 = jnp.float32
BF16 = jnp.bfloat16

D_MODEL = 1024
D_FF = 2816
RMS_EPS = 1e-6
SUBLN_EPS = 1e-5
L2_EPS = 1e-6
CONV_WIDTH = 4
DN_HEADS = 8
DN_HEAD_DIM = 64
DN_WIDTH = DN_HEADS * DN_HEAD_DIM
DN_CHUNK = 64
DN_PAIRS = DN_HEADS // 2
DIFF_HEADS = 4
DIFF_HEAD_DIM = 64
DIFF_V_DIM = 2 * DIFF_HEAD_DIM
DIFF_WIDTH = DIFF_HEADS * DIFF_V_DIM
LANES = 128
SUBLANES = 8
LOG2E = math.log2(math.e)

VMEM_LIMIT_BYTES = 56 * 1024 * 1024

FFN_ROWS = 512
FF_CHUNKS = ((0, 512), (512, 1024), (1024, 1536), (1536, 2048), (2048, 2560), (2560, 2816))
PROJ_ROWS = 512
GDN_ROWS = 512
GDN_PREP_CHUNKS = 2
ATTN_Q = 512
ATTN_K = 512
ATTN_SPAN = 2
ATTN_ROW_GROUP = 512
ATTN_DIAG_GROUP = 128
MERGE_ROWS = 512
CAST_STEPS = 8
POS_LO = 16
SKIP_LOG2 = 160.0
NORM_SLACK = 1.01


def _rms(x, g, eps):
    return x * lax.rsqrt(jnp.mean(x * x, axis=-1, keepdims=True) + eps) * g


def _sigmoid(x):
    return 1.0 / (1.0 + jnp.exp(-x))


def _dot(a, b):
    return jnp.dot(a, b, preferred_element_type=F32)


def _dot_nt(a, b):
    return lax.dot_general(a, b, (((1,), (1,)), ((), ())), preferred_element_type=F32)


def _split_bf16(x, terms):
    parts = []
    rest = x
    for t in range(terms):
        part = rest.astype(BF16)
        parts.append(part)
        if t + 1 < terms:
            rest = rest - part.astype(F32)
    return parts


def _resident(shape):
    return pl.BlockSpec(shape, lambda *_: (0,) * len(shape), pipeline_mode=pl.Buffered(1))


def _cast_kernel(*refs):
    half = len(refs) // 2
    for src, dst in zip(refs[:half], refs[half:]):
        dst[...] = src[...].astype(BF16)


def _to_bf16(*arrays):
    specs = [pl.BlockSpec((a.shape[0] // CAST_STEPS, a.shape[1]), lambda i: (i, 0)) for a in arrays]
    return pl.pallas_call(
        _cast_kernel,
        grid=(CAST_STEPS,),
        in_specs=specs,
        out_specs=specs,
        out_shape=[jax.ShapeDtypeStruct(a.shape, BF16) for a in arrays],
        compiler_params=pltpu.CompilerParams(dimension_semantics=("arbitrary",),
                                             vmem_limit_bytes=VMEM_LIMIT_BYTES),
        name="cast_weights",
    )(*arrays)


def _split_w_in_kernel(w_ref, qkv_ref, z_ref, ab_ref, dfq_ref, dfkv_ref, gates_ref):
    c_z = 3 * DN_WIDTH
    c_ab = c_z + DN_WIDTH
    c_dfq = c_ab + 2 * DN_HEADS
    c_dfkv = c_dfq + DIFF_WIDTH
    c_gates = c_dfkv + 2 * DIFF_WIDTH
    w = w_ref[...]
    qkv_ref[...] = w[:, :c_z].astype(BF16)
    z_ref[...] = w[:, c_z:c_ab].astype(BF16)
    lane = lax.broadcasted_iota(jnp.int32, (w.shape[0], LANES), 1)
    ab_ref[...] = jnp.where(lane < 2 * DN_HEADS, w[:, c_ab:c_ab + LANES], 0.0).astype(BF16)
    dfq_ref[...] = w[:, c_dfq:c_dfkv].astype(BF16)
    dfkv_ref[...] = w[:, c_dfkv:c_gates].astype(BF16)
    gates_ref[...] = w[:, c_gates:c_gates + 2 * D_MODEL].astype(BF16)


def _split_w_in(w, layer):
    rows = w.shape[1] // CAST_STEPS
    widths = (3 * DN_WIDTH, DN_WIDTH, LANES, DIFF_WIDTH, 2 * DIFF_WIDTH, 2 * D_MODEL)
    return pl.pallas_call(
        _split_w_in_kernel,
        grid=(CAST_STEPS,),
        in_specs=[pl.BlockSpec((None, rows, w.shape[2]), lambda i: (layer, i, 0))],
        out_specs=[pl.BlockSpec((rows, c), lambda i: (i, 0)) for c in widths],
        out_shape=[jax.ShapeDtypeStruct((w.shape[1], c), BF16) for c in widths],
        compiler_params=pltpu.CompilerParams(dimension_semantics=("arbitrary",),
                                             vmem_limit_bytes=VMEM_LIMIT_BYTES),
        name="split_w_in",
    )(w)


def _ffn_half_step(x, g_ref, wg_ref, wu_ref, wd_ref, fg_ref, final_norm):
    h = _rms(x, g_ref[...], RMS_EPS).astype(BF16)
    acc = None
    for c0, c1 in FF_CHUNKS:
        gate = _dot(h, wg_ref[:, c0:c1])
        up = _dot(h, wu_ref[:, c0:c1])
        act = (gate * _sigmoid(gate) * up).astype(BF16)
        part = _dot(act, wd_ref[c0:c1, :])
        acc = part if acc is None else acc + part
    y = x + 0.5 * acc
    if final_norm:
        y = _rms(y, fg_ref[...], RMS_EPS)
    return y


def _ffn_kernel(x_ref, g_ref, wg_ref, wu_ref, wd_ref, fg_ref, o_ref, *, final_norm):
    o_ref[...] = _ffn_half_step(x_ref[...], g_ref, wg_ref, wu_ref, wd_ref, fg_ref, final_norm)


def _ffn(x, g, wg, wu, wd, fg, final_norm):
    n = x.shape[0]
    row = pl.BlockSpec((FFN_ROWS, D_MODEL), lambda i: (i, 0))
    return pl.pallas_call(
        functools.partial(_ffn_kernel, final_norm=final_norm),
        grid=(n // FFN_ROWS,),
        in_specs=[row, _resident((1, D_MODEL)), _resident((D_MODEL, D_FF)), _resident((D_MODEL, D_FF)),
                  _resident((D_FF, D_MODEL)), _resident((1, D_MODEL))],
        out_specs=row,
        out_shape=jax.ShapeDtypeStruct((n, D_MODEL), F32),
        compiler_params=pltpu.CompilerParams(dimension_semantics=("arbitrary",),
                                             vmem_limit_bytes=VMEM_LIMIT_BYTES),
        name="ffn",
    )(x, g, wg, wu, wd, fg)


def _proj_kernel(x_ref, g_ref, wqkv_ref, wz_ref, wab_ref, wdfq_ref, wdfkv_ref, cw_ref,
                 qkv_ref, z_ref, ab_ref, df_ref, ext_ref, *, tiles_per_seq):
    halo = SUBLANES

    @pl.when(pl.program_id(0) % tiles_per_seq == 0)
    def _():
        ext_ref[0:halo, :] = jnp.zeros((halo, 3 * DN_WIDTH), F32)

    h = _rms(x_ref[...], g_ref[...], RMS_EPS).astype(BF16)
    pre = _dot(h, wqkv_ref[...])
    ext_ref[halo:halo + PROJ_ROWS, :] = pre
    cw = cw_ref[...]
    y = cw[CONV_WIDTH - 1:CONV_WIDTH, :] * pre
    for j in range(CONV_WIDTH - 1):
        y = y + cw[j:j + 1, :] * ext_ref[pl.ds(halo - (CONV_WIDTH - 1) + j, PROJ_ROWS), :]
    ext_ref[0:halo, :] = ext_ref[PROJ_ROWS:PROJ_ROWS + halo, :]
    qkv_ref[...] = (y * _sigmoid(y)).astype(BF16)
    z_ref[...] = _dot(h, wz_ref[...]).astype(BF16)
    ab_ref[...] = _dot(h, wab_ref[...])
    df_ref[:, :DIFF_WIDTH] = (_dot(h, wdfq_ref[...]) * (DIFF_HEAD_DIM ** -0.5 * LOG2E)).astype(BF16)
    df_ref[:, DIFF_WIDTH:] = _dot(h, wdfkv_ref[...]).astype(BF16)


def _proj(x1, g, wqkv, wz, wab, wdfq, wdfkv, cw, seq):
    n = x1.shape[0]
    row = lambda w: pl.BlockSpec((PROJ_ROWS, w), lambda i: (i, 0))
    return pl.pallas_call(
        functools.partial(_proj_kernel, tiles_per_seq=seq // PROJ_ROWS),
        grid=(n // PROJ_ROWS,),
        in_specs=[row(D_MODEL), _resident((1, D_MODEL)), _resident(wqkv.shape), _resident(wz.shape),
                  _resident(wab.shape), _resident(wdfq.shape), _resident(wdfkv.shape), _resident(cw.shape)],
        out_specs=[row(3 * DN_WIDTH), row(DN_WIDTH), row(LANES), row(3 * DIFF_WIDTH)],
        out_shape=[jax.ShapeDtypeStruct((n, 3 * DN_WIDTH), BF16),
                   jax.ShapeDtypeStruct((n, DN_WIDTH), BF16),
                   jax.ShapeDtypeStruct((n, LANES), F32),
                   jax.ShapeDtypeStruct((n, 3 * DIFF_WIDTH), BF16)],
        scratch_shapes=[pltpu.VMEM((PROJ_ROWS + SUBLANES, 3 * DN_WIDTH), F32)],
        compiler_params=pltpu.CompilerParams(dimension_semantics=("arbitrary",),
                                             vmem_limit_bytes=VMEM_LIMIT_BYTES),
        name="proj",
    )(x1, g, wqkv, wz, wab, wdfq, wdfkv, cw)


def _gdn_kernel(q_ref, k_ref, v_ref, ab_ref, z_ref, alog_ref, dtb_ref, onorm_ref, o_ref, s_ref, *, batch):
    c = DN_CHUNK
    hd = DN_HEAD_DIM
    units = [(b, p) for b in range(batch) for p in range(DN_PAIRS)]

    @pl.when(pl.program_id(0) == 0)
    def _():
        s_ref[...] = jnp.zeros(s_ref.shape, F32)

    row = lax.broadcasted_iota(jnp.int32, (c, LANES), 0)
    lane = lax.broadcasted_iota(jnp.int32, (c, LANES), 1)
    col = lane & (hd - 1)
    even_bf = jnp.where(lane < hd, 1.0, 0.0).astype(BF16)
    odd_bf = jnp.where(lane < hd, 0.0, 1.0).astype(BF16)
    incl = col <= row
    strict = col < row
    same16 = (row >> 4) == (col >> 4)
    same32 = (row >> 5) == (col >> 5)
    eye = jnp.where(col == row, 1.0, 0.0).astype(F32)
    r2 = lax.broadcasted_iota(jnp.int32, (LANES, LANES), 0)
    l2 = lax.broadcasted_iota(jnp.int32, (LANES, LANES), 1)
    same_head = (r2 >> 6) == (l2 >> 6)
    head_ones = jnp.where(same_head, 1.0, 0.0).astype(BF16)
    head_ones2 = jnp.concatenate([head_ones, head_ones], axis=0)
    r3 = lax.broadcasted_iota(jnp.int32, (3 * LANES, LANES), 0) & (LANES - 1)
    l3 = lax.broadcasted_iota(jnp.int32, (3 * LANES, LANES), 1)
    rc = lax.broadcasted_iota(jnp.int32, (c, 3 * c), 0)
    cc = lax.broadcasted_iota(jnp.int32, (c, 3 * c), 1) & (c - 1)
    lower3 = jnp.where(cc <= rc, 1.0, 0.0).astype(BF16)
    ones3 = jnp.ones((c, 3 * c), BF16)

    def stack(x):
        xb = x.astype(BF16)
        return jnp.concatenate([xb * even_bf, xb * odd_bf], axis=0)

    def mm(a, b):
        return _dot(a.astype(BF16), b.astype(BF16))

    def mm_nt(a, b):
        return _dot_nt(a.astype(BF16), b.astype(BF16))

    def head_sums(x):
        return _dot(jnp.concatenate(_split_bf16(x, 2), axis=1), head_ones2)

    def exact_rows(lhs3, x):
        return _dot(lhs3, jnp.concatenate(_split_bf16(x, 3), axis=0))

    ab3 = [jnp.concatenate(_split_bf16(ab_ref[b], 3), axis=1) for b in range(batch)]
    q_t, k_t, v_t, g_t, beta_t = [], [], [], [], []
    for b, p in units:
        lanes = slice(p * LANES, (p + 1) * LANES)
        expand_a = jnp.where(r3 == 2 * p + (l3 >> 6), 1.0, 0.0).astype(BF16)
        expand_b = jnp.where(r3 == DN_HEADS + 2 * p + (l3 >> 6), 1.0, 0.0).astype(BF16)
        sp_in = _dot(ab3[b], expand_a) + dtb_ref[0:1, lanes]
        softplus = jnp.maximum(sp_in, 0.0) + jnp.log(1.0 + jnp.exp(-jnp.abs(sp_in)))
        g_t.append(-jnp.exp(alog_ref[0:1, lanes]) * softplus)
        beta_t.append(_sigmoid(_dot(ab3[b], expand_b)))
        q = q_ref[b, :, lanes].astype(F32)
        k = k_ref[b, :, lanes].astype(F32)
        q_t.append(q * lax.rsqrt(head_sums(q * q) + L2_EPS) * (hd ** -0.5))
        k_t.append(k * lax.rsqrt(head_sums(k * k) + L2_EPS))
        v_t.append(v_ref[b, :, lanes].astype(F32))

    states = [s_ref[i] for i in range(len(units))]
    outs = [[] for _ in units]
    for n0 in range(0, GDN_ROWS // c, GDN_PREP_CHUNKS):
        items = [(u, n) for n in range(n0, n0 + GDN_PREP_CHUNKS) for u in range(len(units))]

        def take(per_unit):
            return [per_unit[u][n * c:(n + 1) * c] for u, n in items]

        q = take(q_t)
        k = take(k_t)
        v = take(v_t)
        beta = take(beta_t)
        gc = [exact_rows(lower3, x) for x in take(g_t)]
        gc_row = [exact_rows(ones3, x * eye) for x in gc]
        decay = [jnp.where(incl, jnp.exp(jnp.where(incl, a - b_, 0.0)), 0.0) for a, b_ in zip(gc, gc_row)]
        exp_gc = [jnp.exp(x) for x in gc]
        kb = [a * b_ for a, b_ in zip(k, beta)]
        kq = [mm_nt(jnp.concatenate([a, b_], axis=0), stack(k_)) for a, b_, k_ in zip(kb, q, k)]
        a_mat = [jnp.where(strict, x[:c] * d, 0.0) for x, d in zip(kq, decay)]
        attn = [x[c:] * d for x, d in zip(kq, decay)]

        neg_d = [jnp.where(strict & same16, -x, 0.0) for x in a_mat]
        t_inv = [eye + x for x in neg_d]
        power = [mm(x, stack(x)) for x in neg_d]
        for _ in range(2):
            both = [mm(jnp.concatenate([pw, t], axis=0), stack(pw)) for pw, t in zip(power, t_inv)]
            power = [x[:c] for x in both]
            t_inv = [t + x[c:] for t, x in zip(t_inv, both)]
        t_inv = [t + mm(t, stack(pw)) for t, pw in zip(t_inv, power)]
        for block in (strict & same32 & ~same16, strict & ~same32):
            left = [mm(t, stack(jnp.where(block, x, 0.0))) for t, x in zip(t_inv, a_mat)]
            t_inv = [t - mm(x, stack(t)) for t, x in zip(t_inv, left)]

        uw = [mm(t, jnp.concatenate([stack(v_ * b_), stack(kb_ * e)], axis=1))
              for t, v_, b_, kb_, e in zip(t_inv, v, beta, kb, exp_gc)]
        q_dec = [a * e for a, e in zip(q, exp_gc)]
        gc_last = [x[c - 1:c, :] for x in gc]
        k_dec_t = [(k_ * jnp.exp(gl - g_)).T for k_, gl, g_ in zip(k, gc_last, gc)]

        for first in range(0, len(items), len(units)):
            sel = slice(first, first + len(units))
            ws = [mm(jnp.concatenate([x[:, LANES:], qd], axis=0), st)
                  for x, qd, st in zip(uw[sel], q_dec[sel], states)]
            v_new = [x[:, :LANES] - y[:c] for x, y in zip(uw[sel], ws)]
            o_n = [y[c:] + mm(a, stack(vn)) for y, a, vn in zip(ws, attn[sel], v_new)]
            for lst, x in zip(outs, o_n):
                lst.append(x)
            states = [st * jnp.exp(gl) + jnp.where(same_head, mm(kt, vn), 0.0)
                      for st, gl, kt, vn in zip(states, gc_last[sel], k_dec_t[sel], v_new)]

    for i, (b, p) in enumerate(units):
        lanes = slice(p * LANES, (p + 1) * LANES)
        s_ref[i] = states[i]
        o = jnp.concatenate(outs[i], axis=0)
        ms = head_sums(o * o) * (1.0 / hd)
        z = z_ref[b, :, lanes].astype(F32)
        o_ref[b, :, lanes] = (o * lax.rsqrt(ms + RMS_EPS) * onorm_ref[0:1, lanes] * (z * _sigmoid(z))).astype(BF16)


def _gdn(qkv, ab, z, alog_lanes, dtb_lanes, onorm_lanes, batch, seq):
    qkv3 = qkv.reshape(batch, seq, 3 * DN_WIDTH)

    def tile(width, col):
        return pl.BlockSpec((batch, GDN_ROWS, width), lambda t: (0, t, col))

    return pl.pallas_call(
        functools.partial(_gdn_kernel, batch=batch),
        grid=(seq // GDN_ROWS,),
        in_specs=[tile(DN_WIDTH, 0), tile(DN_WIDTH, 1), tile(DN_WIDTH, 2), tile(LANES, 0), tile(DN_WIDTH, 0),
                  _resident(alog_lanes.shape), _resident(dtb_lanes.shape), _resident(onorm_lanes.shape)],
        out_specs=tile(DN_WIDTH, 0),
        out_shape=jax.ShapeDtypeStruct((batch, seq, DN_WIDTH), BF16),
        scratch_shapes=[pltpu.VMEM((batch * DN_PAIRS, LANES, LANES), F32)],
        compiler_params=pltpu.CompilerParams(dimension_semantics=("arbitrary",),
                                             vmem_limit_bytes=VMEM_LIMIT_BYTES),
        name="gdn",
    )(qkv3, qkv3, qkv3, ab.reshape(batch, seq, LANES), z.reshape(batch, seq, DN_WIDTH),
      alog_lanes, dtb_lanes, onorm_lanes).reshape(batch * seq, DN_WIDTH)


def _attn_kernel(off_ref, coef_ref, q_ref, k_ref, v_ref, qcols_ref, kcols_ref, vcols_ref, lam_ref, subln_ref,
                 o_ref, qa_ref, m_ref, acc_ref, k2max_ref, *, lam_init):
    h = pl.program_id(1)
    i = pl.program_id(2)
    d = DIFF_HEAD_DIM
    tq = ATTN_Q
    tk = ATTN_K

    q = q_ref[...]
    first = lax.broadcasted_iota(jnp.int32, q.shape, 1) < d
    qcols = jnp.broadcast_to(qcols_ref[0, 0:1, :], (tq, LANES))
    qa_ref[0:tq, :] = jnp.concatenate([jnp.where(first, q, 0), qcols], axis=1)
    qa_ref[tq:2 * tq, :] = jnp.concatenate([jnp.where(first, 0, q), qcols], axis=1)
    m_ref[...] = jnp.full(m_ref.shape, -jnp.inf, F32)
    acc_ref[...] = jnp.zeros(acc_ref.shape, F32)
    kcols = kcols_ref[...]
    vcols = vcols_ref[...]

    def max_sq_norm(x):
        sq = x.astype(F32) * x.astype(F32)
        r = lax.broadcasted_iota(jnp.int32, (LANES, LANES), 0)
        c = lax.broadcasted_iota(jnp.int32, (LANES, LANES), 1)
        same_half = jnp.where((r < d) == (c < d), 1.0, 0.0).astype(BF16)
        half_sums = _dot(sq.astype(BF16), same_half)
        return jnp.max(jnp.max(half_sums, axis=0, keepdims=True), axis=1, keepdims=True)

    @pl.when(i == 0)
    def _():
        k2max_ref[...] = jnp.broadcast_to(max_sq_norm(k_ref[...]), k2max_ref.shape)

    qk_bound = jnp.sqrt(max_sq_norm(q) * k2max_ref[0:1, 0:1]) * NORM_SLACK

    def kv_steps(blocks, live=None):
        scores, values, offs = [], [], []
        m_low = None
        for j, span, diag in blocks:
            start = pl.multiple_of(j * tk, tk)
            keys = span * tk
            ka = jnp.concatenate([k_ref[pl.ds(start, keys), :], kcols[:keys]], axis=1)
            rg = ATTN_ROW_GROUP if diag is None else ATTN_DIAG_GROUP
            groups = range(0, 2 * tq, rg)
            widths = [keys if diag is None else keys - tk + (r0 % tq) + rg for r0 in groups]
            scores.append([(r0, _dot_nt(qa_ref[r0:r0 + rg, :], ka[:w])) for r0, w in zip(groups, widths)])
            values.append(jnp.concatenate([v_ref[pl.ds(start, keys), :], vcols[:keys]], axis=1))
            offs.append(off_ref[h] * j.astype(F32))
        for idx, ((_, _, diag), block_scores, va, off) in enumerate(zip(blocks, scores, values, offs)):
            for r0, s in block_scores:
                rg, width = s.shape
                rows = slice(r0, r0 + rg)
                if diag is not None:
                    qrow = lax.broadcasted_iota(jnp.int32, (rg, width), 0) + ((r0 % tq) + diag)
                    kcol = lax.broadcasted_iota(jnp.int32, (rg, width), 1)
                    s = jnp.where(kcol <= qrow, s, -jnp.inf)
                elif live is not None:
                    s = jnp.where(live, s, -jnp.inf)
                m_old = m_ref[rows, :]
                m_blk = jnp.broadcast_to(jnp.max(s, axis=-1, keepdims=True), (rg, LANES)) + off
                m_new = jnp.maximum(m_old, m_blk)
                alpha = jnp.exp2(m_old - m_new)
                shift = m_new - off
                p = jnp.concatenate([jnp.exp2((s[:, c0:c0 + LANES] - shift).astype(BF16))
                                     for c0 in range(0, width, LANES)], axis=1)
                acc_ref[rows, :] = (jnp.concatenate([alpha, alpha], axis=1) * acc_ref[rows, :]
                                    + _dot(p, va[:width]))
                m_ref[rows, :] = m_new
                if idx == 0:
                    low = jnp.min(m_new, axis=0, keepdims=True)
                    m_low = low if m_low is None else jnp.minimum(m_low, low)
        return m_low[:, 0:1]

    prev = jnp.maximum(i - 1, 0)
    m_low = kv_steps([(i, 1, 0), (prev, 1, None)], live=i > 0)

    reach = (m_low - SKIP_LOG2 - qk_bound) / off_ref[h]
    j0 = jnp.clip(jnp.floor(reach), 0.0, prev.astype(F32)).astype(jnp.int32)[0, 0]

    def body4(t, carry):
        kv_steps([(j0 + 4 * t, ATTN_SPAN, None), (j0 + 4 * t + 2, ATTN_SPAN, None)])
        return carry

    def body(t, carry):
        kv_steps([(j0 + 2 * t, ATTN_SPAN, None)])
        return carry

    n_past = prev - j0
    n_quads = n_past // 4
    lax.fori_loop(0, n_quads, body4, 0)
    lax.fori_loop(2 * n_quads, n_past // 2, body, 0)

    @pl.when(n_past % 2 == 1)
    def _():
        kv_steps([(prev - 1, 1, None)])

    lam_rows = lam_ref[...]
    lam = (jnp.exp(jnp.sum(lam_rows[0:1] * lam_rows[1:2], axis=-1, keepdims=True))
           - jnp.exp(jnp.sum(lam_rows[2:3] * lam_rows[3:4], axis=-1, keepdims=True)) + lam_init)
    acc1 = acc_ref[0:tq, :]
    acc2 = acc_ref[tq:2 * tq, :]
    o = acc1[:, :LANES] / acc1[:, LANES:] - lam * (acc2[:, :LANES] / acc2[:, LANES:])
    o_ref[...] = (_rms(o, subln_ref[...], SUBLN_EPS) * (1.0 - lam_init)).astype(BF16)


def _attn(slopes, df, lam_rows, subln, batch, seq, lam_init):
    span_keys = ATTN_SPAN * ATTN_K
    assert ATTN_Q == ATTN_K and span_keys <= 256 * POS_LO
    assert (2 * ATTN_Q) % ATTN_ROW_GROUP == 0 and ATTN_Q % ATTN_DIAG_GROUP == 0
    nq = seq // ATTN_Q
    df3 = df.reshape(batch, seq, 3 * DIFF_WIDTH)
    coef = slopes * LOG2E
    c_parts = jnp.stack(_split_bf16(coef, 3), axis=1)
    qcols = jnp.zeros((DIFF_HEADS, SUBLANES, LANES), BF16).at[:, :, 0:6].set(
        jnp.broadcast_to(jnp.tile(c_parts, (1, 2))[:, None, :], (DIFF_HEADS, SUBLANES, 6)))
    pos = np.arange(span_keys)
    kcols_np = np.zeros((span_keys, LANES), np.float32)
    kcols_np[:, 0:3] = (pos // POS_LO * POS_LO)[:, None]
    kcols_np[:, 3:6] = (pos % POS_LO)[:, None]
    kcols = jnp.asarray(kcols_np, BF16)
    vcols = jnp.ones((span_keys, LANES), BF16)
    block_off = coef * ATTN_K

    seq_block = lambda col0: pl.BlockSpec((None, seq, LANES), lambda b, h, i: (b, 0, col0 + h))
    const = lambda shape: pl.BlockSpec(shape, lambda b, h, i: (0,) * len(shape))
    return pl.pallas_call(
        functools.partial(_attn_kernel, lam_init=lam_init),
        grid=(batch, DIFF_HEADS, nq),
        in_specs=[pl.BlockSpec(memory_space=pltpu.SMEM), pl.BlockSpec(memory_space=pltpu.SMEM),
                  pl.BlockSpec((None, ATTN_Q, LANES), lambda b, h, i: (b, i, h)),
                  seq_block(DIFF_HEADS), seq_block(2 * DIFF_HEADS),
                  pl.BlockSpec((1, SUBLANES, LANES), lambda b, h, i: (h, 0, 0)),
                  const((span_keys, LANES)), const((span_keys, LANES)),
                  const((SUBLANES, LANES)), const((1, DIFF_V_DIM))],
        out_specs=pl.BlockSpec((None, ATTN_Q, LANES), lambda b, h, i: (b, i, h)),
        out_shape=jax.ShapeDtypeStruct((batch, seq, DIFF_WIDTH), BF16),
        scratch_shapes=[pltpu.VMEM((2 * ATTN_Q, 2 * LANES), BF16),
                        pltpu.VMEM((2 * ATTN_Q, LANES), F32),
                        pltpu.VMEM((2 * ATTN_Q, 2 * LANES), F32),
                        pltpu.VMEM((SUBLANES, LANES), F32)],
        compiler_params=pltpu.CompilerParams(dimension_semantics=("arbitrary", "arbitrary", "arbitrary"),
                                             vmem_limit_bytes=VMEM_LIMIT_BYTES),
        name="diff_attn",
    )(block_off, coef, df3, df3, df3, qcols, kcols, vcols, lam_rows, subln).reshape(batch * seq, DIFF_WIDTH)


def _merge_ffn_kernel(x_ref, g_ref, oa_ref, ob_ref, wa_ref, wb_ref, wgate_ref, wout_ref,
                      fg_norm_ref, wg_ref, wu_ref, wd_ref, fg_ref, o_ref, *, final_norm):
    x = x_ref[...]
    h = _rms(x, g_ref[...], RMS_EPS).astype(BF16)
    gates = _sigmoid(_dot(h, wgate_ref[...]))
    y_a = _dot(oa_ref[...], wa_ref[...])
    y_b = _dot(ob_ref[...], wb_ref[...])
    mixed = (gates[:, :D_MODEL] * y_a + gates[:, D_MODEL:] * y_b).astype(BF16)
    x = x + _dot(mixed, wout_ref[...])
    o_ref[...] = _ffn_half_step(x, fg_norm_ref, wg_ref, wu_ref, wd_ref, fg_ref, final_norm)


def _merge_ffn(x1, g, o_a, o_b, wa, wb, wgate, wout, ffn_g, wg, wu, wd, fg, final_norm):
    n = x1.shape[0]
    row = lambda w: pl.BlockSpec((MERGE_ROWS, w), lambda i: (i, 0))
    return pl.pallas_call(
        functools.partial(_merge_ffn_kernel, final_norm=final_norm),
        grid=(n // MERGE_ROWS,),
        in_specs=[row(D_MODEL), _resident((1, D_MODEL)), row(DN_WIDTH), row(DIFF_WIDTH),
                  _resident(wa.shape), _resident(wb.shape), _resident(wgate.shape), _resident(wout.shape),
                  _resident((1, D_MODEL)), _resident(wg.shape), _resident(wu.shape), _resident(wd.shape),
                  _resident((1, D_MODEL))],
        out_specs=row(D_MODEL),
        out_shape=jax.ShapeDtypeStruct((n, D_MODEL), F32),
        compiler_params=pltpu.CompilerParams(dimension_semantics=("arbitrary",),
                                             vmem_limit_bytes=VMEM_LIMIT_BYTES),
        name="merge_ffn",
    )(x1, g, o_a, o_b, wa, wb, wgate, wout, ffn_g, wg, wu, wd, fg)


def _lambda_init(layer_idx):
    return 0.8 - 0.6 * math.exp(-0.3 * layer_idx)


def _head_lanes(per_head):
    lanes = jnp.repeat(per_head.astype(F32), DN_HEAD_DIM)
    return jnp.broadcast_to(lanes[None, :], (SUBLANES, DN_WIDTH))


def kernel(x, ffn1_norm, ffn1_w_gate, ffn1_w_up, ffn1_w_down, mix_norm, w_in, conv_qkv, dn_a_log, dn_dt_bias, dn_out_norm, diff_lambda_q1, diff_lambda_k1, diff_lambda_q2, diff_lambda_k2, diff_subln, w_branch_a, w_branch_b, w_out, ffn2_norm, ffn2_w_gate, ffn2_w_up, ffn2_w_down, final_norm):
    batch, seq, _ = x.shape
    depth = ffn1_norm.shape[0]
    n = batch * seq
    xs = x.reshape(n, D_MODEL)
    slopes = jnp.asarray([2.0 ** (-8.0 * (i + 1) / DIFF_HEADS) for i in range(DIFF_HEADS)], F32)
    final_g = final_norm.reshape(1, D_MODEL)

    for l in range(depth):
        (f1_gate, f1_up, f1_down, f2_gate, f2_up, f2_down, wa, wb, wout) = _to_bf16(
            ffn1_w_gate[l], ffn1_w_up[l], ffn1_w_down[l], ffn2_w_gate[l], ffn2_w_up[l], ffn2_w_down[l],
            w_branch_a[l], w_branch_b[l], w_out[l])
        wqkv, wz, wab, wdfq, wdfkv, wgates = _split_w_in(w_in, l)

        xs = _ffn(xs, ffn1_norm[l].reshape(1, D_MODEL), f1_gate, f1_up, f1_down, final_g, final_norm=False)

        qkv, z, ab, df = _proj(xs, mix_norm[l].reshape(1, D_MODEL), wqkv, wz, wab, wdfq, wdfkv, conv_qkv[l], seq)

        o_a = _gdn(qkv, ab, z, _head_lanes(dn_a_log[l]), _head_lanes(dn_dt_bias[l]),
                   jnp.broadcast_to(jnp.tile(dn_out_norm[l].astype(F32), DN_HEADS)[None, :], (SUBLANES, DN_WIDTH)),
                   batch, seq)

        lam_rows = jnp.zeros((SUBLANES, LANES), F32).at[0:4, :DIFF_HEAD_DIM].set(
            jnp.stack([diff_lambda_q1[l], diff_lambda_k1[l], diff_lambda_q2[l], diff_lambda_k2[l]]).astype(F32))
        o_b = _attn(slopes, df, lam_rows, diff_subln[l].reshape(1, DIFF_V_DIM), batch, seq, _lambda_init(l))

        xs = _merge_ffn(xs, mix_norm[l].reshape(1, D_MODEL), o_a, o_b, wa, wb, wgates, wout,
                        ffn2_norm[l].reshape(1, D_MODEL), f2_gate, f2_up, f2_down, final_g,
                        final_norm=(l == depth - 1))

    return xs.reshape(batch, seq, D_MODEL)
```

```python
import functools
import math

import jax
import jax.numpy as jnp
import numpy as np
from jax import lax
from jax.experimental import pallas as pl
from jax.experimental.pallas import tpu as pltpu

F32 = jnp.float32
BF16 = jnp.bfloat16

D_MODEL = 1024
D_FF = 2816
RMS_EPS = 1e-6
SUBLN_EPS = 1e-5
L2_EPS = 1e-6
CONV_WIDTH = 4
DN_HEADS = 8
DN_HEAD_DIM = 64
DN_WIDTH = DN_HEADS * DN_HEAD_DIM
DN_CHUNK = 64
DN_PAIRS = DN_HEADS // 2
DIFF_HEADS = 4
DIFF_HEAD_DIM = 64
DIFF_V_DIM = 2 * DIFF_HEAD_DIM
DIFF_WIDTH = DIFF_HEADS * DIFF_V_DIM
LANES = 128
SUBLANES = 8
LOG2E = math.log2(math.e)

VMEM_LIMIT_BYTES = 56 * 1024 * 1024

FFN_ROWS = 512
FF_CHUNKS = ((0, 512), (512, 1024), (1024, 1536), (1536, 2048), (2048, 2560), (2560, 2816))
PROJ_ROWS = 512
GDN_ROWS = 512
GDN_PREP_CHUNKS = 2
ATTN_Q = 512
ATTN_K = 512
ATTN_SPAN = 2
ATTN_ROW_GROUP = 512
ATTN_DIAG_GROUP = 256
MERGE_ROWS = 512
CAST_STEPS = 8
POS_LO = 16
SKIP_LOG2 = 160.0
NORM_SLACK = 1.01


def _rms(x, g, eps):
    return x * lax.rsqrt(jnp.mean(x * x, axis=-1, keepdims=True) + eps) * g


def _sigmoid(x):
    return 1.0 / (1.0 + jnp.exp(-x))


def _dot(a, b):
    return jnp.dot(a, b, preferred_element_type=F32)


def _dot_nt(a, b):
    return lax.dot_general(a, b, (((1,), (1,)), ((), ())), preferred_element_type=F32)


def _split_bf16(x, terms):
    parts = []
    rest = x
    for t in range(terms):
        part = rest.astype(BF16)
        parts.append(part)
        if t + 1 < terms:
            rest = rest - part.astype(F32)
    return parts


def _resident(shape):
    return pl.BlockSpec(shape, lambda *_: (0,) * len(shape), pipeline_mode=pl.Buffered(1))


def _cast_kernel(*refs):
    half = len(refs) // 2
    for src, dst in zip(refs[:half], refs[half:]):
        dst[...] = src[...].astype(BF16)


def _to_bf16(*arrays):
    specs = [pl.BlockSpec((a.shape[0] // CAST_STEPS, a.shape[1]), lambda i: (i, 0)) for a in arrays]
    return pl.pallas_call(
        _cast_kernel,
        grid=(CAST_STEPS,),
        in_specs=specs,
        out_specs=specs,
        out_shape=[jax.ShapeDtypeStruct(a.shape, BF16) for a in arrays],
        compiler_params=pltpu.CompilerParams(dimension_semantics=("arbitrary",),
                                             vmem_limit_bytes=VMEM_LIMIT_BYTES),
        name="cast_weights",
    )(*arrays)


def _split_w_in_kernel(w_ref, qkv_ref, z_ref, ab_ref, dfq_ref, dfkv_ref, gates_ref):
    c_z = 3 * DN_WIDTH
    c_ab = c_z + DN_WIDTH
    c_dfq = c_ab + 2 * DN_HEADS
    c_dfkv = c_dfq + DIFF_WIDTH
    c_gates = c_dfkv + 2 * DIFF_WIDTH
    w = w_ref[...]
    qkv_ref[...] = w[:, :c_z].astype(BF16)
    z_ref[...] = w[:, c_z:c_ab].astype(BF16)
    lane = lax.broadcasted_iota(jnp.int32, (w.shape[0], LANES), 1)
    ab_ref[...] = jnp.where(lane < 2 * DN_HEADS, w[:, c_ab:c_ab + LANES], 0.0).astype(BF16)
    dfq_ref[...] = w[:, c_dfq:c_dfkv].astype(BF16)
    dfkv_ref[...] = w[:, c_dfkv:c_gates].astype(BF16)
    gates_ref[...] = w[:, c_gates:c_gates + 2 * D_MODEL].astype(BF16)


def _split_w_in(w, layer):
    rows = w.shape[1] // CAST_STEPS
    widths = (3 * DN_WIDTH, DN_WIDTH, LANES, DIFF_WIDTH, 2 * DIFF_WIDTH, 2 * D_MODEL)
    return pl.pallas_call(
        _split_w_in_kernel,
        grid=(CAST_STEPS,),
        in_specs=[pl.BlockSpec((None, rows, w.shape[2]), lambda i: (layer, i, 0))],
        out_specs=[pl.BlockSpec((rows, c), lambda i: (i, 0)) for c in widths],
        out_shape=[jax.ShapeDtypeStruct((w.shape[1], c), BF16) for c in widths],
        compiler_params=pltpu.CompilerParams(dimension_semantics=("arbitrary",),
                                             vmem_limit_bytes=VMEM_LIMIT_BYTES),
        name="split_w_in",
    )(w)


def _ffn_half_step(x, g_ref, wg_ref, wu_ref, wd_ref, fg_ref, final_norm):
    h = _rms(x, g_ref[...], RMS_EPS).astype(BF16)
    acc = None
    for c0, c1 in FF_CHUNKS:
        gate = _dot(h, wg_ref[:, c0:c1])
        up = _dot(h, wu_ref[:, c0:c1])
        act = (gate * _sigmoid(gate) * up).astype(BF16)
        part = _dot(act, wd_ref[c0:c1, :])
        acc = part if acc is None else acc + part
    y = x + 0.5 * acc
    if final_norm:
        y = _rms(y, fg_ref[...], RMS_EPS)
    return y


def _ffn_kernel(x_ref, g_ref, wg_ref, wu_ref, wd_ref, fg_ref, o_ref, *, final_norm):
    o_ref[...] = _ffn_half_step(x_ref[...], g_ref, wg_ref, wu_ref, wd_ref, fg_ref, final_norm)


def _ffn(x, g, wg, wu, wd, fg, final_norm):
    n = x.shape[0]
    row = pl.BlockSpec((FFN_ROWS, D_MODEL), lambda i: (i, 0))
    return pl.pallas_call(
        functools.partial(_ffn_kernel, final_norm=final_norm),
        grid=(n // FFN_ROWS,),
        in_specs=[row, _resident((1, D_MODEL)), _resident((D_MODEL, D_FF)), _resident((D_MODEL, D_FF)),
                  _resident((D_FF, D_MODEL)), _resident((1, D_MODEL))],
        out_specs=row,
        out_shape=jax.ShapeDtypeStruct((n, D_MODEL), F32),
        compiler_params=pltpu.CompilerParams(dimension_semantics=("arbitrary",),
                                             vmem_limit_bytes=VMEM_LIMIT_BYTES),
        name="ffn",
    )(x, g, wg, wu, wd, fg)


def _proj_kernel(x_ref, g_ref, wqkv_ref, wz_ref, wab_ref, wdfq_ref, wdfkv_ref, cw_ref,
                 qkv_ref, z_ref, ab_ref, df_ref, ext_ref, *, tiles_per_seq):
    halo = SUBLANES

    @pl.when(pl.program_id(0) % tiles_per_seq == 0)
    def _():
        ext_ref[0:halo, :] = jnp.zeros((halo, 3 * DN_WIDTH), F32)

    h = _rms(x_ref[...], g_ref[...], RMS_EPS).astype(BF16)
    pre = _dot(h, wqkv_ref[...])
    ext_ref[halo:halo + PROJ_ROWS, :] = pre
    cw = cw_ref[...]
    y = cw[CONV_WIDTH - 1:CONV_WIDTH, :] * pre
    for j in range(CONV_WIDTH - 1):
        y = y + cw[j:j + 1, :] * ext_ref[pl.ds(halo - (CONV_WIDTH - 1) + j, PROJ_ROWS), :]
    ext_ref[0:halo, :] = ext_ref[PROJ_ROWS:PROJ_ROWS + halo, :]
    qkv_ref[...] = (y * _sigmoid(y)).astype(BF16)
    z_ref[...] = _dot(h, wz_ref[...]).astype(BF16)
    ab_ref[...] = _dot(h, wab_ref[...])
    df_ref[:, :DIFF_WIDTH] = (_dot(h, wdfq_ref[...]) * (DIFF_HEAD_DIM ** -0.5 * LOG2E)).astype(BF16)
    df_ref[:, DIFF_WIDTH:] = _dot(h, wdfkv_ref[...]).astype(BF16)


def _proj(x1, g, wqkv, wz, wab, wdfq, wdfkv, cw, seq):
    n = x1.shape[0]
    row = lambda w: pl.BlockSpec((PROJ_ROWS, w), lambda i: (i, 0))
    return pl.pallas_call(
        functools.partial(_proj_kernel, tiles_per_seq=seq // PROJ_ROWS),
        grid=(n // PROJ_ROWS,),
        in_specs=[row(D_MODEL), _resident((1, D_MODEL)), _resident(wqkv.shape), _resident(wz.shape),
                  _resident(wab.shape), _resident(wdfq.shape), _resident(wdfkv.shape), _resident(cw.shape)],
        out_specs=[row(3 * DN_WIDTH), row(DN_WIDTH), row(LANES), row(3 * DIFF_WIDTH)],
        out_shape=[jax.ShapeDtypeStruct((n, 3 * DN_WIDTH), BF16),
                   jax.ShapeDtypeStruct((n, DN_WIDTH), BF16),
                   jax.ShapeDtypeStruct((n, LANES), F32),
                   jax.ShapeDtypeStruct((n, 3 * DIFF_WIDTH), BF16)],
        scratch_shapes=[pltpu.VMEM((PROJ_ROWS + SUBLANES, 3 * DN_WIDTH), F32)],
        compiler_params=pltpu.CompilerParams(dimension_semantics=("arbitrary",),
                                             vmem_limit_bytes=VMEM_LIMIT_BYTES),
        name="proj",
    )(x1, g, wqkv, wz, wab, wdfq, wdfkv, cw)


def _gdn_kernel(q_ref, k_ref, v_ref, ab_ref, z_ref, alog_ref, dtb_ref, onorm_ref, o_ref, s_ref, *, batch):
    c = DN_CHUNK
    hd = DN_HEAD_DIM
    units = [(b, p) for b in range(batch) for p in range(DN_PAIRS)]

    @pl.when(pl.program_id(0) == 0)
    def _():
        s_ref[...] = jnp.zeros(s_ref.shape, F32)

    row = lax.broadcasted_iota(jnp.int32, (c, LANES), 0)
    lane = lax.broadcasted_iota(jnp.int32, (c, LANES), 1)
    col = lane & (hd - 1)
    even_bf = jnp.where(lane < hd, 1.0, 0.0).astype(BF16)
    odd_bf = jnp.where(lane < hd, 0.0, 1.0).astype(BF16)
    incl = col <= row
    strict = col < row
    same16 = (row >> 4) == (col >> 4)
    same32 = (row >> 5) == (col >> 5)
    eye = jnp.where(col == row, 1.0, 0.0).astype(F32)
    r2 = lax.broadcasted_iota(jnp.int32, (LANES, LANES), 0)
    l2 = lax.broadcasted_iota(jnp.int32, (LANES, LANES), 1)
    same_head = (r2 >> 6) == (l2 >> 6)
    head_ones = jnp.where(same_head, 1.0, 0.0).astype(BF16)
    head_ones2 = jnp.concatenate([head_ones, head_ones], axis=0)
    r3 = lax.broadcasted_iota(jnp.int32, (3 * LANES, LANES), 0) & (LANES - 1)
    l3 = lax.broadcasted_iota(jnp.int32, (3 * LANES, LANES), 1)
    rc = lax.broadcasted_iota(jnp.int32, (c, 3 * c), 0)
    cc = lax.broadcasted_iota(jnp.int32, (c, 3 * c), 1) & (c - 1)
    lower3 = jnp.where(cc <= rc, 1.0, 0.0).astype(BF16)
    ones3 = jnp.ones((c, 3 * c), BF16)

    def stack(x):
        xb = x.astype(BF16)
        return jnp.concatenate([xb * even_bf, xb * odd_bf], axis=0)

    def mm(a, b):
        return _dot(a.astype(BF16), b.astype(BF16))

    def mm_nt(a, b):
        return _dot_nt(a.astype(BF16), b.astype(BF16))

    def head_sums(x):
        return _dot(jnp.concatenate(_split_bf16(x, 2), axis=1), head_ones2)

    def exact_rows(lhs3, x):
        return _dot(lhs3, jnp.concatenate(_split_bf16(x, 3), axis=0))

    ab3 = [jnp.concatenate(_split_bf16(ab_ref[b], 3), axis=1) for b in range(batch)]
    q_t, k_t, v_t, g_t, beta_t = [], [], [], [], []
    for b, p in units:
        lanes = slice(p * LANES, (p + 1) * LANES)
        expand_a = jnp.where(r3 == 2 * p + (l3 >> 6), 1.0, 0.0).astype(BF16)
        expand_b = jnp.where(r3 == DN_HEADS + 2 * p + (l3 >> 6), 1.0, 0.0).astype(BF16)
        sp_in = _dot(ab3[b], expand_a) + dtb_ref[0:1, lanes]
        softplus = jnp.maximum(sp_in, 0.0) + jnp.log(1.0 + jnp.exp(-jnp.abs(sp_in)))
        g_t.append(-jnp.exp(alog_ref[0:1, lanes]) * softplus)
        beta_t.append(_sigmoid(_dot(ab3[b], expand_b)))
        q = q_ref[b, :, lanes].astype(F32)
        k = k_ref[b, :, lanes].astype(F32)
        q_t.append(q * lax.rsqrt(head_sums(q * q) + L2_EPS) * (hd ** -0.5))
        k_t.append(k * lax.rsqrt(head_sums(k * k) + L2_EPS))
        v_t.append(v_ref[b, :, lanes].astype(F32))

    states = [s_ref[i] for i in range(len(units))]
    outs = [[] for _ in units]
    for n0 in range(0, GDN_ROWS // c, GDN_PREP_CHUNKS):
        items = [(u, n) for n in range(n0, n0 + GDN_PREP_CHUNKS) for u in range(len(units))]

        def take(per_unit):
            return [per_unit[u][n * c:(n + 1) * c] for u, n in items]

        q = take(q_t)
        k = take(k_t)
        v = take(v_t)
        beta = take(beta_t)
        gc = [exact_rows(lower3, x) for x in take(g_t)]
        gc_row = [exact_rows(ones3, x * eye) for x in gc]
        decay = [jnp.where(incl, jnp.exp(jnp.where(incl, a - b_, 0.0)), 0.0) for a, b_ in zip(gc, gc_row)]
        exp_gc = [jnp.exp(x) for x in gc]
        kb = [a * b_ for a, b_ in zip(k, beta)]
        kq = [mm_nt(jnp.concatenate([a, b_], axis=0), stack(k_)) for a, b_, k_ in zip(kb, q, k)]
        a_mat = [jnp.where(strict, x[:c] * d, 0.0) for x, d in zip(kq, decay)]
        attn = [x[c:] * d for x, d in zip(kq, decay)]

        neg_d = [jnp.where(strict & same16, -x, 0.0) for x in a_mat]
        t_inv = [eye + x for x in neg_d]
        power = [mm(x, stack(x)) for x in neg_d]
        for _ in range(2):
            both = [mm(jnp.concatenate([pw, t], axis=0), stack(pw)) for pw, t in zip(power, t_inv)]
            power = [x[:c] for x in both]
            t_inv = [t + x[c:] for t, x in zip(t_inv, both)]
        t_inv = [t + mm(t, stack(pw)) for t, pw in zip(t_inv, power)]
        for block in (strict & same32 & ~same16, strict & ~same32):
            left = [mm(t, stack(jnp.where(block, x, 0.0))) for t, x in zip(t_inv, a_mat)]
            t_inv = [t - mm(x, stack(t)) for t, x in zip(t_inv, left)]

        uw = [mm(t, jnp.concatenate([stack(v_ * b_), stack(kb_ * e)], axis=1))
              for t, v_, b_, kb_, e in zip(t_inv, v, beta, kb, exp_gc)]
        q_dec = [a * e for a, e in zip(q, exp_gc)]
        gc_last = [x[c - 1:c, :] for x in gc]
        k_dec_t = [(k_ * jnp.exp(gl - g_)).T for k_, gl, g_ in zip(k, gc_last, gc)]

        for first in range(0, len(items), len(units)):
            sel = slice(first, first + len(units))
            ws = [mm(jnp.concatenate([x[:, LANES:], qd], axis=0), st)
                  for x, qd, st in zip(uw[sel], q_dec[sel], states)]
            v_new = [x[:, :LANES] - y[:c] for x, y in zip(uw[sel], ws)]
            o_n = [y[c:] + mm(a, stack(vn)) for y, a, vn in zip(ws, attn[sel], v_new)]
            for lst, x in zip(outs, o_n):
                lst.append(x)
            states = [st * jnp.exp(gl) + jnp.where(same_head, mm(kt, vn), 0.0)
                      for st, gl, kt, vn in zip(states, gc_last[sel], k_dec_t[sel], v_new)]

    for i, (b, p) in enumerate(units):
        lanes = slice(p * LANES, (p + 1) * LANES)
        s_ref[i] = states[i]
        o = jnp.concatenate(outs[i], axis=0)
        ms = head_sums(o * o) * (1.0 / hd)
        z = z_ref[b, :, lanes].astype(F32)
        o_ref[b, :, lanes] = (o * lax.rsqrt(ms + RMS_EPS) * onorm_ref[0:1, lanes] * (z * _sigmoid(z))).astype(BF16)


def _gdn(qkv, ab, z, alog_lanes, dtb_lanes, onorm_lanes, batch, seq):
    qkv3 = qkv.reshape(batch, seq, 3 * DN_WIDTH)

    def tile(width, col):
        return pl.BlockSpec((batch, GDN_ROWS, width), lambda t: (0, t, col))

    return pl.pallas_call(
        functools.partial(_gdn_kernel, batch=batch),
        grid=(seq // GDN_ROWS,),
        in_specs=[tile(DN_WIDTH, 0), tile(DN_WIDTH, 1), tile(DN_WIDTH, 2), tile(LANES, 0), tile(DN_WIDTH, 0),
                  _resident(alog_lanes.shape), _resident(dtb_lanes.shape), _resident(onorm_lanes.shape)],
        out_specs=tile(DN_WIDTH, 0),
        out_shape=jax.ShapeDtypeStruct((batch, seq, DN_WIDTH), BF16),
        scratch_shapes=[pltpu.VMEM((batch * DN_PAIRS, LANES, LANES), F32)],
        compiler_params=pltpu.CompilerParams(dimension_semantics=("arbitrary",),
                                             vmem_limit_bytes=VMEM_LIMIT_BYTES),
        name="gdn",
    )(qkv3, qkv3, qkv3, ab.reshape(batch, seq, LANES), z.reshape(batch, seq, DN_WIDTH),
      alog_lanes, dtb_lanes, onorm_lanes).reshape(batch * seq, DN_WIDTH)


def _attn_kernel(off_ref, coef_ref, q_ref, k_ref, v_ref, qcols_ref, kcols_ref, vcols_ref, lam_ref, subln_ref,
                 o_ref, qa_ref, m_ref, acc_ref, k2max_ref, *, lam_init):
    h = pl.program_id(1)
    i = pl.program_id(2)
    d = DIFF_HEAD_DIM
    tq = ATTN_Q
    tk = ATTN_K

    q = q_ref[...]
    first = lax.broadcasted_iota(jnp.int32, q.shape, 1) < d
    qcols = jnp.broadcast_to(qcols_ref[0, 0:1, :], (tq, LANES))
    qa_ref[0:tq, :] = jnp.concatenate([jnp.where(first, q, 0), qcols], axis=1)
    qa_ref[tq:2 * tq, :] = jnp.concatenate([jnp.where(first, 0, q), qcols], axis=1)
    m_ref[...] = jnp.full(m_ref.shape, -jnp.inf, F32)
    acc_ref[...] = jnp.zeros(acc_ref.shape, F32)
    kcols = kcols_ref[...]
    vcols = vcols_ref[...]

    def max_sq_norm(x):
        sq = x.astype(F32) * x.astype(F32)
        r = lax.broadcasted_iota(jnp.int32, (LANES, LANES), 0)
        c = lax.broadcasted_iota(jnp.int32, (LANES, LANES), 1)
        same_half = jnp.where((r < d) == (c < d), 1.0, 0.0).astype(BF16)
        half_sums = _dot(sq.astype(BF16), same_half)
        return jnp.max(jnp.max(half_sums, axis=0, keepdims=True), axis=1, keepdims=True)

    @pl.when(i == 0)
    def _():
        k2max_ref[...] = jnp.broadcast_to(max_sq_norm(k_ref[...]), k2max_ref.shape)

    qk_bound = jnp.sqrt(max_sq_norm(q) * k2max_ref[0:1, 0:1]) * NORM_SLACK

    def kv_steps(blocks, live=None):
        scores, values, offs = [], [], []
        m_low = None
        for j, span, diag in blocks:
            start = pl.multiple_of(j * tk, tk)
            keys = span * tk
            ka = jnp.concatenate([k_ref[pl.ds(start, keys), :], kcols[:keys]], axis=1)
            rg = ATTN_ROW_GROUP if diag is None else ATTN_DIAG_GROUP
            groups = range(0, 2 * tq, rg)
            widths = [keys if diag is None else keys - tk + (r0 % tq) + rg for r0 in groups]
            scores.append([(r0, _dot_nt(qa_ref[r0:r0 + rg, :], ka[:w])) for r0, w in zip(groups, widths)])
            values.append(jnp.concatenate([v_ref[pl.ds(start, keys), :], vcols[:keys]], axis=1))
            offs.append(off_ref[h] * j.astype(F32))
        for idx, ((_, _, diag), block_scores, va, off) in enumerate(zip(blocks, scores, values, offs)):
            for r0, s in block_scores:
                rg, width = s.shape
                rows = slice(r0, r0 + rg)
                if diag is not None:
                    qrow = lax.broadcasted_iota(jnp.int32, (rg, width), 0) + ((r0 % tq) + diag)
                    kcol = lax.broadcasted_iota(jnp.int32, (rg, width), 1)
                    s = jnp.where(kcol <= qrow, s, -jnp.inf)
                elif live is not None:
                    s = jnp.where(live, s, -jnp.inf)
                m_old = m_ref[rows, :]
                m_blk = jnp.broadcast_to(jnp.max(s, axis=-1, keepdims=True), (rg, LANES)) + off
                m_new = jnp.maximum(m_old, m_blk)
                alpha = jnp.exp2(m_old - m_new)
                shift = m_new - off
                p = jnp.concatenate([jnp.exp2((s[:, c0:c0 + LANES] - shift).astype(BF16))
                                     for c0 in range(0, width, LANES)], axis=1)
                acc_ref[rows, :] = (jnp.concatenate([alpha, alpha], axis=1) * acc_ref[rows, :]
                                    + _dot(p, va[:width]))
                m_ref[rows, :] = m_new
                if idx == 0:
                    low = jnp.min(m_new, axis=0, keepdims=True)
                    m_low = low if m_low is None else jnp.minimum(m_low, low)
        return m_low[:, 0:1]

    prev = jnp.maximum(i - 1, 0)
    m_low = kv_steps([(i, 1, 0), (prev, 1, None)], live=i > 0)

    reach = (m_low - SKIP_LOG2 - qk_bound) / off_ref[h]
    j0 = jnp.clip(jnp.floor(reach), 0.0, prev.astype(F32)).astype(jnp.int32)[0, 0]

    def body4(t, carry):
        kv_steps([(j0 + 4 * t, ATTN_SPAN, None), (j0 + 4 * t + 2, ATTN_SPAN, None)])
        return carry

    def body(t, carry):
        kv_steps([(j0 + 2 * t, ATTN_SPAN, None)])
        return carry

    n_past = prev - j0
    n_quads = n_past // 4
    lax.fori_loop(0, n_quads, body4, 0)
    lax.fori_loop(2 * n_quads, n_past // 2, body, 0)

    @pl.when(n_past % 2 == 1)
    def _():
        kv_steps([(prev - 1, 1, None)])

    lam_rows = lam_ref[...]
    lam = (jnp.exp(jnp.sum(lam_rows[0:1] * lam_rows[1:2], axis=-1, keepdims=True))
           - jnp.exp(jnp.sum(lam_rows[2:3] * lam_rows[3:4], axis=-1, keepdims=True)) + lam_init)
    acc1 = acc_ref[0:tq, :]
    acc2 = acc_ref[tq:2 * tq, :]
    o = acc1[:, :LANES] / acc1[:, LANES:] - lam * (acc2[:, :LANES] / acc2[:, LANES:])
    o_ref[...] = (_rms(o, subln_ref[...], SUBLN_EPS) * (1.0 - lam_init)).astype(BF16)


def _attn(slopes, df, lam_rows, subln, batch, seq, lam_init):
    span_keys = ATTN_SPAN * ATTN_K
    assert ATTN_Q == ATTN_K and span_keys <= 256 * POS_LO
    assert (2 * ATTN_Q) % ATTN_ROW_GROUP == 0 and ATTN_Q % ATTN_DIAG_GROUP == 0
    nq = seq // ATTN_Q
    df3 = df.reshape(batch, seq, 3 * DIFF_WIDTH)
    coef = slopes * LOG2E
    c_parts = jnp.stack(_split_bf16(coef, 3), axis=1)
    qcols = jnp.zeros((DIFF_HEADS, SUBLANES, LANES), BF16).at[:, :, 0:6].set(
        jnp.broadcast_to(jnp.tile(c_parts, (1, 2))[:, None, :], (DIFF_HEADS, SUBLANES, 6)))
    pos = np.arange(span_keys)
    kcols_np = np.zeros((span_keys, LANES), np.float32)
    kcols_np[:, 0:3] = (pos // POS_LO * POS_LO)[:, None]
    kcols_np[:, 3:6] = (pos % POS_LO)[:, None]
    kcols = jnp.asarray(kcols_np, BF16)
    vcols = jnp.ones((span_keys, LANES), BF16)
    block_off = coef * ATTN_K

    seq_block = lambda col0: pl.BlockSpec((None, seq, LANES), lambda b, h, i: (b, 0, col0 + h))
    const = lambda shape: pl.BlockSpec(shape, lambda b, h, i: (0,) * len(shape))
    return pl.pallas_call(
        functools.partial(_attn_kernel, lam_init=lam_init),
        grid=(batch, DIFF_HEADS, nq),
        in_specs=[pl.BlockSpec(memory_space=pltpu.SMEM), pl.BlockSpec(memory_space=pltpu.SMEM),
                  pl.BlockSpec((None, ATTN_Q, LANES), lambda b, h, i: (b, i, h)),
                  seq_block(DIFF_HEADS), seq_block(2 * DIFF_HEADS),
                  pl.BlockSpec((1, SUBLANES, LANES), lambda b, h, i: (h, 0, 0)),
                  const((span_keys, LANES)), const((span_keys, LANES)),
                  const((SUBLANES, LANES)), const((1, DIFF_V_DIM))],
        out_specs=pl.BlockSpec((None, ATTN_Q, LANES), lambda b, h, i: (b, i, h)),
        out_shape=jax.ShapeDtypeStruct((batch, seq, DIFF_WIDTH), BF16),
        scratch_shapes=[pltpu.VMEM((2 * ATTN_Q, 2 * LANES), BF16),
                        pltpu.VMEM((2 * ATTN_Q, LANES), F32),
                        pltpu.VMEM((2 * ATTN_Q, 2 * LANES), F32),
                        pltpu.VMEM((SUBLANES, LANES), F32)],
        compiler_params=pltpu.CompilerParams(dimension_semantics=("arbitrary", "arbitrary", "arbitrary"),
                                             vmem_limit_bytes=VMEM_LIMIT_BYTES),
        name="diff_attn",
    )(block_off, coef, df3, df3, df3, qcols, kcols, vcols, lam_rows, subln).reshape(batch * seq, DIFF_WIDTH)


def _merge_ffn_kernel(x_ref, g_ref, oa_ref, ob_ref, wa_ref, wb_ref, wgate_ref, wout_ref,
                      fg_norm_ref, wg_ref, wu_ref, wd_ref, fg_ref, o_ref, *, final_norm):
    x = x_ref[...]
    h = _rms(x, g_ref[...], RMS_EPS).astype(BF16)
    gates = _sigmoid(_dot(h, wgate_ref[...]))
    y_a = _dot(oa_ref[...], wa_ref[...])
    y_b = _dot(ob_ref[...], wb_ref[...])
    mixed = (gates[:, :D_MODEL] * y_a + gates[:, D_MODEL:] * y_b).astype(BF16)
    x = x + _dot(mixed, wout_ref[...])
    o_ref[...] = _ffn_half_step(x, fg_norm_ref, wg_ref, wu_ref, wd_ref, fg_ref, final_norm)


def _merge_ffn(x1, g, o_a, o_b, wa, wb, wgate, wout, ffn_g, wg, wu, wd, fg, final_norm):
    n = x1.shape[0]
    row = lambda w: pl.BlockSpec((MERGE_ROWS, w), lambda i: (i, 0))
    return pl.pallas_call(
        functools.partial(_merge_ffn_kernel, final_norm=final_norm),
        grid=(n // MERGE_ROWS,),
        in_specs=[row(D_MODEL), _resident((1, D_MODEL)), row(DN_WIDTH), row(DIFF_WIDTH),
                  _resident(wa.shape), _resident(wb.shape), _resident(wgate.shape), _resident(wout.shape),
                  _resident((1, D_MODEL)), _resident(wg.shape), _resident(wu.shape), _resident(wd.shape),
                  _resident((1, D_MODEL))],
        out_specs=row(D_MODEL),
        out_shape=jax.ShapeDtypeStruct((n, D_MODEL), F32),
        compiler_params=pltpu.CompilerParams(dimension_semantics=("arbitrary",),
                                             vmem_limit_bytes=VMEM_LIMIT_BYTES),
        name="merge_ffn",
    )(x1, g, o_a, o_b, wa, wb, wgate, wout, ffn_g, wg, wu, wd, fg)


def _lambda_init(layer_idx):
    return 0.8 - 0.6 * math.exp(-0.3 * layer_idx)


def _head_lanes(per_head):
    lanes = jnp.repeat(per_head.astype(F32), DN_HEAD_DIM)
    return jnp.broadcast_to(lanes[None, :], (SUBLANES, DN_WIDTH))


def kernel(x, ffn1_norm, ffn1_w_gate, ffn1_w_up, ffn1_w_down, mix_norm, w_in, conv_qkv, dn_a_log, dn_dt_bias, dn_out_norm, diff_lambda_q1, diff_lambda_k1, diff_lambda_q2, diff_lambda_k2, diff_subln, w_branch_a, w_branch_b, w_out, ffn2_norm, ffn2_w_gate, ffn2_w_up, ffn2_w_down, final_norm):
    batch, seq, _ = x.shape
    depth = ffn1_norm.shape[0]
    n = batch * seq
    xs = x.reshape(n, D_MODEL)
    slopes = jnp.asarray([2.0 ** (-8.0 * (i + 1) / DIFF_HEADS) for i in range(DIFF_HEADS)], F32)
    final_g = final_norm.reshape(1, D_MODEL)

    for l in range(depth):
        (f1_gate, f1_up, f1_down, f2_gate, f2_up, f2_down, wa, wb, wout) = _to_bf16(
            ffn1_w_gate[l], ffn1_w_up[l], ffn1_w_down[l], ffn2_w_gate[l], ffn2_w_up[l], ffn2_w_down[l],
            w_branch_a[l], w_branch_b[l], w_out[l])
        wqkv, wz, wab, wdfq, wdfkv, wgates = _split_w_in(w_in, l)

        xs = _ffn(xs, ffn1_norm[l].reshape(1, D_MODEL), f1_gate, f1_up, f1_down, final_g, final_norm=False)

        qkv, z, ab, df = _proj(xs, mix_norm[l].reshape(1, D_MODEL), wqkv, wz, wab, wdfq, wdfkv, conv_qkv[l], seq)

        o_a = _gdn(qkv, ab, z, _head_lanes(dn_a_log[l]), _head_lanes(dn_dt_bias[l]),
                   jnp.broadcast_to(jnp.tile(dn_out_norm[l].astype(F32), DN_HEADS)[None, :], (SUBLANES, DN_WIDTH)),
                   batch, seq)

        lam_rows = jnp.zeros((SUBLANES, LANES), F32).at[0:4, :DIFF_HEAD_DIM].set(
            jnp.stack([diff_lambda_q1[l], diff_lambda_k1[l], diff_lambda_q2[l], diff_lambda_k2[l]]).astype(F32))
        o_b = _attn(slopes, df, lam_rows, diff_subln[l].reshape(1, DIFF_V_DIM), batch, seq, _lambda_init(l))

        xs = _merge_ffn(xs, mix_norm[l].reshape(1, D_MODEL), o_a, o_b, wa, wb, wgates, wout,
                        ffn2_norm[l].reshape(1, D_MODEL), f2_gate, f2_up, f2_down, final_g,
                        final_norm=(l == depth - 1))

    return xs.reshape(batch, seq, D_MODEL)
```

```python
import functools
import math

import jax
import jax.numpy as jnp
import numpy as np
from jax import lax
from jax.experimental import pallas as pl
from jax.experimental.pallas import tpu as pltpu

F32 = jnp.float32
BF16 = jnp.bfloat16

D_MODEL = 1024
D_FF = 2816
RMS_EPS = 1e-6
SUBLN_EPS = 1e-5
L2_EPS = 1e-6
CONV_WIDTH = 4
DN_HEADS = 8
DN_HEAD_DIM = 64
DN_WIDTH = DN_HEADS * DN_HEAD_DIM
DN_CHUNK = 64
DN_PAIRS = DN_HEADS // 2
DIFF_HEADS = 4
DIFF_HEAD_DIM = 64
DIFF_V_DIM = 2 * DIFF_HEAD_DIM
DIFF_WIDTH = DIFF_HEADS * DIFF_V_DIM
LANES = 128
SUBLANES = 8
LOG2E = math.log2(math.e)

VMEM_LIMIT_BYTES = 56 * 1024 * 1024

FFN_ROWS = 512
FF_CHUNKS = ((0, 512), (512, 1024), (1024, 1536), (1536, 2048), (2048, 2560), (2560, 2816))
PROJ_ROWS = 512
GDN_ROWS = 512
GDN_PREP_CHUNKS = 2
ATTN_Q = 512
ATTN_K = 512
ATTN_SPAN = 2
ATTN_ROW_GROUP = 512
ATTN_DIAG_GROUP = 256
MERGE_ROWS = 512
CAST_STEPS = 8
POS_LO = 16
SKIP_LOG2 = 160.0
NORM_SLACK = 1.01


def _rms(x, g, eps):
    return x * lax.rsqrt(jnp.mean(x * x, axis=-1, keepdims=True) + eps) * g


def _sigmoid(x):
    return 1.0 / (1.0 + jnp.exp(-x))


def _dot(a, b):
    return jnp.dot(a, b, preferred_element_type=F32)


def _dot_nt(a, b):
    return lax.dot_general(a, b, (((1,), (1,)), ((), ())), preferred_element_type=F32)


def _split_bf16(x, terms):
    parts = []
    rest = x
    for t in range(terms):
        part = rest.astype(BF16)
        parts.append(part)
        if t + 1 < terms:
            rest = rest - part.astype(F32)
    return parts


def _resident(shape):
    return pl.BlockSpec(shape, lambda *_: (0,) * len(shape), pipeline_mode=pl.Buffered(1))


def _cast_kernel(*refs):
    half = len(refs) // 2
    for src, dst in zip(refs[:half], refs[half:]):
        dst[...] = src[...].astype(BF16)


def _to_bf16(*arrays):
    specs = [pl.BlockSpec((a.shape[0] // CAST_STEPS, a.shape[1]), lambda i: (i, 0)) for a in arrays]
    return pl.pallas_call(
        _cast_kernel,
        grid=(CAST_STEPS,),
        in_specs=specs,
        out_specs=specs,
        out_shape=[jax.ShapeDtypeStruct(a.shape, BF16) for a in arrays],
        compiler_params=pltpu.CompilerParams(dimension_semantics=("arbitrary",),
                                             vmem_limit_bytes=VMEM_LIMIT_BYTES),
        name="cast_weights",
    )(*arrays)


def _split_w_in_kernel(w_ref, qkv_ref, z_ref, ab_ref, dfq_ref, dfkv_ref, gates_ref):
    c_z = 3 * DN_WIDTH
    c_ab = c_z + DN_WIDTH
    c_dfq = c_ab + 2 * DN_HEADS
    c_dfkv = c_dfq + DIFF_WIDTH
    c_gates = c_dfkv + 2 * DIFF_WIDTH
    w = w_ref[...]
    qkv_ref[...] = w[:, :c_z].astype(BF16)
    z_ref[...] = w[:, c_z:c_ab].astype(BF16)
    lane = lax.broadcasted_iota(jnp.int32, (w.shape[0], LANES), 1)
    ab_ref[...] = jnp.where(lane < 2 * DN_HEADS, w[:, c_ab:c_ab + LANES], 0.0).astype(BF16)
    dfq_ref[...] = w[:, c_dfq:c_dfkv].astype(BF16)
    dfkv_ref[...] = w[:, c_dfkv:c_gates].astype(BF16)
    gates_ref[...] = w[:, c_gates:c_gates + 2 * D_MODEL].astype(BF16)


def _ffn_half_step(x, g_ref, wg_ref, wu_ref, wd_ref, fg_ref, final_norm):
    h = _rms(x, g_ref[...], RMS_EPS).astype(BF16)
    acc = None
    for c0, c1 in FF_CHUNKS:
        gate = _dot(h, wg_ref[:, c0:c1])
        up = _dot(h, wu_ref[:, c0:c1])
        act = (gate * _sigmoid(gate) * up).astype(BF16)
        part = _dot(act, wd_ref[c0:c1, :])
        acc = part if acc is None else acc + part
    y = x + 0.5 * acc
    if final_norm:
        y = _rms(y, fg_ref[...], RMS_EPS)
    return y


def _ffn_kernel(*refs, final_norm, n_cast):
    x_ref, g_ref, wg_ref, wu_ref, wd_ref, fg_ref = refs[:6]
    cast_src = refs[6:6 + n_cast]
    w_in_ref = refs[6 + n_cast]
    o_ref = refs[7 + n_cast]
    cast_dst = refs[8 + n_cast:8 + 2 * n_cast]
    split_dst = refs[8 + 2 * n_cast:]
    o_ref[...] = _ffn_half_step(x_ref[...], g_ref, wg_ref, wu_ref, wd_ref, fg_ref, final_norm)
    _cast_kernel(*cast_src, *cast_dst)
    _split_w_in_kernel(w_in_ref, *split_dst)


def _cast_block_rows(rows, steps):
    for block in range(2 * SUBLANES, rows + 1, 2 * SUBLANES):
        if rows % block == 0 and steps % (rows // block) == 0:
            return block
    raise ValueError((rows, steps))


def _ffn(x, g, wg, wu, wd, fg, final_norm, cast, w_in, layer):
    n = x.shape[0]
    steps = n // FFN_ROWS
    row = pl.BlockSpec((FFN_ROWS, D_MODEL), lambda i: (i, 0))

    def cast_spec(a):
        block = _cast_block_rows(a.shape[0], steps)
        reps = steps // (a.shape[0] // block)
        return pl.BlockSpec((block, a.shape[1]), lambda i: (i // reps, 0))

    cast_specs = [cast_spec(a) for a in cast]
    w_rows = _cast_block_rows(w_in.shape[1], steps)
    w_reps = steps // (w_in.shape[1] // w_rows)
    widths = (3 * DN_WIDTH, DN_WIDTH, LANES, DIFF_WIDTH, 2 * DIFF_WIDTH, 2 * D_MODEL)
    outs = pl.pallas_call(
        functools.partial(_ffn_kernel, final_norm=final_norm, n_cast=len(cast)),
        grid=(steps,),
        in_specs=[row, _resident((1, D_MODEL)), _resident((D_MODEL, D_FF)), _resident((D_MODEL, D_FF)),
                  _resident((D_FF, D_MODEL)), _resident((1, D_MODEL))] + cast_specs
                 + [pl.BlockSpec((None, w_rows, w_in.shape[2]), lambda i: (layer, i // w_reps, 0))],
        out_specs=[row] + cast_specs + [pl.BlockSpec((w_rows, c), lambda i: (i // w_reps, 0)) for c in widths],
        out_shape=[jax.ShapeDtypeStruct((n, D_MODEL), F32)]
                  + [jax.ShapeDtypeStruct(a.shape, BF16) for a in cast]
                  + [jax.ShapeDtypeStruct((w_in.shape[1], c), BF16) for c in widths],
        compiler_params=pltpu.CompilerParams(dimension_semantics=("arbitrary",),
                                             vmem_limit_bytes=VMEM_LIMIT_BYTES),
        name="ffn",
    )(x, g, wg, wu, wd, fg, *cast, w_in)
    return outs[0], outs[1:1 + len(cast)], outs[1 + len(cast):]


def _proj_kernel(x_ref, g_ref, wqkv_ref, wz_ref, wab_ref, wdfq_ref, wdfkv_ref, cw_ref,
                 qkv_ref, z_ref, ab_ref, df_ref, ext_ref, *, tiles_per_seq):
    halo = SUBLANES

    @pl.when(pl.program_id(0) % tiles_per_seq == 0)
    def _():
        ext_ref[0:halo, :] = jnp.zeros((halo, 3 * DN_WIDTH), F32)

    h = _rms(x_ref[...], g_ref[...], RMS_EPS).astype(BF16)
    pre = _dot(h, wqkv_ref[...])
    ext_ref[halo:halo + PROJ_ROWS, :] = pre
    cw = cw_ref[...]
    y = cw[CONV_WIDTH - 1:CONV_WIDTH, :] * pre
    for j in range(CONV_WIDTH - 1):
        y = y + cw[j:j + 1, :] * ext_ref[pl.ds(halo - (CONV_WIDTH - 1) + j, PROJ_ROWS), :]
    ext_ref[0:halo, :] = ext_ref[PROJ_ROWS:PROJ_ROWS + halo, :]
    qkv_ref[...] = (y * _sigmoid(y)).astype(BF16)
    z_ref[...] = _dot(h, wz_ref[...]).astype(BF16)
    ab_ref[...] = _dot(h, wab_ref[...])
    df_ref[:, :DIFF_WIDTH] = (_dot(h, wdfq_ref[...]) * (DIFF_HEAD_DIM ** -0.5 * LOG2E)).astype(BF16)
    df_ref[:, DIFF_WIDTH:] = _dot(h, wdfkv_ref[...]).astype(BF16)


def _proj(x1, g, wqkv, wz, wab, wdfq, wdfkv, cw, seq):
    n = x1.shape[0]
    row = lambda w: pl.BlockSpec((PROJ_ROWS, w), lambda i: (i, 0))
    return pl.pallas_call(
        functools.partial(_proj_kernel, tiles_per_seq=seq // PROJ_ROWS),
        grid=(n // PROJ_ROWS,),
        in_specs=[row(D_MODEL), _resident((1, D_MODEL)), _resident(wqkv.shape), _resident(wz.shape),
                  _resident(wab.shape), _resident(wdfq.shape), _resident(wdfkv.shape), _resident(cw.shape)],
        out_specs=[row(3 * DN_WIDTH), row(DN_WIDTH), row(LANES), row(3 * DIFF_WIDTH)],
        out_shape=[jax.ShapeDtypeStruct((n, 3 * DN_WIDTH), BF16),
                   jax.ShapeDtypeStruct((n, DN_WIDTH), BF16),
                   jax.ShapeDtypeStruct((n, LANES), F32),
                   jax.ShapeDtypeStruct((n, 3 * DIFF_WIDTH), BF16)],
        scratch_shapes=[pltpu.VMEM((PROJ_ROWS + SUBLANES, 3 * DN_WIDTH), F32)],
        compiler_params=pltpu.CompilerParams(dimension_semantics=("arbitrary",),
                                             vmem_limit_bytes=VMEM_LIMIT_BYTES),
        name="proj",
    )(x1, g, wqkv, wz, wab, wdfq, wdfkv, cw)


def _gdn_kernel(q_ref, k_ref, v_ref, ab_ref, z_ref, alog_ref, dtb_ref, onorm_ref, o_ref, s_ref, *, batch):
    c = DN_CHUNK
    hd = DN_HEAD_DIM
    units = [(b, p) for b in range(batch) for p in range(DN_PAIRS)]

    @pl.when(pl.program_id(0) == 0)
    def _():
        s_ref[...] = jnp.zeros(s_ref.shape, F32)

    row = lax.broadcasted_iota(jnp.int32, (c, LANES), 0)
    lane = lax.broadcasted_iota(jnp.int32, (c, LANES), 1)
    col = lane & (hd - 1)
    even_bf = jnp.where(lane < hd, 1.0, 0.0).astype(BF16)
    odd_bf = jnp.where(lane < hd, 0.0, 1.0).astype(BF16)
    incl = col <= row
    strict = col < row
    same16 = (row >> 4) == (col >> 4)
    same32 = (row >> 5) == (col >> 5)
    eye = jnp.where(col == row, 1.0, 0.0).astype(F32)
    r2 = lax.broadcasted_iota(jnp.int32, (LANES, LANES), 0)
    l2 = lax.broadcasted_iota(jnp.int32, (LANES, LANES), 1)
    same_head = (r2 >> 6) == (l2 >> 6)
    head_ones = jnp.where(same_head, 1.0, 0.0).astype(BF16)
    head_ones2 = jnp.concatenate([head_ones, head_ones], axis=0)
    r3 = lax.broadcasted_iota(jnp.int32, (3 * LANES, LANES), 0) & (LANES - 1)
    l3 = lax.broadcasted_iota(jnp.int32, (3 * LANES, LANES), 1)
    rc = lax.broadcasted_iota(jnp.int32, (c, 3 * c), 0)
    cc = lax.broadcasted_iota(jnp.int32, (c, 3 * c), 1) & (c - 1)
    lower3 = jnp.where(cc <= rc, 1.0, 0.0).astype(BF16)
    ones3 = jnp.ones((c, 3 * c), BF16)

    def stack(x):
        xb = x.astype(BF16)
        return jnp.concatenate([xb * even_bf, xb * odd_bf], axis=0)

    def mm(a, b):
        return _dot(a.astype(BF16), b.astype(BF16))

    def mm_nt(a, b):
        return _dot_nt(a.astype(BF16), b.astype(BF16))

    def head_sums(x):
        return _dot(jnp.concatenate(_split_bf16(x, 2), axis=1), head_ones2)

    def exact_rows(lhs3, x):
        return _dot(lhs3, jnp.concatenate(_split_bf16(x, 3), axis=0))

    ab3 = [jnp.concatenate(_split_bf16(ab_ref[b], 3), axis=1) for b in range(batch)]
    q_t, k_t, v_t, g_t, beta_t = [], [], [], [], []
    for b, p in units:
        lanes = slice(p * LANES, (p + 1) * LANES)
        expand_a = jnp.where(r3 == 2 * p + (l3 >> 6), 1.0, 0.0).astype(BF16)
        expand_b = jnp.where(r3 == DN_HEADS + 2 * p + (l3 >> 6), 1.0, 0.0).astype(BF16)
        sp_in = _dot(ab3[b], expand_a) + dtb_ref[0:1, lanes]
        softplus = jnp.maximum(sp_in, 0.0) + jnp.log(1.0 + jnp.exp(-jnp.abs(sp_in)))
        g_t.append(-jnp.exp(alog_ref[0:1, lanes]) * softplus)
        beta_t.append(_sigmoid(_dot(ab3[b], expand_b)))
        q = q_ref[b, :, lanes].astype(F32)
        k = k_ref[b, :, lanes].astype(F32)
        q_t.append(q * lax.rsqrt(head_sums(q * q) + L2_EPS) * (hd ** -0.5))
        k_t.append(k * lax.rsqrt(head_sums(k * k) + L2_EPS))
        v_t.append(v_ref[b, :, lanes].astype(F32))

    states = [s_ref[i] for i in range(len(units))]
    outs = [[] for _ in units]
    for n0 in range(0, GDN_ROWS // c, GDN_PREP_CHUNKS):
        items = [(u, n) for n in range(n0, n0 + GDN_PREP_CHUNKS) for u in range(len(units))]

        def take(per_unit):
            return [per_unit[u][n * c:(n + 1) * c] for u, n in items]

        q = take(q_t)
        k = take(k_t)
        v = take(v_t)
        beta = take(beta_t)
        gc = [exact_rows(lower3, x) for x in take(g_t)]
        gc_row = [exact_rows(ones3, x * eye) for x in gc]
        decay = [jnp.where(incl, jnp.exp(jnp.where(incl, a - b_, 0.0)), 0.0) for a, b_ in zip(gc, gc_row)]
        exp_gc = [jnp.exp(x) for x in gc]
        kb = [a * b_ for a, b_ in zip(k, beta)]
        kq = [mm_nt(jnp.concatenate([a, b_], axis=0), stack(k_)) for a, b_, k_ in zip(kb, q, k)]
        a_mat = [jnp.where(strict, x[:c] * d, 0.0) for x, d in zip(kq, decay)]
        attn = [x[c:] * d for x, d in zip(kq, decay)]

        neg_d = [jnp.where(strict & same16, -x, 0.0) for x in a_mat]
        t_inv = [eye + x for x in neg_d]
        power = [mm(x, stack(x)) for x in neg_d]
        for _ in range(2):
            both = [mm(jnp.concatenate([pw, t], axis=0), stack(pw)) for pw, t in zip(power, t_inv)]
            power = [x[:c] for x in both]
            t_inv = [t + x[c:] for t, x in zip(t_inv, both)]
        t_inv = [t + mm(t, stack(pw)) for t, pw in zip(t_inv, power)]
        for block in (strict & same32 & ~same16, strict & ~same32):
            left = [mm(t, stack(jnp.where(block, x, 0.0))) for t, x in zip(t_inv, a_mat)]
            t_inv = [t - mm(x, stack(t)) for t, x in zip(t_inv, left)]

        uw = [mm(t, jnp.concatenate([stack(v_ * b_), stack(kb_ * e)], axis=1))
              for t, v_, b_, kb_, e in zip(t_inv, v, beta, kb, exp_gc)]
        q_dec = [a * e for a, e in zip(q, exp_gc)]
        gc_last = [x[c - 1:c, :] for x in gc]
        k_dec_t = [(k_ * jnp.exp(gl - g_)).T for k_, gl, g_ in zip(k, gc_last, gc)]

        for first in range(0, len(items), len(units)):
            sel = slice(first, first + len(units))
            ws = [mm(jnp.concatenate([x[:, LANES:], qd], axis=0), st)
                  for x, qd, st in zip(uw[sel], q_dec[sel], states)]
            v_new = [x[:, :LANES] - y[:c] for x, y in zip(uw[sel], ws)]
            o_n = [y[c:] + mm(a, stack(vn)) for y, a, vn in zip(ws, attn[sel], v_new)]
            for lst, x in zip(outs, o_n):
                lst.append(x)
            states = [st * jnp.exp(gl) + jnp.where(same_head, mm(kt, vn), 0.0)
                      for st, gl, kt, vn in zip(states, gc_last[sel], k_dec_t[sel], v_new)]

    for i, (b, p) in enumerate(units):
        lanes = slice(p * LANES, (p + 1) * LANES)
        s_ref[i] = states[i]
        o = jnp.concatenate(outs[i], axis=0)
        ms = head_sums(o * o) * (1.0 / hd)
        z = z_ref[b, :, lanes].astype(F32)
        o_ref[b, :, lanes] = (o * lax.rsqrt(ms + RMS_EPS) * onorm_ref[0:1, lanes] * (z * _sigmoid(z))).astype(BF16)


def _gdn(qkv, ab, z, alog_lanes, dtb_lanes, onorm_lanes, batch, seq):
    qkv3 = qkv.reshape(batch, seq, 3 * DN_WIDTH)

    def tile(width, col):
        return pl.BlockSpec((batch, GDN_ROWS, width), lambda t: (0, t, col))

    return pl.pallas_call(
        functools.partial(_gdn_kernel, batch=batch),
        grid=(seq // GDN_ROWS,),
        in_specs=[tile(DN_WIDTH, 0), tile(DN_WIDTH, 1), tile(DN_WIDTH, 2), tile(LANES, 0), tile(DN_WIDTH, 0),
                  _resident(alog_lanes.shape), _resident(dtb_lanes.shape), _resident(onorm_lanes.shape)],
        out_specs=tile(DN_WIDTH, 0),
        out_shape=jax.ShapeDtypeStruct((batch, seq, DN_WIDTH), BF16),
        scratch_shapes=[pltpu.VMEM((batch * DN_PAIRS, LANES, LANES), F32)],
        compiler_params=pltpu.CompilerParams(dimension_semantics=("arbitrary",),
                                             vmem_limit_bytes=VMEM_LIMIT_BYTES),
        name="gdn",
    )(qkv3, qkv3, qkv3, ab.reshape(batch, seq, LANES), z.reshape(batch, seq, DN_WIDTH),
      alog_lanes, dtb_lanes, onorm_lanes).reshape(batch * seq, DN_WIDTH)


def _attn_kernel(off_ref, coef_ref, q_ref, k_ref, v_ref, qcols_ref, kcols_ref, vcols_ref, lam_ref, subln_ref,
                 o_ref, qa_ref, m_ref, acc_ref, k2max_ref, *, lam_init):
    h = pl.program_id(1)
    i = pl.program_id(2)
    d = DIFF_HEAD_DIM
    tq = ATTN_Q
    tk = ATTN_K

    q = q_ref[...]
    first = lax.broadcasted_iota(jnp.int32, q.shape, 1) < d
    qcols = jnp.broadcast_to(qcols_ref[0, 0:1, :], (tq, LANES))
    qa_ref[0:tq, :] = jnp.concatenate([jnp.where(first, q, 0), qcols], axis=1)
    qa_ref[tq:2 * tq, :] = jnp.concatenate([jnp.where(first, 0, q), qcols], axis=1)
    m_ref[...] = jnp.full(m_ref.shape, -jnp.inf, F32)
    acc_ref[...] = jnp.zeros(acc_ref.shape, F32)
    kcols = kcols_ref[...]
    vcols = vcols_ref[...]

    def max_sq_norm(x):
        sq = x.astype(F32) * x.astype(F32)
        r = lax.broadcasted_iota(jnp.int32, (LANES, LANES), 0)
        c = lax.broadcasted_iota(jnp.int32, (LANES, LANES), 1)
        same_half = jnp.where((r < d) == (c < d), 1.0, 0.0).astype(BF16)
        half_sums = _dot(sq.astype(BF16), same_half)
        return jnp.max(jnp.max(half_sums, axis=0, keepdims=True), axis=1, keepdims=True)

    @pl.when(i == 0)
    def _():
        k2max_ref[...] = jnp.broadcast_to(max_sq_norm(k_ref[...]), k2max_ref.shape)

    qk_bound = jnp.sqrt(max_sq_norm(q) * k2max_ref[0:1, 0:1]) * NORM_SLACK

    def kv_steps(blocks, live=None):
        scores, values, offs = [], [], []
        m_low = None
        for j, span, diag in blocks:
            start = pl.multiple_of(j * tk, tk)
            keys = span * tk
            ka = jnp.concatenate([k_ref[pl.ds(start, keys), :], kcols[:keys]], axis=1)
            rg = ATTN_ROW_GROUP if diag is None else ATTN_DIAG_GROUP
            groups = range(0, 2 * tq, rg)
            widths = [keys if diag is None else keys - tk + (r0 % tq) + rg for r0 in groups]
            scores.append([(r0, _dot_nt(qa_ref[r0:r0 + rg, :], ka[:w])) for r0, w in zip(groups, widths)])
            values.append(jnp.concatenate([v_ref[pl.ds(start, keys), :], vcols[:keys]], axis=1))
            offs.append(off_ref[h] * j.astype(F32))
        for idx, ((_, _, diag), block_scores, va, off) in enumerate(zip(blocks, scores, values, offs)):
            for r0, s in block_scores:
                rg, width = s.shape
                rows = slice(r0, r0 + rg)
                if diag is not None:
                    qrow = lax.broadcasted_iota(jnp.int32, (rg, width), 0) + ((r0 % tq) + diag)
                    kcol = lax.broadcasted_iota(jnp.int32, (rg, width), 1)
                    s = jnp.where(kcol <= qrow, s, -jnp.inf)
                elif live is not None:
                    s = jnp.where(live, s, -jnp.inf)
                m_old = m_ref[rows, :]
                m_blk = jnp.broadcast_to(jnp.max(s, axis=-1, keepdims=True), (rg, LANES)) + off
                m_new = jnp.maximum(m_old, m_blk)
                alpha = jnp.exp2(m_old - m_new)
                shift = m_new - off
                p = jnp.concatenate([jnp.exp2((s[:, c0:c0 + LANES] - shift).astype(BF16))
                                     for c0 in range(0, width, LANES)], axis=1)
                acc_ref[rows, :] = (jnp.concatenate([alpha, alpha], axis=1) * acc_ref[rows, :]
                                    + _dot(p, va[:width]))
                m_ref[rows, :] = m_new
                if idx == 0:
                    low = jnp.min(m_new, axis=0, keepdims=True)
                    m_low = low if m_low is None else jnp.minimum(m_low, low)
        return m_low[:, 0:1]

    prev = jnp.maximum(i - 1, 0)
    m_low = kv_steps([(i, 1, 0), (prev, 1, None)], live=i > 0)

    reach = (m_low - SKIP_LOG2 - qk_bound) / off_ref[h]
    j0 = jnp.clip(jnp.floor(reach), 0.0, prev.astype(F32)).astype(jnp.int32)[0, 0]

    def body4(t, carry):
        kv_steps([(j0 + 4 * t, ATTN_SPAN, None), (j0 + 4 * t + 2, ATTN_SPAN, None)])
        return carry

    def body(t, carry):
        kv_steps([(j0 + 2 * t, ATTN_SPAN, None)])
        return carry

    n_past = prev - j0
    n_quads = n_past // 4
    lax.fori_loop(0, n_quads, body4, 0)
    lax.fori_loop(2 * n_quads, n_past // 2, body, 0)

    @pl.when(n_past % 2 == 1)
    def _():
        kv_steps([(prev - 1, 1, None)])

    lam_rows = lam_ref[...]
    lam = (jnp.exp(jnp.sum(lam_rows[0:1] * lam_rows[1:2], axis=-1, keepdims=True))
           - jnp.exp(jnp.sum(lam_rows[2:3] * lam_rows[3:4], axis=-1, keepdims=True)) + lam_init)
    acc1 = acc_ref[0:tq, :]
    acc2 = acc_ref[tq:2 * tq, :]
    o = acc1[:, :LANES] / acc1[:, LANES:] - lam * (acc2[:, :LANES] / acc2[:, LANES:])
    o_ref[...] = (_rms(o, subln_ref[...], SUBLN_EPS) * (1.0 - lam_init)).astype(BF16)


def _attn(slopes, df, lam_rows, subln, batch, seq, lam_init):
    span_keys = ATTN_SPAN * ATTN_K
    assert ATTN_Q == ATTN_K and span_keys <= 256 * POS_LO
    assert (2 * ATTN_Q) % ATTN_ROW_GROUP == 0 and ATTN_Q % ATTN_DIAG_GROUP == 0
    nq = seq // ATTN_Q
    df3 = df.reshape(batch, seq, 3 * DIFF_WIDTH)
    coef = slopes * LOG2E
    c_parts = jnp.stack(_split_bf16(coef, 3), axis=1)
    qcols = jnp.zeros((DIFF_HEADS, SUBLANES, LANES), BF16).at[:, :, 0:6].set(
        jnp.broadcast_to(jnp.tile(c_parts, (1, 2))[:, None, :], (DIFF_HEADS, SUBLANES, 6)))
    pos = np.arange(span_keys)
    kcols_np = np.zeros((span_keys, LANES), np.float32)
    kcols_np[:, 0:3] = (pos // POS_LO * POS_LO)[:, None]
    kcols_np[:, 3:6] = (pos % POS_LO)[:, None]
    kcols = jnp.asarray(kcols_np, BF16)
    vcols = jnp.ones((span_keys, LANES), BF16)
    block_off = coef * ATTN_K

    seq_block = lambda col0: pl.BlockSpec((None, seq, LANES), lambda b, h, i: (b, 0, col0 + h))
    const = lambda shape: pl.BlockSpec(shape, lambda b, h, i: (0,) * len(shape))
    return pl.pallas_call(
        functools.partial(_attn_kernel, lam_init=lam_init),
        grid=(batch, DIFF_HEADS, nq),
        in_specs=[pl.BlockSpec(memory_space=pltpu.SMEM), pl.BlockSpec(memory_space=pltpu.SMEM),
                  pl.BlockSpec((None, ATTN_Q, LANES), lambda b, h, i: (b, i, h)),
                  seq_block(DIFF_HEADS), seq_block(2 * DIFF_HEADS),
                  pl.BlockSpec((1, SUBLANES, LANES), lambda b, h, i: (h, 0, 0)),
                  const((span_keys, LANES)), const((span_keys, LANES)),
                  const((SUBLANES, LANES)), const((1, DIFF_V_DIM))],
        out_specs=pl.BlockSpec((None, ATTN_Q, LANES), lambda b, h, i: (b, i, h)),
        out_shape=jax.ShapeDtypeStruct((batch, seq, DIFF_WIDTH), BF16),
        scratch_shapes=[pltpu.VMEM((2 * ATTN_Q, 2 * LANES), BF16),
                        pltpu.VMEM((2 * ATTN_Q, LANES), F32),
                        pltpu.VMEM((2 * ATTN_Q, 2 * LANES), F32),
                        pltpu.VMEM((SUBLANES, LANES), F32)],
        compiler_params=pltpu.CompilerParams(dimension_semantics=("arbitrary", "arbitrary", "arbitrary"),
                                             vmem_limit_bytes=VMEM_LIMIT_BYTES),
        name="diff_attn",
    )(block_off, coef, df3, df3, df3, qcols, kcols, vcols, lam_rows, subln).reshape(batch * seq, DIFF_WIDTH)


def _merge_ffn_kernel(x_ref, g_ref, oa_ref, ob_ref, wa_ref, wb_ref, wgate_ref, wout_ref,
                      fg_norm_ref, wg_ref, wu_ref, wd_ref, fg_ref, o_ref, *, final_norm):
    x = x_ref[...]
    h = _rms(x, g_ref[...], RMS_EPS).astype(BF16)
    gates = _sigmoid(_dot(h, wgate_ref[...]))
    y_a = _dot(oa_ref[...], wa_ref[...])
    y_b = _dot(ob_ref[...], wb_ref[...])
    mixed = (gates[:, :D_MODEL] * y_a + gates[:, D_MODEL:] * y_b).astype(BF16)
    x = x + _dot(mixed, wout_ref[...])
    o_ref[...] = _ffn_half_step(x, fg_norm_ref, wg_ref, wu_ref, wd_ref, fg_ref, final_norm)


def _merge_ffn(x1, g, o_a, o_b, wa, wb, wgate, wout, ffn_g, wg, wu, wd, fg, final_norm):
    n = x1.shape[0]
    row = lambda w: pl.BlockSpec((MERGE_ROWS, w), lambda i: (i, 0))
    return pl.pallas_call(
        functools.partial(_merge_ffn_kernel, final_norm=final_norm),
        grid=(n // MERGE_ROWS,),
        in_specs=[row(D_MODEL), _resident((1, D_MODEL)), row(DN_WIDTH), row(DIFF_WIDTH),
                  _resident(wa.shape), _resident(wb.shape), _resident(wgate.shape), _resident(wout.shape),
                  _resident((1, D_MODEL)), _resident(wg.shape), _resident(wu.shape), _resident(wd.shape),
                  _resident((1, D_MODEL))],
        out_specs=row(D_MODEL),
        out_shape=jax.ShapeDtypeStruct((n, D_MODEL), F32),
        compiler_params=pltpu.CompilerParams(dimension_semantics=("arbitrary",),
                                             vmem_limit_bytes=VMEM_LIMIT_BYTES),
        name="merge_ffn",
    )(x1, g, o_a, o_b, wa, wb, wgate, wout, ffn_g, wg, wu, wd, fg)


def _lambda_init(layer_idx):
    return 0.8 - 0.6 * math.exp(-0.3 * layer_idx)


def _head_lanes(per_head):
    lanes = jnp.repeat(per_head.astype(F32), DN_HEAD_DIM)
    return jnp.broadcast_to(lanes[None, :], (SUBLANES, DN_WIDTH))


def kernel(x, ffn1_norm, ffn1_w_gate, ffn1_w_up, ffn1_w_down, mix_norm, w_in, conv_qkv, dn_a_log, dn_dt_bias, dn_out_norm, diff_lambda_q1, diff_lambda_k1, diff_lambda_q2, diff_lambda_k2, diff_subln, w_branch_a, w_branch_b, w_out, ffn2_norm, ffn2_w_gate, ffn2_w_up, ffn2_w_down, final_norm):
    batch, seq, _ = x.shape
    depth = ffn1_norm.shape[0]
    n = batch * seq
    xs = x.reshape(n, D_MODEL)
    slopes = jnp.asarray([2.0 ** (-8.0 * (i + 1) / DIFF_HEADS) for i in range(DIFF_HEADS)], F32)
    final_g = final_norm.reshape(1, D_MODEL)

    for l in range(depth):
        f1_gate, f1_up, f1_down = _to_bf16(ffn1_w_gate[l], ffn1_w_up[l], ffn1_w_down[l])
        xs, (f2_gate, f2_up, f2_down, wa, wb, wout), (wqkv, wz, wab, wdfq, wdfkv, wgates) = _ffn(
            xs, ffn1_norm[l].reshape(1, D_MODEL), f1_gate, f1_up, f1_down, final_g, False,
            (ffn2_w_gate[l], ffn2_w_up[l], ffn2_w_down[l], w_branch_a[l], w_branch_b[l], w_out[l]), w_in, l)

        qkv, z, ab, df = _proj(xs, mix_norm[l].reshape(1, D_MODEL), wqkv, wz, wab, wdfq, wdfkv, conv_qkv[l], seq)

        o_a = _gdn(qkv, ab, z, _head_lanes(dn_a_log[l]), _head_lanes(dn_dt_bias[l]),
                   jnp.broadcast_to(jnp.tile(dn_out_norm[l].astype(F32), DN_HEADS)[None, :], (SUBLANES, DN_WIDTH)),
                   batch, seq)

        lam_rows = jnp.zeros((SUBLANES, LANES), F32).at[0:4, :DIFF_HEAD_DIM].set(
            jnp.stack([diff_lambda_q1[l], diff_lambda_k1[l], diff_lambda_q2[l], diff_lambda_k2[l]]).astype(F32))
        o_b = _attn(slopes, df, lam_rows, diff_subln[l].reshape(1, DIFF_V_DIM), batch, seq, _lambda_init(l))

        xs = _merge_ffn(xs, mix_norm[l].reshape(1, D_MODEL), o_a, o_b, wa, wb, wgates, wout,
                        ffn2_norm[l].reshape(1, D_MODEL), f2_gate, f2_up, f2_down, final_g,
                        final_norm=(l == depth - 1))

    return xs.reshape(batch, seq, D_MODEL)
```

```python
import functools
import math

import jax
import jax.numpy as jnp
import numpy as np
from jax import lax
from jax.experimental import pallas as pl
from jax.experimental.pallas import tpu as pltpu

F32 = jnp.float32
BF16 = jnp.bfloat16

D_MODEL = 1024
D_FF = 2816
RMS_EPS = 1e-6
SUBLN_EPS = 1e-5
L2_EPS = 1e-6
CONV_WIDTH = 4
DN_HEADS = 8
DN_HEAD_DIM = 64
DN_WIDTH = DN_HEADS * DN_HEAD_DIM
DN_CHUNK = 64
DN_PAIRS = DN_HEADS // 2
DIFF_HEADS = 4
DIFF_HEAD_DIM = 64
DIFF_V_DIM = 2 * DIFF_HEAD_DIM
DIFF_WIDTH = DIFF_HEADS * DIFF_V_DIM
LANES = 128
SUBLANES = 8
LOG2E = math.log2(math.e)

VMEM_LIMIT_BYTES = 56 * 1024 * 1024

FFN_ROWS = 512
FF_CHUNKS = ((0, 512), (512, 1024), (1024, 1536), (1536, 2048), (2048, 2560), (2560, 2816))
PROJ_ROWS = 512
GDN_ROWS = 512
GDN_PREP_CHUNKS = 2
ATTN_Q = 512
ATTN_K = 512
ATTN_SPAN = 2
ATTN_ROW_GROUP = 512
ATTN_DIAG_GROUP = 256
MERGE_ROWS = 512
CAST_STEPS = 8
POS_LO = 16
SKIP_LOG2 = 160.0
NORM_SLACK = 1.01


def _rms(x, g, eps):
    return x * lax.rsqrt(jnp.mean(x * x, axis=-1, keepdims=True) + eps) * g


def _sigmoid(x):
    return 0.5 * jnp.tanh(0.5 * x) + 0.5


def _dot(a, b):
    return jnp.dot(a, b, preferred_element_type=F32)


def _dot_nt(a, b):
    return lax.dot_general(a, b, (((1,), (1,)), ((), ())), preferred_element_type=F32)


def _split_bf16(x, terms):
    parts = []
    rest = x
    for t in range(terms):
        part = rest.astype(BF16)
        parts.append(part)
        if t + 1 < terms:
            rest = rest - part.astype(F32)
    return parts


def _resident(shape):
    return pl.BlockSpec(shape, lambda *_: (0,) * len(shape), pipeline_mode=pl.Buffered(1))


def _cast_kernel(*refs):
    half = len(refs) // 2
    for src, dst in zip(refs[:half], refs[half:]):
        dst[...] = src[...].astype(BF16)


def _to_bf16(*arrays):
    specs = [pl.BlockSpec((a.shape[0] // CAST_STEPS, a.shape[1]), lambda i: (i, 0)) for a in arrays]
    return pl.pallas_call(
        _cast_kernel,
        grid=(CAST_STEPS,),
        in_specs=specs,
        out_specs=specs,
        out_shape=[jax.ShapeDtypeStruct(a.shape, BF16) for a in arrays],
        compiler_params=pltpu.CompilerParams(dimension_semantics=("arbitrary",),
                                             vmem_limit_bytes=VMEM_LIMIT_BYTES),
        name="cast_weights",
    )(*arrays)


def _split_w_in_kernel(w_ref, qkv_ref, z_ref, ab_ref, dfq_ref, dfkv_ref, gates_ref):
    c_z = 3 * DN_WIDTH
    c_ab = c_z + DN_WIDTH
    c_dfq = c_ab + 2 * DN_HEADS
    c_dfkv = c_dfq + DIFF_WIDTH
    c_gates = c_dfkv + 2 * DIFF_WIDTH
    w = w_ref[...]
    qkv_ref[...] = w[:, :c_z].astype(BF16)
    z_ref[...] = w[:, c_z:c_ab].astype(BF16)
    lane = lax.broadcasted_iota(jnp.int32, (w.shape[0], LANES), 1)
    ab_ref[...] = jnp.where(lane < 2 * DN_HEADS, w[:, c_ab:c_ab + LANES], 0.0).astype(BF16)
    dfq_ref[...] = w[:, c_dfq:c_dfkv].astype(BF16)
    dfkv_ref[...] = w[:, c_dfkv:c_gates].astype(BF16)
    gates_ref[...] = w[:, c_gates:c_gates + 2 * D_MODEL].astype(BF16)


def _ffn_half_step(x, g_ref, wg_ref, wu_ref, wd_ref, fg_ref, final_norm):
    h = _rms(x, g_ref[...], RMS_EPS).astype(BF16)
    acc = None
    for c0, c1 in FF_CHUNKS:
        gate = _dot(h, wg_ref[:, c0:c1])
        up = _dot(h, wu_ref[:, c0:c1])
        act = (gate * _sigmoid(gate) * up).astype(BF16)
        part = _dot(act, wd_ref[c0:c1, :])
        acc = part if acc is None else acc + part
    y = x + 0.5 * acc
    if final_norm:
        y = _rms(y, fg_ref[...], RMS_EPS)
    return y


def _ffn_kernel(*refs, final_norm, n_cast):
    x_ref, g_ref, wg_ref, wu_ref, wd_ref, fg_ref = refs[:6]
    cast_src = refs[6:6 + n_cast]
    w_in_ref = refs[6 + n_cast]
    o_ref = refs[7 + n_cast]
    cast_dst = refs[8 + n_cast:8 + 2 * n_cast]
    split_dst = refs[8 + 2 * n_cast:]
    o_ref[...] = _ffn_half_step(x_ref[...], g_ref, wg_ref, wu_ref, wd_ref, fg_ref, final_norm)
    _cast_kernel(*cast_src, *cast_dst)
    _split_w_in_kernel(w_in_ref, *split_dst)


def _cast_block_rows(rows, steps):
    for block in range(2 * SUBLANES, rows + 1, 2 * SUBLANES):
        if rows % block == 0 and steps % (rows // block) == 0:
            return block
    raise ValueError((rows, steps))


def _ffn(x, g, wg, wu, wd, fg, final_norm, cast, w_in, layer):
    n = x.shape[0]
    steps = n // FFN_ROWS
    row = pl.BlockSpec((FFN_ROWS, D_MODEL), lambda i: (i, 0))

    def cast_spec(a):
        block = _cast_block_rows(a.shape[0], steps)
        reps = steps // (a.shape[0] // block)
        return pl.BlockSpec((block, a.shape[1]), lambda i: (i // reps, 0))

    cast_specs = [cast_spec(a) for a in cast]
    w_rows = _cast_block_rows(w_in.shape[1], steps)
    w_reps = steps // (w_in.shape[1] // w_rows)
    widths = (3 * DN_WIDTH, DN_WIDTH, LANES, DIFF_WIDTH, 2 * DIFF_WIDTH, 2 * D_MODEL)
    outs = pl.pallas_call(
        functools.partial(_ffn_kernel, final_norm=final_norm, n_cast=len(cast)),
        grid=(steps,),
        in_specs=[row, _resident((1, D_MODEL)), _resident((D_MODEL, D_FF)), _resident((D_MODEL, D_FF)),
                  _resident((D_FF, D_MODEL)), _resident((1, D_MODEL))] + cast_specs
                 + [pl.BlockSpec((None, w_rows, w_in.shape[2]), lambda i: (layer, i // w_reps, 0))],
        out_specs=[row] + cast_specs + [pl.BlockSpec((w_rows, c), lambda i: (i // w_reps, 0)) for c in widths],
        out_shape=[jax.ShapeDtypeStruct((n, D_MODEL), F32)]
                  + [jax.ShapeDtypeStruct(a.shape, BF16) for a in cast]
                  + [jax.ShapeDtypeStruct((w_in.shape[1], c), BF16) for c in widths],
        compiler_params=pltpu.CompilerParams(dimension_semantics=("arbitrary",),
                                             vmem_limit_bytes=VMEM_LIMIT_BYTES),
        name="ffn",
    )(x, g, wg, wu, wd, fg, *cast, w_in)
    return outs[0], outs[1:1 + len(cast)], outs[1 + len(cast):]


def _proj_kernel(x_ref, g_ref, wqkv_ref, wz_ref, wab_ref, wdfq_ref, wdfkv_ref, cw_ref,
                 qkv_ref, z_ref, ab_ref, df_ref, ext_ref, *, tiles_per_seq):
    halo = SUBLANES

    @pl.when(pl.program_id(0) % tiles_per_seq == 0)
    def _():
        ext_ref[0:halo, :] = jnp.zeros((halo, 3 * DN_WIDTH), F32)

    h = _rms(x_ref[...], g_ref[...], RMS_EPS).astype(BF16)
    pre = _dot(h, wqkv_ref[...])
    ext_ref[halo:halo + PROJ_ROWS, :] = pre
    cw = cw_ref[...]
    y = cw[CONV_WIDTH - 1:CONV_WIDTH, :] * pre
    for j in range(CONV_WIDTH - 1):
        y = y + cw[j:j + 1, :] * ext_ref[pl.ds(halo - (CONV_WIDTH - 1) + j, PROJ_ROWS), :]
    ext_ref[0:halo, :] = ext_ref[PROJ_ROWS:PROJ_ROWS + halo, :]
    qkv_ref[...] = (y * _sigmoid(y)).astype(BF16)
    z_ref[...] = _dot(h, wz_ref[...]).astype(BF16)
    ab_ref[...] = _dot(h, wab_ref[...])
    df_ref[:, :DIFF_WIDTH] = (_dot(h, wdfq_ref[...]) * (DIFF_HEAD_DIM ** -0.5 * LOG2E)).astype(BF16)
    df_ref[:, DIFF_WIDTH:] = _dot(h, wdfkv_ref[...]).astype(BF16)


def _proj(x1, g, wqkv, wz, wab, wdfq, wdfkv, cw, seq):
    n = x1.shape[0]
    row = lambda w: pl.BlockSpec((PROJ_ROWS, w), lambda i: (i, 0))
    return pl.pallas_call(
        functools.partial(_proj_kernel, tiles_per_seq=seq // PROJ_ROWS),
        grid=(n // PROJ_ROWS,),
        in_specs=[row(D_MODEL), _resident((1, D_MODEL)), _resident(wqkv.shape), _resident(wz.shape),
                  _resident(wab.shape), _resident(wdfq.shape), _resident(wdfkv.shape), _resident(cw.shape)],
        out_specs=[row(3 * DN_WIDTH), row(DN_WIDTH), row(LANES), row(3 * DIFF_WIDTH)],
        out_shape=[jax.ShapeDtypeStruct((n, 3 * DN_WIDTH), BF16),
                   jax.ShapeDtypeStruct((n, DN_WIDTH), BF16),
                   jax.ShapeDtypeStruct((n, LANES), F32),
                   jax.ShapeDtypeStruct((n, 3 * DIFF_WIDTH), BF16)],
        scratch_shapes=[pltpu.VMEM((PROJ_ROWS + SUBLANES, 3 * DN_WIDTH), F32)],
        compiler_params=pltpu.CompilerParams(dimension_semantics=("arbitrary",),
                                             vmem_limit_bytes=VMEM_LIMIT_BYTES),
        name="proj",
    )(x1, g, wqkv, wz, wab, wdfq, wdfkv, cw)


def _gdn_kernel(q_ref, k_ref, v_ref, ab_ref, z_ref, alog_ref, dtb_ref, onorm_ref, o_ref, s_ref, *, batch):
    c = DN_CHUNK
    hd = DN_HEAD_DIM
    units = [(b, p) for b in range(batch) for p in range(DN_PAIRS)]

    @pl.when(pl.program_id(0) == 0)
    def _():
        s_ref[...] = jnp.zeros(s_ref.shape, F32)

    row = lax.broadcasted_iota(jnp.int32, (c, LANES), 0)
    lane = lax.broadcasted_iota(jnp.int32, (c, LANES), 1)
    col = lane & (hd - 1)
    even_bf = jnp.where(lane < hd, 1.0, 0.0).astype(BF16)
    odd_bf = jnp.where(lane < hd, 0.0, 1.0).astype(BF16)
    incl = col <= row
    strict = col < row
    same16 = (row >> 4) == (col >> 4)
    same32 = (row >> 5) == (col >> 5)
    eye = jnp.where(col == row, 1.0, 0.0).astype(F32)
    r2 = lax.broadcasted_iota(jnp.int32, (LANES, LANES), 0)
    l2 = lax.broadcasted_iota(jnp.int32, (LANES, LANES), 1)
    same_head = (r2 >> 6) == (l2 >> 6)
    head_ones = jnp.where(same_head, 1.0, 0.0).astype(BF16)
    head_ones2 = jnp.concatenate([head_ones, head_ones], axis=0)
    r3 = lax.broadcasted_iota(jnp.int32, (3 * LANES, LANES), 0) & (LANES - 1)
    l3 = lax.broadcasted_iota(jnp.int32, (3 * LANES, LANES), 1)
    rc = lax.broadcasted_iota(jnp.int32, (c, 3 * c), 0)
    cc = lax.broadcasted_iota(jnp.int32, (c, 3 * c), 1) & (c - 1)
    lower3 = jnp.where(cc <= rc, 1.0, 0.0).astype(BF16)
    ones3 = jnp.ones((c, 3 * c), BF16)

    def stack(x):
        xb = x.astype(BF16)
        return jnp.concatenate([xb * even_bf, xb * odd_bf], axis=0)

    def mm(a, b):
        return _dot(a.astype(BF16), b.astype(BF16))

    def mm_nt(a, b):
        return _dot_nt(a.astype(BF16), b.astype(BF16))

    def head_sums(x):
        return _dot(jnp.concatenate(_split_bf16(x, 2), axis=1), head_ones2)

    def exact_rows(lhs3, x):
        return _dot(lhs3, jnp.concatenate(_split_bf16(x, 3), axis=0))

    ab3 = [jnp.concatenate(_split_bf16(ab_ref[b], 3), axis=1) for b in range(batch)]
    q_t, k_t, v_t, g_t, beta_t = [], [], [], [], []
    for b, p in units:
        lanes = slice(p * LANES, (p + 1) * LANES)
        expand_a = jnp.where(r3 == 2 * p + (l3 >> 6), 1.0, 0.0).astype(BF16)
        expand_b = jnp.where(r3 == DN_HEADS + 2 * p + (l3 >> 6), 1.0, 0.0).astype(BF16)
        sp_in = _dot(ab3[b], expand_a) + dtb_ref[0:1, lanes]
        softplus = jnp.maximum(sp_in, 0.0) + jnp.log(1.0 + jnp.exp(-jnp.abs(sp_in)))
        g_t.append(-jnp.exp(alog_ref[0:1, lanes]) * softplus)
        beta_t.append(_sigmoid(_dot(ab3[b], expand_b)))
        q = q_ref[b, :, lanes].astype(F32)
        k = k_ref[b, :, lanes].astype(F32)
        q_t.append(q * lax.rsqrt(head_sums(q * q) + L2_EPS) * (hd ** -0.5))
        k_t.append(k * lax.rsqrt(head_sums(k * k) + L2_EPS))
        v_t.append(v_ref[b, :, lanes].astype(F32))

    states = [s_ref[i] for i in range(len(units))]
    outs = [[] for _ in units]
    for n0 in range(0, GDN_ROWS // c, GDN_PREP_CHUNKS):
        items = [(u, n) for n in range(n0, n0 + GDN_PREP_CHUNKS) for u in range(len(units))]

        def take(per_unit):
            return [per_unit[u][n * c:(n + 1) * c] for u, n in items]

        q = take(q_t)
        k = take(k_t)
        v = take(v_t)
        beta = take(beta_t)
        gc = [exact_rows(lower3, x) for x in take(g_t)]
        gc_row = [exact_rows(ones3, x * eye) for x in gc]
        decay = [jnp.where(incl, jnp.exp(jnp.where(incl, a - b_, 0.0)), 0.0) for a, b_ in zip(gc, gc_row)]
        exp_gc = [jnp.exp(x) for x in gc]
        kb = [a * b_ for a, b_ in zip(k, beta)]
        kq = [mm_nt(jnp.concatenate([a, b_], axis=0), stack(k_)) for a, b_, k_ in zip(kb, q, k)]
        a_mat = [jnp.where(strict, x[:c] * d, 0.0) for x, d in zip(kq, decay)]
        attn = [x[c:] * d for x, d in zip(kq, decay)]

        neg_d = [jnp.where(strict & same16, -x, 0.0) for x in a_mat]
        t_inv = [eye + x for x in neg_d]
        power = [mm(x, stack(x)) for x in neg_d]
        for _ in range(2):
            both = [mm(jnp.concatenate([pw, t], axis=0), stack(pw)) for pw, t in zip(power, t_inv)]
            power = [x[:c] for x in both]
            t_inv = [t + x[c:] for t, x in zip(t_inv, both)]
        t_inv = [t + mm(t, stack(pw)) for t, pw in zip(t_inv, power)]
        for block in (strict & same32 & ~same16, strict & ~same32):
            left = [mm(t, stack(jnp.where(block, x, 0.0))) for t, x in zip(t_inv, a_mat)]
            t_inv = [t - mm(x, stack(t)) for t, x in zip(t_inv, left)]

        uw = [mm(t, jnp.concatenate([stack(v_ * b_), stack(kb_ * e)], axis=1))
              for t, v_, b_, kb_, e in zip(t_inv, v, beta, kb, exp_gc)]
        q_dec = [a * e for a, e in zip(q, exp_gc)]
        gc_last = [x[c - 1:c, :] for x in gc]
        k_dec_t = [(k_ * jnp.exp(gl - g_)).T for k_, gl, g_ in zip(k, gc_last, gc)]

        for first in range(0, len(items), len(units)):
            sel = slice(first, first + len(units))
            ws = [mm(jnp.concatenate([x[:, LANES:], qd], axis=0), st)
                  for x, qd, st in zip(uw[sel], q_dec[sel], states)]
            v_new = [x[:, :LANES] - y[:c] for x, y in zip(uw[sel], ws)]
            o_n = [y[c:] + mm(a, stack(vn)) for y, a, vn in zip(ws, attn[sel], v_new)]
            for lst, x in zip(outs, o_n):
                lst.append(x)
            states = [st * jnp.exp(gl) + jnp.where(same_head, mm(kt, vn), 0.0)
                      for st, gl, kt, vn in zip(states, gc_last[sel], k_dec_t[sel], v_new)]

    for i, (b, p) in enumerate(units):
        lanes = slice(p * LANES, (p + 1) * LANES)
        s_ref[i] = states[i]
        o = jnp.concatenate(outs[i], axis=0)
        ms = head_sums(o * o) * (1.0 / hd)
        z = z_ref[b, :, lanes].astype(F32)
        o_ref[b, :, lanes] = (o * lax.rsqrt(ms + RMS_EPS) * onorm_ref[0:1, lanes] * (z * _sigmoid(z))).astype(BF16)


def _gdn(qkv, ab, z, alog_lanes, dtb_lanes, onorm_lanes, batch, seq):
    qkv3 = qkv.reshape(batch, seq, 3 * DN_WIDTH)

    def tile(width, col):
        return pl.BlockSpec((batch, GDN_ROWS, width), lambda t: (0, t, col))

    return pl.pallas_call(
        functools.partial(_gdn_kernel, batch=batch),
        grid=(seq // GDN_ROWS,),
        in_specs=[tile(DN_WIDTH, 0), tile(DN_WIDTH, 1), tile(DN_WIDTH, 2), tile(LANES, 0), tile(DN_WIDTH, 0),
                  _resident(alog_lanes.shape), _resident(dtb_lanes.shape), _resident(onorm_lanes.shape)],
        out_specs=tile(DN_WIDTH, 0),
        out_shape=jax.ShapeDtypeStruct((batch, seq, DN_WIDTH), BF16),
        scratch_shapes=[pltpu.VMEM((batch * DN_PAIRS, LANES, LANES), F32)],
        compiler_params=pltpu.CompilerParams(dimension_semantics=("arbitrary",),
                                             vmem_limit_bytes=VMEM_LIMIT_BYTES),
        name="gdn",
    )(qkv3, qkv3, qkv3, ab.reshape(batch, seq, LANES), z.reshape(batch, seq, DN_WIDTH),
      alog_lanes, dtb_lanes, onorm_lanes).reshape(batch * seq, DN_WIDTH)


def _attn_kernel(off_ref, coef_ref, q_ref, k_ref, v_ref, qcols_ref, kcols_ref, vcols_ref, lam_ref, subln_ref,
                 o_ref, qa_ref, m_ref, acc_ref, k2max_ref, *, lam_init):
    h = pl.program_id(1)
    i = pl.program_id(2)
    d = DIFF_HEAD_DIM
    tq = ATTN_Q
    tk = ATTN_K

    q = q_ref[...]
    first = lax.broadcasted_iota(jnp.int32, q.shape, 1) < d
    qcols = jnp.broadcast_to(qcols_ref[0, 0:1, :], (tq, LANES))
    qa_ref[0:tq, :] = jnp.concatenate([jnp.where(first, q, 0), qcols], axis=1)
    qa_ref[tq:2 * tq, :] = jnp.concatenate([jnp.where(first, 0, q), qcols], axis=1)
    m_ref[...] = jnp.full(m_ref.shape, -jnp.inf, F32)
    acc_ref[...] = jnp.zeros(acc_ref.shape, F32)
    kcols = kcols_ref[...]
    vcols = vcols_ref[...]

    def max_sq_norm(x):
        sq = x.astype(F32) * x.astype(F32)
        r = lax.broadcasted_iota(jnp.int32, (LANES, LANES), 0)
        c = lax.broadcasted_iota(jnp.int32, (LANES, LANES), 1)
        same_half = jnp.where((r < d) == (c < d), 1.0, 0.0).astype(BF16)
        half_sums = _dot(sq.astype(BF16), same_half)
        return jnp.max(jnp.max(half_sums, axis=0, keepdims=True), axis=1, keepdims=True)

    @pl.when(i == 0)
    def _():
        k2max_ref[...] = jnp.broadcast_to(max_sq_norm(k_ref[...]), k2max_ref.shape)

    qk_bound = jnp.sqrt(max_sq_norm(q) * k2max_ref[0:1, 0:1]) * NORM_SLACK

    def kv_steps(blocks, live=None):
        scores, values, offs = [], [], []
        m_low = None
        for j, span, diag in blocks:
            start = pl.multiple_of(j * tk, tk)
            keys = span * tk
            ka = jnp.concatenate([k_ref[pl.ds(start, keys), :], kcols[:keys]], axis=1)
            rg = ATTN_ROW_GROUP if diag is None else ATTN_DIAG_GROUP
            groups = range(0, 2 * tq, rg)
            widths = [keys if diag is None else keys - tk + (r0 % tq) + rg for r0 in groups]
            scores.append([(r0, _dot_nt(qa_ref[r0:r0 + rg, :], ka[:w])) for r0, w in zip(groups, widths)])
            values.append(jnp.concatenate([v_ref[pl.ds(start, keys), :], vcols[:keys]], axis=1))
            offs.append(off_ref[h] * j.astype(F32))
        for idx, ((_, _, diag), block_scores, va, off) in enumerate(zip(blocks, scores, values, offs)):
            for r0, s in block_scores:
                rg, width = s.shape
                rows = slice(r0, r0 + rg)
                if diag is not None:
                    qrow = lax.broadcasted_iota(jnp.int32, (rg, width), 0) + ((r0 % tq) + diag)
                    kcol = lax.broadcasted_iota(jnp.int32, (rg, width), 1)
                    s = jnp.where(kcol <= qrow, s, -jnp.inf)
                elif live is not None:
                    s = jnp.where(live, s, -jnp.inf)
                m_old = m_ref[rows, :]
                m_blk = jnp.broadcast_to(jnp.max(s, axis=-1, keepdims=True), (rg, LANES)) + off
                m_new = jnp.maximum(m_old, m_blk)
                alpha = jnp.exp2(m_old - m_new)
                shift = m_new - off
                p = jnp.concatenate([jnp.exp2((s[:, c0:c0 + LANES] - shift).astype(BF16))
                                     for c0 in range(0, width, LANES)], axis=1)
                acc_ref[rows, :] = (jnp.concatenate([alpha, alpha], axis=1) * acc_ref[rows, :]
                                    + _dot(p, va[:width]))
                m_ref[rows, :] = m_new
                if idx == 0:
                    low = jnp.min(m_new, axis=0, keepdims=True)
                    m_low = low if m_low is None else jnp.minimum(m_low, low)
        return m_low[:, 0:1]

    prev = jnp.maximum(i - 1, 0)
    m_low = kv_steps([(i, 1, 0), (prev, 1, None)], live=i > 0)

    reach = (m_low - SKIP_LOG2 - qk_bound) / off_ref[h]
    j0 = jnp.clip(jnp.floor(reach), 0.0, prev.astype(F32)).astype(jnp.int32)[0, 0]

    def body4(t, carry):
        kv_steps([(j0 + 4 * t, ATTN_SPAN, None), (j0 + 4 * t + 2, ATTN_SPAN, None)])
        return carry

    def body(t, carry):
        kv_steps([(j0 + 2 * t, ATTN_SPAN, None)])
        return carry

    n_past = prev - j0
    n_quads = n_past // 4
    lax.fori_loop(0, n_quads, body4, 0)
    lax.fori_loop(2 * n_quads, n_past // 2, body, 0)

    @pl.when(n_past % 2 == 1)
    def _():
        kv_steps([(prev - 1, 1, None)])

    lam_rows = lam_ref[...]
    lam = (jnp.exp(jnp.sum(lam_rows[0:1] * lam_rows[1:2], axis=-1, keepdims=True))
           - jnp.exp(jnp.sum(lam_rows[2:3] * lam_rows[3:4], axis=-1, keepdims=True)) + lam_init)
    acc1 = acc_ref[0:tq, :]
    acc2 = acc_ref[tq:2 * tq, :]
    o = acc1[:, :LANES] / acc1[:, LANES:] - lam * (acc2[:, :LANES] / acc2[:, LANES:])
    o_ref[...] = (_rms(o, subln_ref[...], SUBLN_EPS) * (1.0 - lam_init)).astype(BF16)


def _attn(slopes, df, lam_rows, subln, batch, seq, lam_init):
    span_keys = ATTN_SPAN * ATTN_K
    assert ATTN_Q == ATTN_K and span_keys <= 256 * POS_LO
    assert (2 * ATTN_Q) % ATTN_ROW_GROUP == 0 and ATTN_Q % ATTN_DIAG_GROUP == 0
    nq = seq // ATTN_Q
    df3 = df.reshape(batch, seq, 3 * DIFF_WIDTH)
    coef = slopes * LOG2E
    c_parts = jnp.stack(_split_bf16(coef, 3), axis=1)
    qcols = jnp.zeros((DIFF_HEADS, SUBLANES, LANES), BF16).at[:, :, 0:6].set(
        jnp.broadcast_to(jnp.tile(c_parts, (1, 2))[:, None, :], (DIFF_HEADS, SUBLANES, 6)))
    pos = np.arange(span_keys)
    kcols_np = np.zeros((span_keys, LANES), np.float32)
    kcols_np[:, 0:3] = (pos // POS_LO * POS_LO)[:, None]
    kcols_np[:, 3:6] = (pos % POS_LO)[:, None]
    kcols = jnp.asarray(kcols_np, BF16)
    vcols = jnp.ones((span_keys, LANES), BF16)
    block_off = coef * ATTN_K

    seq_block = lambda col0: pl.BlockSpec((None, seq, LANES), lambda b, h, i: (b, 0, col0 + h))
    const = lambda shape: pl.BlockSpec(shape, lambda b, h, i: (0,) * len(shape))
    return pl.pallas_call(
        functools.partial(_attn_kernel, lam_init=lam_init),
        grid=(batch, DIFF_HEADS, nq),
        in_specs=[pl.BlockSpec(memory_space=pltpu.SMEM), pl.BlockSpec(memory_space=pltpu.SMEM),
                  pl.BlockSpec((None, ATTN_Q, LANES), lambda b, h, i: (b, i, h)),
                  seq_block(DIFF_HEADS), seq_block(2 * DIFF_HEADS),
                  pl.BlockSpec((1, SUBLANES, LANES), lambda b, h, i: (h, 0, 0)),
                  const((span_keys, LANES)), const((span_keys, LANES)),
                  const((SUBLANES, LANES)), const((1, DIFF_V_DIM))],
        out_specs=pl.BlockSpec((None, ATTN_Q, LANES), lambda b, h, i: (b, i, h)),
        out_shape=jax.ShapeDtypeStruct((batch, seq, DIFF_WIDTH), BF16),
        scratch_shapes=[pltpu.VMEM((2 * ATTN_Q, 2 * LANES), BF16),
                        pltpu.VMEM((2 * ATTN_Q, LANES), F32),
                        pltpu.VMEM((2 * ATTN_Q, 2 * LANES), F32),
                        pltpu.VMEM((SUBLANES, LANES), F32)],
        compiler_params=pltpu.CompilerParams(dimension_semantics=("arbitrary", "arbitrary", "arbitrary"),
                                             vmem_limit_bytes=VMEM_LIMIT_BYTES),
        name="diff_attn",
    )(block_off, coef, df3, df3, df3, qcols, kcols, vcols, lam_rows, subln).reshape(batch * seq, DIFF_WIDTH)


def _merge_ffn_kernel(x_ref, g_ref, oa_ref, ob_ref, wa_ref, wb_ref, wgate_ref, wout_ref,
                      fg_norm_ref, wg_ref, wu_ref, wd_ref, fg_ref, o_ref, *, final_norm):
    x = x_ref[...]
    h = _rms(x, g_ref[...], RMS_EPS).astype(BF16)
    gates = _sigmoid(_dot(h, wgate_ref[...]))
    y_a = _dot(oa_ref[...], wa_ref[...])
    y_b = _dot(ob_ref[...], wb_ref[...])
    mixed = (gates[:, :D_MODEL] * y_a + gates[:, D_MODEL:] * y_b).astype(BF16)
    x = x + _dot(mixed, wout_ref[...])
    o_ref[...] = _ffn_half_step(x, fg_norm_ref, wg_ref, wu_ref, wd_ref, fg_ref, final_norm)


def _merge_ffn(x1, g, o_a, o_b, wa, wb, wgate, wout, ffn_g, wg, wu, wd, fg, final_norm):
    n = x1.shape[0]
    row = lambda w: pl.BlockSpec((MERGE_ROWS, w), lambda i: (i, 0))
    return pl.pallas_call(
        functools.partial(_merge_ffn_kernel, final_norm=final_norm),
        grid=(n // MERGE_ROWS,),
        in_specs=[row(D_MODEL), _resident((1, D_MODEL)), row(DN_WIDTH), row(DIFF_WIDTH),
                  _resident(wa.shape), _resident(wb.shape), _resident(wgate.shape), _resident(wout.shape),
                  _resident((1, D_MODEL)), _resident(wg.shape), _resident(wu.shape), _resident(wd.shape),
                  _resident((1, D_MODEL))],
        out_specs=row(D_MODEL),
        out_shape=jax.ShapeDtypeStruct((n, D_MODEL), F32),
        compiler_params=pltpu.CompilerParams(dimension_semantics=("arbitrary",),
                                             vmem_limit_bytes=VMEM_LIMIT_BYTES),
        name="merge_ffn",
    )(x1, g, o_a, o_b, wa, wb, wgate, wout, ffn_g, wg, wu, wd, fg)


def _lambda_init(layer_idx):
    return 0.8 - 0.6 * math.exp(-0.3 * layer_idx)


def _head_lanes(per_head):
    lanes = jnp.repeat(per_head.astype(F32), DN_HEAD_DIM)
    return jnp.broadcast_to(lanes[None, :], (SUBLANES, DN_WIDTH))


def kernel(x, ffn1_norm, ffn1_w_gate, ffn1_w_up, ffn1_w_down, mix_norm, w_in, conv_qkv, dn_a_log, dn_dt_bias, dn_out_norm, diff_lambda_q1, diff_lambda_k1, diff_lambda_q2, diff_lambda_k2, diff_subln, w_branch_a, w_branch_b, w_out, ffn2_norm, ffn2_w_gate, ffn2_w_up, ffn2_w_down, final_norm):
    batch, seq, _ = x.shape
    depth = ffn1_norm.shape[0]
    n = batch * seq
    xs = x.reshape(n, D_MODEL)
    slopes = jnp.asarray([2.0 ** (-8.0 * (i + 1) / DIFF_HEADS) for i in range(DIFF_HEADS)], F32)
    final_g = final_norm.reshape(1, D_MODEL)

    for l in range(depth):
        f1_gate, f1_up, f1_down = _to_bf16(ffn1_w_gate[l], ffn1_w_up[l], ffn1_w_down[l])
        xs, (f2_gate, f2_up, f2_down, wa, wb, wout), (wqkv, wz, wab, wdfq, wdfkv, wgates) = _ffn(
            xs, ffn1_norm[l].reshape(1, D_MODEL), f1_gate, f1_up, f1_down, final_g, False,
            (ffn2_w_gate[l], ffn2_w_up[l], ffn2_w_down[l], w_branch_a[l], w_branch_b[l], w_out[l]), w_in, l)

        qkv, z, ab, df = _proj(xs, mix_norm[l].reshape(1, D_MODEL), wqkv, wz, wab, wdfq, wdfkv, conv_qkv[l], seq)

        o_a = _gdn(qkv, ab, z, _head_lanes(dn_a_log[l]), _head_lanes(dn_dt_bias[l]),
                   jnp.broadcast_to(jnp.tile(dn_out_norm[l].astype(F32), DN_HEADS)[None, :], (SUBLANES, DN_WIDTH)),
                   batch, seq)

        lam_rows = jnp.zeros((SUBLANES, LANES), F32).at[0:4, :DIFF_HEAD_DIM].set(
            jnp.stack([diff_lambda_q1[l], diff_lambda_k1[l], diff_lambda_q2[l], diff_lambda_k2[l]]).astype(F32))
        o_b = _attn(slopes, df, lam_rows, diff_subln[l].reshape(1, DIFF_V_DIM), batch, seq, _lambda_init(l))

        xs = _merge_ffn(xs, mix_norm[l].reshape(1, D_MODEL), o_a, o_b, wa, wb, wgates, wout,
                        ffn2_norm[l].reshape(1, D_MODEL), f2_gate, f2_up, f2_down, final_g,
                        final_norm=(l == depth - 1))

    return xs.reshape(batch, seq, D_MODEL)
```

```python
import functools
import math

import jax
import jax.numpy as jnp
import numpy as np
from jax import lax
from jax.experimental import pallas as pl
from jax.experimental.pallas import tpu as pltpu

F32 = jnp.float32
BF16 = jnp.bfloat16

D_MODEL = 1024
D_FF = 2816
RMS_EPS = 1e-6
SUBLN_EPS = 1e-5
L2_EPS = 1e-6
CONV_WIDTH = 4
DN_HEADS = 8
DN_HEAD_DIM = 64
DN_WIDTH = DN_HEADS * DN_HEAD_DIM
DN_CHUNK = 64
DN_PAIRS = DN_HEADS // 2
DIFF_HEADS = 4
DIFF_HEAD_DIM = 64
DIFF_V_DIM = 2 * DIFF_HEAD_DIM
DIFF_WIDTH = DIFF_HEADS * DIFF_V_DIM
LANES = 128
SUBLANES = 8
LOG2E = math.log2(math.e)

VMEM_LIMIT_BYTES = 56 * 1024 * 1024

FFN_ROWS = 512
FF_CHUNKS = ((0, 512), (512, 1024), (1024, 1536), (1536, 2048), (2048, 2560), (2560, 2816))
PROJ_ROWS = 1024
GDN_ROWS = 512
GDN_PREP_CHUNKS = 2
ATTN_Q = 512
ATTN_K = 512
ATTN_SPAN = 2
ATTN_ROW_GROUP = 512
ATTN_DIAG_GROUP = 256
MERGE_ROWS = 512
CAST_STEPS = 8
POS_LO = 16
SKIP_LOG2 = 160.0
NORM_SLACK = 1.01


def _rms(x, g, eps):
    return x * lax.rsqrt(jnp.mean(x * x, axis=-1, keepdims=True) + eps) * g


def _sigmoid(x):
    return 0.5 * jnp.tanh(0.5 * x) + 0.5


def _dot(a, b):
    return jnp.dot(a, b, preferred_element_type=F32)


def _dot_nt(a, b):
    return lax.dot_general(a, b, (((1,), (1,)), ((), ())), preferred_element_type=F32)


def _split_bf16(x, terms):
    parts = []
    rest = x
    for t in range(terms):
        part = rest.astype(BF16)
        parts.append(part)
        if t + 1 < terms:
            rest = rest - part.astype(F32)
    return parts


def _resident(shape):
    return pl.BlockSpec(shape, lambda *_: (0,) * len(shape), pipeline_mode=pl.Buffered(1))


def _cast_kernel(*refs):
    half = len(refs) // 2
    for src, dst in zip(refs[:half], refs[half:]):
        dst[...] = src[...].astype(BF16)


def _to_bf16(*arrays):
    specs = [pl.BlockSpec((a.shape[0] // CAST_STEPS, a.shape[1]), lambda i: (i, 0)) for a in arrays]
    return pl.pallas_call(
        _cast_kernel,
        grid=(CAST_STEPS,),
        in_specs=specs,
        out_specs=specs,
        out_shape=[jax.ShapeDtypeStruct(a.shape, BF16) for a in arrays],
        compiler_params=pltpu.CompilerParams(dimension_semantics=("arbitrary",),
                                             vmem_limit_bytes=VMEM_LIMIT_BYTES),
        name="cast_weights",
    )(*arrays)


def _split_w_in_kernel(w_ref, qkv_ref, z_ref, ab_ref, dfq_ref, dfkv_ref, gates_ref):
    c_z = 3 * DN_WIDTH
    c_ab = c_z + DN_WIDTH
    c_dfq = c_ab + 2 * DN_HEADS
    c_dfkv = c_dfq + DIFF_WIDTH
    c_gates = c_dfkv + 2 * DIFF_WIDTH
    w = w_ref[...]
    qkv_ref[...] = w[:, :c_z].astype(BF16)
    z_ref[...] = w[:, c_z:c_ab].astype(BF16)
    lane = lax.broadcasted_iota(jnp.int32, (w.shape[0], LANES), 1)
    ab_ref[...] = jnp.where(lane < 2 * DN_HEADS, w[:, c_ab:c_ab + LANES], 0.0).astype(BF16)
    dfq_ref[...] = w[:, c_dfq:c_dfkv].astype(BF16)
    dfkv_ref[...] = w[:, c_dfkv:c_gates].astype(BF16)
    gates_ref[...] = w[:, c_gates:c_gates + 2 * D_MODEL].astype(BF16)


def _ffn_half_step(x, g_ref, wg_ref, wu_ref, wd_ref, fg_ref, final_norm):
    h = _rms(x, g_ref[...], RMS_EPS).astype(BF16)
    acc = None
    for c0, c1 in FF_CHUNKS:
        gate = _dot(h, wg_ref[:, c0:c1])
        up = _dot(h, wu_ref[:, c0:c1])
        act = (gate * _sigmoid(gate) * up).astype(BF16)
        part = _dot(act, wd_ref[c0:c1, :])
        acc = part if acc is None else acc + part
    y = x + 0.5 * acc
    if final_norm:
        y = _rms(y, fg_ref[...], RMS_EPS)
    return y


def _ffn_kernel(*refs, final_norm, n_cast):
    x_ref, g_ref, wg_ref, wu_ref, wd_ref, fg_ref = refs[:6]
    cast_src = refs[6:6 + n_cast]
    w_in_ref = refs[6 + n_cast]
    o_ref = refs[7 + n_cast]
    cast_dst = refs[8 + n_cast:8 + 2 * n_cast]
    split_dst = refs[8 + 2 * n_cast:]
    o_ref[...] = _ffn_half_step(x_ref[...], g_ref, wg_ref, wu_ref, wd_ref, fg_ref, final_norm)
    _cast_kernel(*cast_src, *cast_dst)
    _split_w_in_kernel(w_in_ref, *split_dst)


def _cast_block_rows(rows, steps):
    for block in range(2 * SUBLANES, rows + 1, 2 * SUBLANES):
        if rows % block == 0 and steps % (rows // block) == 0:
            return block
    raise ValueError((rows, steps))


def _ffn(x, g, wg, wu, wd, fg, final_norm, cast, w_in, layer):
    n = x.shape[0]
    steps = n // FFN_ROWS
    row = pl.BlockSpec((FFN_ROWS, D_MODEL), lambda i: (i, 0))

    def cast_spec(a):
        block = _cast_block_rows(a.shape[0], steps)
        reps = steps // (a.shape[0] // block)
        return pl.BlockSpec((block, a.shape[1]), lambda i: (i // reps, 0))

    cast_specs = [cast_spec(a) for a in cast]
    w_rows = _cast_block_rows(w_in.shape[1], steps)
    w_reps = steps // (w_in.shape[1] // w_rows)
    widths = (3 * DN_WIDTH, DN_WIDTH, LANES, DIFF_WIDTH, 2 * DIFF_WIDTH, 2 * D_MODEL)
    outs = pl.pallas_call(
        functools.partial(_ffn_kernel, final_norm=final_norm, n_cast=len(cast)),
        grid=(steps,),
        in_specs=[row, _resident((1, D_MODEL)), _resident((D_MODEL, D_FF)), _resident((D_MODEL, D_FF)),
                  _resident((D_FF, D_MODEL)), _resident((1, D_MODEL))] + cast_specs
                 + [pl.BlockSpec((None, w_rows, w_in.shape[2]), lambda i: (layer, i // w_reps, 0))],
        out_specs=[row] + cast_specs + [pl.BlockSpec((w_rows, c), lambda i: (i // w_reps, 0)) for c in widths],
        out_shape=[jax.ShapeDtypeStruct((n, D_MODEL), F32)]
                  + [jax.ShapeDtypeStruct(a.shape, BF16) for a in cast]
                  + [jax.ShapeDtypeStruct((w_in.shape[1], c), BF16) for c in widths],
        compiler_params=pltpu.CompilerParams(dimension_semantics=("arbitrary",),
                                             vmem_limit_bytes=VMEM_LIMIT_BYTES),
        name="ffn",
    )(x, g, wg, wu, wd, fg, *cast, w_in)
    return outs[0], outs[1:1 + len(cast)], outs[1 + len(cast):]


def _proj_kernel(x_ref, g_ref, wqkv_ref, wz_ref, wab_ref, wdfq_ref, wdfkv_ref, cw_ref,
                 qkv_ref, z_ref, ab_ref, df_ref, ext_ref, *, tiles_per_seq):
    halo = SUBLANES

    @pl.when(pl.program_id(0) % tiles_per_seq == 0)
    def _():
        ext_ref[0:halo, :] = jnp.zeros((halo, 3 * DN_WIDTH), F32)

    h = _rms(x_ref[...], g_ref[...], RMS_EPS).astype(BF16)
    pre = _dot(h, wqkv_ref[...])
    ext_ref[halo:halo + PROJ_ROWS, :] = pre
    cw = cw_ref[...]
    y = cw[CONV_WIDTH - 1:CONV_WIDTH, :] * pre
    for j in range(CONV_WIDTH - 1):
        y = y + cw[j:j + 1, :] * ext_ref[pl.ds(halo - (CONV_WIDTH - 1) + j, PROJ_ROWS), :]
    ext_ref[0:halo, :] = ext_ref[PROJ_ROWS:PROJ_ROWS + halo, :]
    qkv_ref[...] = (y * _sigmoid(y)).astype(BF16)
    z_ref[...] = _dot(h, wz_ref[...]).astype(BF16)
    ab_ref[...] = _dot(h, wab_ref[...])
    df_ref[:, :DIFF_WIDTH] = (_dot(h, wdfq_ref[...]) * (DIFF_HEAD_DIM ** -0.5 * LOG2E)).astype(BF16)
    df_ref[:, DIFF_WIDTH:] = _dot(h, wdfkv_ref[...]).astype(BF16)


def _proj(x1, g, wqkv, wz, wab, wdfq, wdfkv, cw, seq):
    n = x1.shape[0]
    row = lambda w: pl.BlockSpec((PROJ_ROWS, w), lambda i: (i, 0))
    return pl.pallas_call(
        functools.partial(_proj_kernel, tiles_per_seq=seq // PROJ_ROWS),
        grid=(n // PROJ_ROWS,),
        in_specs=[row(D_MODEL), _resident((1, D_MODEL)), _resident(wqkv.shape), _resident(wz.shape),
                  _resident(wab.shape), _resident(wdfq.shape), _resident(wdfkv.shape), _resident(cw.shape)],
        out_specs=[row(3 * DN_WIDTH), row(DN_WIDTH), row(LANES), row(3 * DIFF_WIDTH)],
        out_shape=[jax.ShapeDtypeStruct((n, 3 * DN_WIDTH), BF16),
                   jax.ShapeDtypeStruct((n, DN_WIDTH), BF16),
                   jax.ShapeDtypeStruct((n, LANES), F32),
                   jax.ShapeDtypeStruct((n, 3 * DIFF_WIDTH), BF16)],
        scratch_shapes=[pltpu.VMEM((PROJ_ROWS + SUBLANES, 3 * DN_WIDTH), F32)],
        compiler_params=pltpu.CompilerParams(dimension_semantics=("arbitrary",),
                                             vmem_limit_bytes=VMEM_LIMIT_BYTES),
        name="proj",
    )(x1, g, wqkv, wz, wab, wdfq, wdfkv, cw)


def _gdn_kernel(q_ref, k_ref, v_ref, ab_ref, z_ref, alog_ref, dtb_ref, onorm_ref, o_ref, s_ref, *, batch):
    c = DN_CHUNK
    hd = DN_HEAD_DIM
    units = [(b, p) for b in range(batch) for p in range(DN_PAIRS)]

    @pl.when(pl.program_id(0) == 0)
    def _():
        s_ref[...] = jnp.zeros(s_ref.shape, F32)

    row = lax.broadcasted_iota(jnp.int32, (c, LANES), 0)
    lane = lax.broadcasted_iota(jnp.int32, (c, LANES), 1)
    col = lane & (hd - 1)
    even_bf = jnp.where(lane < hd, 1.0, 0.0).astype(BF16)
    odd_bf = jnp.where(lane < hd, 0.0, 1.0).astype(BF16)
    incl = col <= row
    strict = col < row
    same16 = (row >> 4) == (col >> 4)
    same32 = (row >> 5) == (col >> 5)
    eye = jnp.where(col == row, 1.0, 0.0).astype(F32)
    r2 = lax.broadcasted_iota(jnp.int32, (LANES, LANES), 0)
    l2 = lax.broadcasted_iota(jnp.int32, (LANES, LANES), 1)
    same_head = (r2 >> 6) == (l2 >> 6)
    head_ones = jnp.where(same_head, 1.0, 0.0).astype(BF16)
    head_ones2 = jnp.concatenate([head_ones, head_ones], axis=0)
    r3 = lax.broadcasted_iota(jnp.int32, (3 * LANES, LANES), 0) & (LANES - 1)
    l3 = lax.broadcasted_iota(jnp.int32, (3 * LANES, LANES), 1)
    rc = lax.broadcasted_iota(jnp.int32, (c, 3 * c), 0)
    cc = lax.broadcasted_iota(jnp.int32, (c, 3 * c), 1) & (c - 1)
    lower3 = jnp.where(cc <= rc, 1.0, 0.0).astype(BF16)
    ones3 = jnp.ones((c, 3 * c), BF16)

    def stack(x):
        xb = x.astype(BF16)
        return jnp.concatenate([xb * even_bf, xb * odd_bf], axis=0)

    def mm(a, b):
        return _dot(a.astype(BF16), b.astype(BF16))

    def mm_nt(a, b):
        return _dot_nt(a.astype(BF16), b.astype(BF16))

    def head_sums(x):
        return _dot(jnp.concatenate(_split_bf16(x, 2), axis=1), head_ones2)

    def exact_rows(lhs3, x):
        return _dot(lhs3, jnp.concatenate(_split_bf16(x, 3), axis=0))

    ab3 = [jnp.concatenate(_split_bf16(ab_ref[b], 3), axis=1) for b in range(batch)]
    q_t, k_t, v_t, g_t, beta_t = [], [], [], [], []
    for b, p in units:
        lanes = slice(p * LANES, (p + 1) * LANES)
        expand_a = jnp.where(r3 == 2 * p + (l3 >> 6), 1.0, 0.0).astype(BF16)
        expand_b = jnp.where(r3 == DN_HEADS + 2 * p + (l3 >> 6), 1.0, 0.0).astype(BF16)
        sp_in = _dot(ab3[b], expand_a) + dtb_ref[0:1, lanes]
        softplus = jnp.maximum(sp_in, 0.0) + jnp.log(1.0 + jnp.exp(-jnp.abs(sp_in)))
        g_t.append(-jnp.exp(alog_ref[0:1, lanes]) * softplus)
        beta_t.append(_sigmoid(_dot(ab3[b], expand_b)))
        q = q_ref[b, :, lanes].astype(F32)
        k = k_ref[b, :, lanes].astype(F32)
        q_t.append(q * lax.rsqrt(head_sums(q * q) + L2_EPS) * (hd ** -0.5))
        k_t.append(k * lax.rsqrt(head_sums(k * k) + L2_EPS))
        v_t.append(v_ref[b, :, lanes].astype(F32))

    states = [s_ref[i] for i in range(len(units))]
    outs = [[] for _ in units]
    for n0 in range(0, GDN_ROWS // c, GDN_PREP_CHUNKS):
        items = [(u, n) for n in range(n0, n0 + GDN_PREP_CHUNKS) for u in range(len(units))]

        def take(per_unit):
            return [per_unit[u][n * c:(n + 1) * c] for u, n in items]

        q = take(q_t)
        k = take(k_t)
        v = take(v_t)
        beta = take(beta_t)
        gc = [exact_rows(lower3, x) for x in take(g_t)]
        gc_row = [exact_rows(ones3, x * eye) for x in gc]
        decay = [jnp.where(incl, jnp.exp(jnp.where(incl, a - b_, 0.0)), 0.0) for a, b_ in zip(gc, gc_row)]
        exp_gc = [jnp.exp(x) for x in gc]
        kb = [a * b_ for a, b_ in zip(k, beta)]
        kq = [mm_nt(jnp.concatenate([a, b_], axis=0), stack(k_)) for a, b_, k_ in zip(kb, q, k)]
        a_mat = [jnp.where(strict, x[:c] * d, 0.0) for x, d in zip(kq, decay)]
        attn = [x[c:] * d for x, d in zip(kq, decay)]

        neg_d = [jnp.where(strict & same16, -x, 0.0) for x in a_mat]
        t_inv = [eye + x for x in neg_d]
        power = [mm(x, stack(x)) for x in neg_d]
        for _ in range(2):
            both = [mm(jnp.concatenate([pw, t], axis=0), stack(pw)) for pw, t in zip(power, t_inv)]
            power = [x[:c] for x in both]
            t_inv = [t + x[c:] for t, x in zip(t_inv, both)]
        t_inv = [t + mm(t, stack(pw)) for t, pw in zip(t_inv, power)]
        for block in (strict & same32 & ~same16, strict & ~same32):
            left = [mm(t, stack(jnp.where(block, x, 0.0))) for t, x in zip(t_inv, a_mat)]
            t_inv = [t - mm(x, stack(t)) for t, x in zip(t_inv, left)]

        uw = [mm(t, jnp.concatenate([stack(v_ * b_), stack(kb_ * e)], axis=1))
              for t, v_, b_, kb_, e in zip(t_inv, v, beta, kb, exp_gc)]
        q_dec = [a * e for a, e in zip(q, exp_gc)]
        gc_last = [x[c - 1:c, :] for x in gc]
        k_dec_t = [(k_ * jnp.exp(gl - g_)).T for k_, gl, g_ in zip(k, gc_last, gc)]

        for first in range(0, len(items), len(units)):
            sel = slice(first, first + len(units))
            ws = [mm(jnp.concatenate([x[:, LANES:], qd], axis=0), st)
                  for x, qd, st in zip(uw[sel], q_dec[sel], states)]
            v_new = [x[:, :LANES] - y[:c] for x, y in zip(uw[sel], ws)]
            o_n = [y[c:] + mm(a, stack(vn)) for y, a, vn in zip(ws, attn[sel], v_new)]
            for lst, x in zip(outs, o_n):
                lst.append(x)
            states = [st * jnp.exp(gl) + jnp.where(same_head, mm(kt, vn), 0.0)
                      for st, gl, kt, vn in zip(states, gc_last[sel], k_dec_t[sel], v_new)]

    for i, (b, p) in enumerate(units):
        lanes = slice(p * LANES, (p + 1) * LANES)
        s_ref[i] = states[i]
        o = jnp.concatenate(outs[i], axis=0)
        ms = head_sums(o * o) * (1.0 / hd)
        z = z_ref[b, :, lanes].astype(F32)
        o_ref[b, :, lanes] = (o * lax.rsqrt(ms + RMS_EPS) * onorm_ref[0:1, lanes] * (z * _sigmoid(z))).astype(BF16)


def _gdn(qkv, ab, z, alog_lanes, dtb_lanes, onorm_lanes, batch, seq):
    qkv3 = qkv.reshape(batch, seq, 3 * DN_WIDTH)

    def tile(width, col):
        return pl.BlockSpec((batch, GDN_ROWS, width), lambda t: (0, t, col))

    return pl.pallas_call(
        functools.partial(_gdn_kernel, batch=batch),
        grid=(seq // GDN_ROWS,),
        in_specs=[tile(DN_WIDTH, 0), tile(DN_WIDTH, 1), tile(DN_WIDTH, 2), tile(LANES, 0), tile(DN_WIDTH, 0),
                  _resident(alog_lanes.shape), _resident(dtb_lanes.shape), _resident(onorm_lanes.shape)],
        out_specs=tile(DN_WIDTH, 0),
        out_shape=jax.ShapeDtypeStruct((batch, seq, DN_WIDTH), BF16),
        scratch_shapes=[pltpu.VMEM((batch * DN_PAIRS, LANES, LANES), F32)],
        compiler_params=pltpu.CompilerParams(dimension_semantics=("arbitrary",),
                                             vmem_limit_bytes=VMEM_LIMIT_BYTES),
        name="gdn",
    )(qkv3, qkv3, qkv3, ab.reshape(batch, seq, LANES), z.reshape(batch, seq, DN_WIDTH),
      alog_lanes, dtb_lanes, onorm_lanes).reshape(batch * seq, DN_WIDTH)


def _attn_kernel(off_ref, coef_ref, q_ref, k_ref, v_ref, qcols_ref, kcols_ref, vcols_ref, lam_ref, subln_ref,
                 o_ref, qa_ref, m_ref, acc_ref, k2max_ref, *, lam_init):
    h = pl.program_id(1)
    i = pl.program_id(2)
    d = DIFF_HEAD_DIM
    tq = ATTN_Q
    tk = ATTN_K

    q = q_ref[...]
    first = lax.broadcasted_iota(jnp.int32, q.shape, 1) < d
    qcols = jnp.broadcast_to(qcols_ref[0, 0:1, :], (tq, LANES))
    qa_ref[0:tq, :] = jnp.concatenate([jnp.where(first, q, 0), qcols], axis=1)
    qa_ref[tq:2 * tq, :] = jnp.concatenate([jnp.where(first, 0, q), qcols], axis=1)
    m_ref[...] = jnp.full(m_ref.shape, -jnp.inf, F32)
    acc_ref[...] = jnp.zeros(acc_ref.shape, F32)
    kcols = kcols_ref[...]
    vcols = vcols_ref[...]

    def max_sq_norm(x):
        sq = x.astype(F32) * x.astype(F32)
        r = lax.broadcasted_iota(jnp.int32, (LANES, LANES), 0)
        c = lax.broadcasted_iota(jnp.int32, (LANES, LANES), 1)
        same_half = jnp.where((r < d) == (c < d), 1.0, 0.0).astype(BF16)
        half_sums = _dot(sq.astype(BF16), same_half)
        return jnp.max(jnp.max(half_sums, axis=0, keepdims=True), axis=1, keepdims=True)

    @pl.when(i == 0)
    def _():
        k2max_ref[...] = jnp.broadcast_to(max_sq_norm(k_ref[...]), k2max_ref.shape)

    qk_bound = jnp.sqrt(max_sq_norm(q) * k2max_ref[0:1, 0:1]) * NORM_SLACK

    def kv_steps(blocks, live=None):
        scores, values, offs = [], [], []
        m_low = None
        for j, span, diag in blocks:
            start = pl.multiple_of(j * tk, tk)
            keys = span * tk
            ka = jnp.concatenate([k_ref[pl.ds(start, keys), :], kcols[:keys]], axis=1)
            rg = ATTN_ROW_GROUP if diag is None else ATTN_DIAG_GROUP
            groups = range(0, 2 * tq, rg)
            widths = [keys if diag is None else keys - tk + (r0 % tq) + rg for r0 in groups]
            scores.append([(r0, _dot_nt(qa_ref[r0:r0 + rg, :], ka[:w])) for r0, w in zip(groups, widths)])
            values.append(jnp.concatenate([v_ref[pl.ds(start, keys), :], vcols[:keys]], axis=1))
            offs.append(off_ref[h] * j.astype(F32))
        for idx, ((_, _, diag), block_scores, va, off) in enumerate(zip(blocks, scores, values, offs)):
            for r0, s in block_scores:
                rg, width = s.shape
                rows = slice(r0, r0 + rg)
                if diag is not None:
                    qrow = lax.broadcasted_iota(jnp.int32, (rg, width), 0) + ((r0 % tq) + diag)
                    kcol = lax.broadcasted_iota(jnp.int32, (rg, width), 1)
                    s = jnp.where(kcol <= qrow, s, -jnp.inf)
                elif live is not None:
                    s = jnp.where(live, s, -jnp.inf)
                m_old = m_ref[rows, :]
                m_blk = jnp.broadcast_to(jnp.max(s, axis=-1, keepdims=True), (rg, LANES)) + off
                m_new = jnp.maximum(m_old, m_blk)
                alpha = jnp.exp2(m_old - m_new)
                shift = m_new - off
                p = jnp.concatenate([jnp.exp2((s[:, c0:c0 + LANES] - shift).astype(BF16))
                                     for c0 in range(0, width, LANES)], axis=1)
                acc_ref[rows, :] = (jnp.concatenate([alpha, alpha], axis=1) * acc_ref[rows, :]
                                    + _dot(p, va[:width]))
                m_ref[rows, :] = m_new
                if idx == 0:
                    low = jnp.min(m_new, axis=0, keepdims=True)
                    m_low = low if m_low is None else jnp.minimum(m_low, low)
        return m_low[:, 0:1]

    prev = jnp.maximum(i - 1, 0)
    m_low = kv_steps([(i, 1, 0), (prev, 1, None)], live=i > 0)

    reach = (m_low - SKIP_LOG2 - qk_bound) / off_ref[h]
    j0 = jnp.clip(jnp.floor(reach), 0.0, prev.astype(F32)).astype(jnp.int32)[0, 0]

    def body4(t, carry):
        kv_steps([(j0 + 4 * t, ATTN_SPAN, None), (j0 + 4 * t + 2, ATTN_SPAN, None)])
        return carry

    def body(t, carry):
        kv_steps([(j0 + 2 * t, ATTN_SPAN, None)])
        return carry

    n_past = prev - j0
    n_quads = n_past // 4
    lax.fori_loop(0, n_quads, body4, 0)
    lax.fori_loop(2 * n_quads, n_past // 2, body, 0)

    @pl.when(n_past % 2 == 1)
    def _():
        kv_steps([(prev - 1, 1, None)])

    lam_rows = lam_ref[...]
    lam = (jnp.exp(jnp.sum(lam_rows[0:1] * lam_rows[1:2], axis=-1, keepdims=True))
           - jnp.exp(jnp.sum(lam_rows[2:3] * lam_rows[3:4], axis=-1, keepdims=True)) + lam_init)
    acc1 = acc_ref[0:tq, :]
    acc2 = acc_ref[tq:2 * tq, :]
    o = acc1[:, :LANES] / acc1[:, LANES:] - lam * (acc2[:, :LANES] / acc2[:, LANES:])
    o_ref[...] = (_rms(o, subln_ref[...], SUBLN_EPS) * (1.0 - lam_init)).astype(BF16)


def _attn(slopes, df, lam_rows, subln, batch, seq, lam_init):
    span_keys = ATTN_SPAN * ATTN_K
    assert ATTN_Q == ATTN_K and span_keys <= 256 * POS_LO
    assert (2 * ATTN_Q) % ATTN_ROW_GROUP == 0 and ATTN_Q % ATTN_DIAG_GROUP == 0
    nq = seq // ATTN_Q
    df3 = df.reshape(batch, seq, 3 * DIFF_WIDTH)
    coef = slopes * LOG2E
    c_parts = jnp.stack(_split_bf16(coef, 3), axis=1)
    qcols = jnp.zeros((DIFF_HEADS, SUBLANES, LANES), BF16).at[:, :, 0:6].set(
        jnp.broadcast_to(jnp.tile(c_parts, (1, 2))[:, None, :], (DIFF_HEADS, SUBLANES, 6)))
    pos = np.arange(span_keys)
    kcols_np = np.zeros((span_keys, LANES), np.float32)
    kcols_np[:, 0:3] = (pos // POS_LO * POS_LO)[:, None]
    kcols_np[:, 3:6] = (pos % POS_LO)[:, None]
    kcols = jnp.asarray(kcols_np, BF16)
    vcols = jnp.ones((span_keys, LANES), BF16)
    block_off = coef * ATTN_K

    seq_block = lambda col0: pl.BlockSpec((None, seq, LANES), lambda b, h, i: (b, 0, col0 + h))
    const = lambda shape: pl.BlockSpec(shape, lambda b, h, i: (0,) * len(shape))
    return pl.pallas_call(
        functools.partial(_attn_kernel, lam_init=lam_init),
        grid=(batch, DIFF_HEADS, nq),
        in_specs=[pl.BlockSpec(memory_space=pltpu.SMEM), pl.BlockSpec(memory_space=pltpu.SMEM),
                  pl.BlockSpec((None, ATTN_Q, LANES), lambda b, h, i: (b, i, h)),
                  seq_block(DIFF_HEADS), seq_block(2 * DIFF_HEADS),
                  pl.BlockSpec((1, SUBLANES, LANES), lambda b, h, i: (h, 0, 0)),
                  const((span_keys, LANES)), const((span_keys, LANES)),
                  const((SUBLANES, LANES)), const((1, DIFF_V_DIM))],
        out_specs=pl.BlockSpec((None, ATTN_Q, LANES), lambda b, h, i: (b, i, h)),
        out_shape=jax.ShapeDtypeStruct((batch, seq, DIFF_WIDTH), BF16),
        scratch_shapes=[pltpu.VMEM((2 * ATTN_Q, 2 * LANES), BF16),
                        pltpu.VMEM((2 * ATTN_Q, LANES), F32),
                        pltpu.VMEM((2 * ATTN_Q, 2 * LANES), F32),
                        pltpu.VMEM((SUBLANES, LANES), F32)],
        compiler_params=pltpu.CompilerParams(dimension_semantics=("arbitrary", "arbitrary", "arbitrary"),
                                             vmem_limit_bytes=VMEM_LIMIT_BYTES),
        name="diff_attn",
    )(block_off, coef, df3, df3, df3, qcols, kcols, vcols, lam_rows, subln).reshape(batch * seq, DIFF_WIDTH)


def _merge_ffn_kernel(x_ref, g_ref, oa_ref, ob_ref, wa_ref, wb_ref, wgate_ref, wout_ref,
                      fg_norm_ref, wg_ref, wu_ref, wd_ref, fg_ref, o_ref, *, final_norm):
    x = x_ref[...]
    h = _rms(x, g_ref[...], RMS_EPS).astype(BF16)
    gates = _sigmoid(_dot(h, wgate_ref[...]))
    y_a = _dot(oa_ref[...], wa_ref[...])
    y_b = _dot(ob_ref[...], wb_ref[...])
    mixed = (gates[:, :D_MODEL] * y_a + gates[:, D_MODEL:] * y_b).astype(BF16)
    x = x + _dot(mixed, wout_ref[...])
    o_ref[...] = _ffn_half_step(x, fg_norm_ref, wg_ref, wu_ref, wd_ref, fg_ref, final_norm)


def _merge_ffn(x1, g, o_a, o_b, wa, wb, wgate, wout, ffn_g, wg, wu, wd, fg, final_norm):
    n = x1.shape[0]
    row = lambda w: pl.BlockSpec((MERGE_ROWS, w), lambda i: (i, 0))
    return pl.pallas_call(
        functools.partial(_merge_ffn_kernel, final_norm=final_norm),
        grid=(n // MERGE_ROWS,),
        in_specs=[row(D_MODEL), _resident((1, D_MODEL)), row(DN_WIDTH), row(DIFF_WIDTH),
                  _resident(wa.shape), _resident(wb.shape), _resident(wgate.shape), _resident(wout.shape),
                  _resident((1, D_MODEL)), _resident(wg.shape), _resident(wu.shape), _resident(wd.shape),
                  _resident((1, D_MODEL))],
        out_specs=row(D_MODEL),
        out_shape=jax.ShapeDtypeStruct((n, D_MODEL), F32),
        compiler_params=pltpu.CompilerParams(dimension_semantics=("arbitrary",),
                                             vmem_limit_bytes=VMEM_LIMIT_BYTES),
        name="merge_ffn",
    )(x1, g, o_a, o_b, wa, wb, wgate, wout, ffn_g, wg, wu, wd, fg)


def _lambda_init(layer_idx):
    return 0.8 - 0.6 * math.exp(-0.3 * layer_idx)


def _head_lanes(per_head):
    lanes = jnp.repeat(per_head.astype(F32), DN_HEAD_DIM)
    return jnp.broadcast_to(lanes[None, :], (SUBLANES, DN_WIDTH))


def kernel(x, ffn1_norm, ffn1_w_gate, ffn1_w_up, ffn1_w_down, mix_norm, w_in, conv_qkv, dn_a_log, dn_dt_bias, dn_out_norm, diff_lambda_q1, diff_lambda_k1, diff_lambda_q2, diff_lambda_k2, diff_subln, w_branch_a, w_branch_b, w_out, ffn2_norm, ffn2_w_gate, ffn2_w_up, ffn2_w_down, final_norm):
    batch, seq, _ = x.shape
    depth = ffn1_norm.shape[0]
    n = batch * seq
    xs = x.reshape(n, D_MODEL)
    slopes = jnp.asarray([2.0 ** (-8.0 * (i + 1) / DIFF_HEADS) for i in range(DIFF_HEADS)], F32)
    final_g = final_norm.reshape(1, D_MODEL)

    for l in range(depth):
        f1_gate, f1_up, f1_down = _to_bf16(ffn1_w_gate[l], ffn1_w_up[l], ffn1_w_down[l])
        xs, (f2_gate, f2_up, f2_down, wa, wb, wout), (wqkv, wz, wab, wdfq, wdfkv, wgates) = _ffn(
            xs, ffn1_norm[l].reshape(1, D_MODEL), f1_gate, f1_up, f1_down, final_g, False,
            (ffn2_w_gate[l], ffn2_w_up[l], ffn2_w_down[l], w_branch_a[l], w_branch_b[l], w_out[l]), w_in, l)

        qkv, z, ab, df = _proj(xs, mix_norm[l].reshape(1, D_MODEL), wqkv, wz, wab, wdfq, wdfkv, conv_qkv[l], seq)

        o_a = _gdn(qkv, ab, z, _head_lanes(dn_a_log[l]), _head_lanes(dn_dt_bias[l]),
                   jnp.broadcast_to(jnp.tile(dn_out_norm[l].astype(F32), DN_HEADS)[None, :], (SUBLANES, DN_WIDTH)),
                   batch, seq)

        lam_rows = jnp.zeros((SUBLANES, LANES), F32).at[0:4, :DIFF_HEAD_DIM].set(
            jnp.stack([diff_lambda_q1[l], diff_lambda_k1[l], diff_lambda_q2[l], diff_lambda_k2[l]]).astype(F32))
        o_b = _attn(slopes, df, lam_rows, diff_subln[l].reshape(1, DIFF_V_DIM), batch, seq, _lambda_init(l))

        xs = _merge_ffn(xs, mix_norm[l].reshape(1, D_MODEL), o_a, o_b, wa, wb, wgates, wout,
                        ffn2_norm[l].reshape(1, D_MODEL), f2_gate, f2_up, f2_down, final_g,
                        final_norm=(l == depth - 1))

    return xs.reshape(batch, seq, D_MODEL)
```
